```python
import jax, jax.numpy as jnp
from jax import lax
import numpy as np

D_MODEL = 1024
BATCH = 2
SEQ = 16384
DEPTH = 1
DEC_BATCH = 8
DEC_SEQ = 32
PAST_LEN = 4096

CHUNK = 64
D_MIX = D_MODEL
W_A = D_MIX // 2
W_B = D_MIX // 2
H_A = 8
HD_A = W_A // H_A
H_B = 8
HD_B = W_B // H_B
GMLP_CHUNK = 128
CONV_W = 4
LRU_C = 8.0
D_FF = 4 * D_MODEL
EPS = 1e-6

kernel_name = "hymba_gmlp_rglru_streaming_step"


def _rmsnorm(x, g):
    xf = x.astype(jnp.float32)
    y = xf * lax.rsqrt(jnp.mean(xf * xf, axis=-1, keepdims=True) + EPS)
    return (y * g.astype(jnp.float32)).astype(x.dtype)


def _layernorm(x, g, b):
    xf = x.astype(jnp.float32)
    mu = jnp.mean(xf, axis=-1, keepdims=True)
    var = jnp.mean(jnp.square(xf - mu), axis=-1, keepdims=True)
    y = (xf - mu) * lax.rsqrt(var + EPS)
    return (y * g.astype(jnp.float32) + b.astype(jnp.float32)).astype(x.dtype)


def _spatial_mix(v, w_s, b_s):
    B, T = v.shape[:2]
    w = jnp.tril(w_s)
    if T % GMLP_CHUNK == 0:
        n = T // GMLP_CHUNK
        vc = v.reshape(B, n, GMLP_CHUNK, H_A, HD_A)
        mixed = jnp.einsum('hts,bnshd->bnthd', w, vc) + b_s.T[:, :, None]
        return mixed.reshape(B, T, H_A, HD_A)
    wt = w[:, :T, :T]
    return jnp.einsum('hts,bshd->bthd', wt, v) + b_s[:, :T].T[:, :, None]


def _causal_conv(x, buf, w, b):
    T = x.shape[1]
    xp = jnp.concatenate([buf.astype(x.dtype), x], axis=1)
    y = b
    for k in range(CONV_W):
        y = y + xp[:, k:k + T] * w[k]
    return y, xp[:, -(CONV_W - 1):]


def _rg_lru(x, h0, w_a, b_a, w_x, b_x, lam, reset_first):
    B, T, C = x.shape
    xf = x.astype(jnp.float32)
    xh = xf.reshape(B, T, H_B, HD_B)
    r = jax.nn.sigmoid(jnp.einsum('bthi,hij->bthj', xh, w_a.astype(jnp.float32)).reshape(B, T, C) + b_a.astype(jnp.float32))
    i = jax.nn.sigmoid(jnp.einsum('bthi,hij->bthj', xh, w_x.astype(jnp.float32)).reshape(B, T, C) + b_x.astype(jnp.float32))
    log_a = -LRU_C * r * jax.nn.softplus(-lam.astype(jnp.float32))
    a = jnp.exp(log_a)
    mult = jnp.sqrt(-jnp.expm1(2.0 * log_a))
    if reset_first:
        mult = mult.at[:, 0].set(1.0)
    bterm = mult * (i * xf)
    bterm = bterm.at[:, 0].add(a[:, 0] * h0.astype(jnp.float32))

    def combine(l, rr):
        a1, b1 = l
        a2, b2 = rr
        return a1 * a2, a2 * b1 + b2

    _, h = lax.associative_scan(combine, (a, bterm), axis=1)
    return h.astype(x.dtype), h[:, -1].astype(x.dtype)


def _layer(x, conv_buf, h0, reset_first, norm1_g, w_in, ln_v_g, ln_v_b, w_s, b_s, conv_w, conv_b,
           w_a, b_a, w_x, b_x, lam, gn_a_g, gn_b_g, w_out, norm2_g, w_up, w_down):
    B, T, _ = x.shape
    xn = _rmsnorm(x, norm1_g)
    z = xn @ w_in
    u_a = jax.nn.gelu(z[..., :W_A])
    v_a = _layernorm(jax.nn.gelu(z[..., W_A:2 * W_A]), ln_v_g, ln_v_b)
    x_b = z[..., 2 * W_A:2 * W_A + W_B]
    g_b = z[..., 2 * W_A + W_B:]
    mixed = _spatial_mix(v_a.reshape(B, T, H_A, HD_A), w_s, b_s).reshape(B, T, W_A)
    y_a = u_a * mixed
    xc, conv_tail = _causal_conv(x_b, conv_buf, conv_w, conv_b)
    hs, h_last = _rg_lru(xc, h0, w_a, b_a, w_x, b_x, lam, reset_first)
    y_b = hs * jax.nn.gelu(g_b)
    o = jnp.concatenate([_rmsnorm(y_a, gn_a_g), _rmsnorm(y_b, gn_b_g)], axis=-1) @ w_out
    h = x + o
    hn = _rmsnorm(h, norm2_g)
    h = h + jnp.square(jax.nn.relu(hn @ w_up)) @ w_down
    return h, conv_tail, h_last, v_a


def setup_inputs(seed: int = 0) -> dict:
    key = jax.random.key(seed)
    ks = jax.random.split(key, 24)
    f = jnp.float32
    nrm = lambda k, shp, s: (jax.random.normal(k, shp, f) * s)
    a_c = jax.random.uniform(ks[13], (DEPTH, W_B), f, 0.9, 0.999)
    a0 = a_c ** (1.0 / LRU_C)
    lam = jnp.log(a0) - jnp.log1p(-a0)
    return {
        "x_prompt": nrm(ks[0], (BATCH, SEQ, D_MODEL), 1.0),
        "x_sample": nrm(ks[1], (DEC_BATCH, DEC_SEQ, D_MODEL), 1.0),
        "state_conv_b": nrm(ks[2], (DEPTH, DEC_BATCH, CONV_W - 1, W_B), 1.0),
        "state_h_b": nrm(ks[3], (DEPTH, DEC_BATCH, W_B), 0.5),
        "norm1_g": 1.0 + nrm(ks[4], (DEPTH, D_MODEL), 0.02),
        "w_in": nrm(ks[5], (DEPTH, D_MODEL, 2 * D_MIX), D_MODEL ** -0.5),
        "ln_v_g": 1.0 + nrm(ks[6], (DEPTH, W_A), 0.02),
        "ln_v_b": nrm(ks[7], (DEPTH, W_A), 0.02),
        "w_s": nrm(ks[8], (DEPTH, H_A, GMLP_CHUNK, GMLP_CHUNK), GMLP_CHUNK ** -0.5),
        "b_s": 1.0 + nrm(ks[9], (DEPTH, H_A, GMLP_CHUNK), 0.1),
        "conv_w": nrm(ks[10], (DEPTH, CONV_W, W_B), CONV_W ** -0.5),
        "conv_b": nrm(ks[11], (DEPTH, W_B), 0.02),
        "w_a": nrm(ks[12], (DEPTH, H_B, HD_B, HD_B), HD_B ** -0.5),
        "b_a": nrm(ks[14], (DEPTH, W_B), 0.02),
        "w_x": nrm(ks[15], (DEPTH, H_B, HD_B, HD_B), HD_B ** -0.5),
        "b_x": nrm(ks[16], (DEPTH, W_B), 0.02),
        "lam": lam,
        "gn_a_g": 1.0 + nrm(ks[17], (DEPTH, W_A), 0.02),
        "gn_b_g": 1.0 + nrm(ks[18], (DEPTH, W_B), 0.02),
        "w_out": nrm(ks[19], (DEPTH, D_MIX, D_MODEL), D_MIX ** -0.5),
        "norm2_g": 1.0 + nrm(ks[20], (DEPTH, D_MODEL), 0.02),
        "w_up": nrm(ks[21], (DEPTH, D_MODEL, D_FF), D_MODEL ** -0.5),
        "w_down": nrm(ks[22], (DEPTH, D_FF, D_MODEL), D_FF ** -0.5),
        "normf_g": 1.0 + nrm(ks[23], (D_MODEL,), 0.02),
    }


def reference(x_prompt, x_sample, state_conv_b, state_h_b, norm1_g, w_in, ln_v_g, ln_v_b, w_s, b_s,
              conv_w, conv_b, w_a, b_a, w_x, b_x, lam, gn_a_g, gn_b_g, w_out, norm2_g, w_up, w_down, normf_g):
    hp = x_prompt
    hs = x_sample
    conv_p, hlast_p, conv_s, hlast_s, v_s = [], [], [], [], []
    for l in range(DEPTH):
        p = (norm1_g[l], w_in[l], ln_v_g[l], ln_v_b[l], w_s[l], b_s[l], conv_w[l], conv_b[l],
             w_a[l], b_a[l], w_x[l], b_x[l], lam[l], gn_a_g[l], gn_b_g[l], w_out[l],
             norm2_g[l], w_up[l], w_down[l])
        zb = jnp.zeros((hp.shape[0], CONV_W - 1, W_B), hp.dtype)
        zh = jnp.zeros((hp.shape[0], W_B), hp.dtype)
        hp, cp, lp, _ = _layer(hp, zb, zh, True, *p)
        hs, cs, ls, vs = _layer(hs, state_conv_b[l], state_h_b[l], False, *p)
        conv_p.append(cp); hlast_p.append(lp)
        conv_s.append(cs); hlast_s.append(ls); v_s.append(vs)
    y_prompt = _rmsnorm(hp, normf_g)
    y_sample = _rmsnorm(hs, normf_g)
    return (y_prompt, y_sample, jnp.stack(conv_p), jnp.stack(hlast_p), jnp.stack(conv_s),
            jnp.stack(hlast_s), jnp.stack(v_s))
```

```python
import functools

import jax
import jax.numpy as jnp
from jax import lax
from jax.experimental import pallas as pl
from jax.experimental.pallas import tpu as pltpu

H_A = 8
H_B = 8
GMLP_CHUNK = 128
CONV_W = 4
LRU_C = 8.0
EPS = 1e-6

SUBLANES = 8
MXU_DIM = 256
FF_CHUNK = 1024
VMEM_LIMIT_BYTES = 56 * 1024 * 1024

P_LNV_G, P_LNV_B, P_CONV_W, P_CONV_B, P_BA, P_BX, P_LAM, P_GNA, P_GNB = 0, 1, 2, 6, 7, 8, 9, 10, 11
P_NORM1, P_NORM2, P_NORMF = 0, 1, 2

_f32 = jnp.float32
_bf16 = jnp.bfloat16


def _rms(x, g):
    return x * lax.rsqrt(jnp.mean(x * x, axis=-1, keepdims=True) + EPS) * g


def _layernorm(x, g, b):
    mu = jnp.mean(x, axis=-1, keepdims=True)
    xc = x - mu
    var = jnp.mean(xc * xc, axis=-1, keepdims=True)
    return xc * lax.rsqrt(var + EPS) * g + b


def _layer_body(x_ref, hdr_ref, h0_ref, p5_ref, p10_ref, win_ref, wmix_ref, bmix_ref, wgate_ref,
                wout_ref, wup_ref, wdn_ref, *rest, tm, chunk, seg, final_norm, reset_first, emit_v):
    if emit_v:
        y_ref, st_ref, v_ref = rest[:3]
        scratch = rest[3:]
    else:
        y_ref, st_ref = rest[:2]
        v_ref = None
        scratch = rest[2:]
    z_ref, xpad_ref, a_ref, b_ref, hs_ref, hcar_ref, cat_ref, hn_ref, wmixm_ref = scratch

    nseg = tm // seg
    nchunk = tm // chunk
    ngroup = seg // SUBLANES
    d_model = x_ref.shape[-1]
    w_a = bmix_ref.shape[-1]
    w_b = p5_ref.shape[-1]
    hd_a = w_a // H_A
    d_ff = wup_ref.shape[-1]
    heads_per_tile = MXU_DIM // hd_a
    n_tiles_a = w_a // MXU_DIM
    t = pl.program_id(1)

    @pl.when(t == 0)
    def _init():
        row = lax.broadcasted_iota(jnp.int32, (chunk, heads_per_tile * chunk), 0)
        col = lax.broadcasted_iota(jnp.int32, (chunk, heads_per_tile * chunk), 1) % chunk
        for q in range(n_tiles_a):
            wmixm_ref[q] = jnp.where(col <= row, wmix_ref[q], 0.0).astype(_bf16)
        xpad_ref[:, 0:SUBLANES, :] = hdr_ref[0]
        hcar_ref[...] = h0_ref[0]

    if nseg == 1:
        @pl.when(t > 0)
        def _carry_conv():
            xpad_ref[0, 0:SUBLANES, :] = xpad_ref[0, seg:seg + SUBLANES, :]

    xn = _rms(x_ref[0], p10_ref[P_NORM1:P_NORM1 + 1, :]).astype(_bf16)
    z_ref[...] = jnp.dot(xn, win_ref[...], preferred_element_type=_f32)

    v = _layernorm(jax.nn.gelu(z_ref[:, w_a:2 * w_a]), p5_ref[P_LNV_G:P_LNV_G + 1, :],
                   p5_ref[P_LNV_B:P_LNV_B + 1, :])
    if emit_v:
        v_ref[0] = v
    lane = lax.broadcasted_iota(jnp.int32, (chunk, MXU_DIM), 1)
    mixed_rows = []
    for c in range(nchunk):
        tiles = []
        for q in range(n_tiles_a):
            vq = v[c * chunk:(c + 1) * chunk, q * MXU_DIM:(q + 1) * MXU_DIM]
            rhs = jnp.concatenate(
                [jnp.where((lane >= hd_a * j) & (lane < hd_a * (j + 1)), vq, 0.0)
                 for j in range(heads_per_tile)], axis=0).astype(_bf16)
            tiles.append(jnp.dot(wmixm_ref[q], rhs, preferred_element_type=_f32))
        mixed_rows.append(jnp.concatenate(tiles, axis=1) + bmix_ref[...])
    mixed = jnp.concatenate(mixed_rows, axis=0)
    ya = jax.nn.gelu(z_ref[:, 0:w_a]) * mixed
    cat_ref[:, 0:w_a] = _rms(ya, p5_ref[P_GNA:P_GNA + 1, :]).astype(_bf16)

    for s in range(nseg):
        xpad_ref[s, SUBLANES:SUBLANES + seg, :] = z_ref[s * seg:(s + 1) * seg, 2 * w_a:2 * w_a + w_b]
    xcs = []
    for s in range(nseg):
        acc = p5_ref[P_CONV_B:P_CONV_B + 1, :]
        for k in range(CONV_W):
            off = SUBLANES - (CONV_W - 1) + k
            acc = acc + xpad_ref[s, off:off + seg, :] * p5_ref[P_CONV_W + k:P_CONV_W + k + 1, :]
        xcs.append(acc)
    xc = jnp.concatenate(xcs, axis=0) if nseg > 1 else xcs[0]
    xcb = xc.astype(_bf16)
    half = w_b // 2
    pre = [jnp.dot(xcb[:, j * half:(j + 1) * half], wgate_ref[j], preferred_element_type=_f32)
           for j in range(2)]
    r = jax.nn.sigmoid(jnp.concatenate([pre[0][:, :half], pre[1][:, :half]], axis=1)
                       + p5_ref[P_BA:P_BA + 1, :])
    i = jax.nn.sigmoid(jnp.concatenate([pre[0][:, half:], pre[1][:, half:]], axis=1)
                       + p5_ref[P_BX:P_BX + 1, :])
    log_a = -LRU_C * r * jax.nn.softplus(-p5_ref[P_LAM:P_LAM + 1, :])
    a = jnp.exp(log_a)
    th = jnp.tanh(log_a)
    mult = jnp.sqrt(-2.0 * th / (1.0 - th))
    if reset_first:
        first = (lax.broadcasted_iota(jnp.int32, (tm, w_b), 0) == 0) & (t == 0)
        mult = jnp.where(first, 1.0, mult)
    a_ref[...] = a
    b_ref[...] = mult * (i * xc)

    rowid = lax.broadcasted_iota(jnp.int32, (SUBLANES, w_b), 0)
    for s in range(nseg):
        hprev = hcar_ref[s]

        def group(g, hp, s=s):
            r0 = pl.multiple_of(s * seg + g * SUBLANES, SUBLANES)
            ag = a_ref[pl.ds(r0, SUBLANES), :]
            bg = b_ref[pl.ds(r0, SUBLANES), :]
            for d in (1, 2, 4):
                keep = rowid >= d
                a_sh = jnp.where(keep, pltpu.roll(ag, d, 0), 1.0)
                b_sh = jnp.where(keep, pltpu.roll(bg, d, 0), 0.0)
                bg = bg + ag * b_sh
                ag = ag * a_sh
            hg = bg + ag * hp
            hs_ref[pl.ds(r0, SUBLANES), :] = hg
            return jnp.broadcast_to(hg[SUBLANES - 1:SUBLANES, :], (SUBLANES, w_b))

        hlast = lax.fori_loop(0, ngroup, group, hprev, unroll=True)
        hcar_ref[s] = hlast
        st_ref[0, s, 0:SUBLANES, :] = xpad_ref[s, seg:seg + SUBLANES, :]
        st_ref[0, s, SUBLANES:2 * SUBLANES, :] = hlast

    yb = hs_ref[...] * jax.nn.gelu(z_ref[:, 2 * w_a + w_b:2 * w_a + 2 * w_b])
    cat_ref[:, w_a:w_a + w_b] = _rms(yb, p5_ref[P_GNB:P_GNB + 1, :]).astype(_bf16)

    y_ref[0] = x_ref[0] + jnp.dot(cat_ref[...], wout_ref[...], preferred_element_type=_f32)
    hn_ref[...] = _rms(y_ref[0], p10_ref[P_NORM2:P_NORM2 + 1, :]).astype(_bf16)
    for c in range(d_ff // FF_CHUNK):
        up = jnp.dot(hn_ref[...], wup_ref[:, c * FF_CHUNK:(c + 1) * FF_CHUNK], preferred_element_type=_f32)
        hid = jnp.square(jnp.maximum(up, 0.0)).astype(_bf16)
        y_ref[0] += jnp.dot(hid, wdn_ref[c * FF_CHUNK:(c + 1) * FF_CHUNK, :], preferred_element_type=_f32)
    if final_norm:
        y_ref[0] = _rms(y_ref[0], p10_ref[P_NORMF:P_NORMF + 1, :])


def _resident(shape):
    nd = len(shape)
    return pl.BlockSpec(shape, lambda b, t: (0,) * nd, pipeline_mode=pl.Buffered(1))


def _run_layer(x, hdr, h0, weights, *, tm, chunk, seg, final_norm, reset_first, emit_v):
    p5, p10, win, wmix, bmix, wgate, wout, wup, wdn = weights
    nb, T, d_model = x.shape
    w_b = p5.shape[-1]
    w_a = bmix.shape[-1]
    nseg = tm // seg
    grid = (nb, T // tm)
    body = functools.partial(_layer_body, tm=tm, chunk=chunk, seg=seg, final_norm=final_norm,
                             reset_first=reset_first, emit_v=emit_v)
    in_specs = [
        pl.BlockSpec((1, tm, d_model), lambda b, t: (b, t, 0)),
        pl.BlockSpec((1, nseg, SUBLANES, w_b), lambda b, t: (b, 0, 0, 0)),
        pl.BlockSpec((1, nseg, SUBLANES, w_b), lambda b, t: (b, 0, 0, 0)),
    ] + [_resident(w.shape) for w in weights]
    out_shape = [jax.ShapeDtypeStruct((nb, T, d_model), _f32),
                 jax.ShapeDtypeStruct((nb, nseg, 2 * SUBLANES, w_b), _f32)]
    out_specs = [pl.BlockSpec((1, tm, d_model), lambda b, t: (b, t, 0)),
                 pl.BlockSpec((1, nseg, 2 * SUBLANES, w_b), lambda b, t: (b, 0, 0, 0))]
    if emit_v:
        out_shape.append(jax.ShapeDtypeStruct((nb, T, w_a), _f32))
        out_specs.append(pl.BlockSpec((1, tm, w_a), lambda b, t: (b, t, 0)))
    scratch = [
        pltpu.VMEM((tm, 2 * w_a + 2 * w_b), _f32),
        pltpu.VMEM((nseg, seg + SUBLANES, w_b), _f32),
        pltpu.VMEM((tm, w_b), _f32),
        pltpu.VMEM((tm, w_b), _f32),
        pltpu.VMEM((tm, w_b), _f32),
        pltpu.VMEM((nseg, SUBLANES, w_b), _f32),
        pltpu.VMEM((tm, w_a + w_b), _bf16),
        pltpu.VMEM((tm, d_model), _bf16),
        pltpu.VMEM(wmix.shape, _bf16),
    ]
    return pl.pallas_call(
        body,
        grid=grid,
        in_specs=in_specs,
        out_specs=out_specs,
        out_shape=out_shape,
        scratch_shapes=scratch,
        compiler_params=pltpu.CompilerParams(
            dimension_semantics=("arbitrary", "arbitrary"),
            vmem_limit_bytes=VMEM_LIMIT_BYTES),
    )(x, hdr, h0, *weights)


def _block_diag(w):
    n, k, _ = w.shape
    eye = jnp.eye(n, dtype=w.dtype)
    return (eye[:, None, :, None] * w[:, :, None, :]).reshape(n * k, n * k)


def _prep_weights(l, chunk, norm1_g, w_in, ln_v_g, ln_v_b, w_s, b_s, conv_w, conv_b, w_a, b_a, w_x, b_x,
                  lam, gn_a_g, gn_b_g, w_out, norm2_g, w_up, w_down, normf_g):
    w_b = conv_b.shape[-1]
    d_model = norm1_g.shape[-1]
    hd_a = gn_a_g.shape[-1] // H_A
    heads_per_tile = MXU_DIM // hd_a
    rows5 = [ln_v_g[l], ln_v_b[l], conv_w[l, 0], conv_w[l, 1], conv_w[l, 2], conv_w[l, 3], conv_b[l],
             b_a[l], b_x[l], lam[l], gn_a_g[l], gn_b_g[l]]
    p5 = jnp.stack(rows5 + [jnp.zeros((w_b,), _f32)] * (16 - len(rows5)))
    p10 = jnp.stack([norm1_g[l], norm2_g[l], normf_g] + [jnp.zeros((d_model,), _f32)] * 5)
    ws = w_s[l][:, :chunk, :chunk]
    wmix = ws.reshape(H_A // heads_per_tile, heads_per_tile, chunk, chunk).transpose(0, 2, 1, 3)
    wmix = wmix.reshape(H_A // heads_per_tile, chunk, heads_per_tile * chunk)
    bmix = jnp.repeat(b_s[l][:, :chunk].T, hd_a, axis=1)
    hh = H_B // 2
    wgate = jnp.stack([jnp.concatenate([_block_diag(w_a[l, j * hh:(j + 1) * hh]),
                                        _block_diag(w_x[l, j * hh:(j + 1) * hh])], axis=1)
                       for j in range(2)]).astype(_bf16)
    return (p5, p10, w_in[l].astype(_bf16), wmix, bmix, wgate, w_out[l].astype(_bf16),
            w_up[l].astype(_bf16), w_down[l].astype(_bf16))


def _state_rows(conv_state, h_state):
    n, k, w = conv_state.shape
    hdr = jnp.concatenate([jnp.zeros((n, SUBLANES - k, w), conv_state.dtype), conv_state], axis=1)
    h0 = jnp.broadcast_to(h_state[:, None, :], (n, SUBLANES, w))
    return hdr[:, None], h0[:, None]


def kernel(x_prompt, x_sample, state_conv_b, state_h_b, norm1_g, w_in, ln_v_g, ln_v_b, w_s, b_s, conv_w, conv_b,
           w_a, b_a, w_x, b_x, lam, gn_a_g, gn_b_g, w_out, norm2_g, w_up, w_down, normf_g):
    depth = w_in.shape[0]
    nb, T, d_model = x_prompt.shape
    ns, ts, _ = x_sample.shape
    w_b = conv_b.shape[-1]
    params = (norm1_g, w_in, ln_v_g, ln_v_b, w_s, b_s, conv_w, conv_b, w_a, b_a, w_x, b_x, lam, gn_a_g, gn_b_g,
              w_out, norm2_g, w_up, w_down, normf_g)
    tm_p = 256
    hp = x_prompt
    hs = x_sample.reshape(1, ns * ts, d_model)
    conv_p, hlast_p, conv_s, hlast_s, v_s = [], [], [], [], []
    for l in range(depth):
        last = l == depth - 1
        wp = _prep_weights(l, GMLP_CHUNK, *params)
        hdr, h0 = _state_rows(jnp.zeros((nb, CONV_W - 1, w_b), _f32), jnp.zeros((nb, w_b), _f32))
        hp, st = _run_layer(hp, hdr, h0, wp, tm=tm_p, chunk=GMLP_CHUNK, seg=tm_p, final_norm=last,
                            reset_first=True, emit_v=False)
        conv_p.append(st[:, 0, SUBLANES - (CONV_W - 1):SUBLANES])
        hlast_p.append(st[:, 0, 2 * SUBLANES - 1])
        wsm = _prep_weights(l, ts, *params)
        hdr, h0 = _state_rows(state_conv_b[l], state_h_b[l])
        hs, st, vv = _run_layer(hs, hdr.reshape(1, ns, SUBLANES, w_b), h0.reshape(1, ns, SUBLANES, w_b), wsm,
                                tm=ns * ts, chunk=ts, seg=ts, final_norm=last, reset_first=False, emit_v=True)
        conv_s.append(st[0, :, SUBLANES - (CONV_W - 1):SUBLANES])
        hlast_s.append(st[0, :, 2 * SUBLANES - 1])
        v_s.append(vv.reshape(ns, ts, -1))
    return (hp, hs.reshape(ns, ts, d_model), jnp.stack(conv_p), jnp.stack(hlast_p), jnp.stack(conv_s),
            jnp.stack(hlast_s), jnp.stack(v_s))
```

```python
import functools

import jax
import jax.numpy as jnp
from jax import lax
from jax.experimental import pallas as pl
from jax.experimental.pallas import tpu as pltpu

H_A = 8
H_B = 8
GMLP_CHUNK = 128
CONV_W = 4
LRU_C = 8.0
EPS = 1e-6

SUBLANES = 8
MXU_DIM = 256
FF_CHUNK = 1024
PROMPT_TM = 256
VMEM_LIMIT_BYTES = 56 * 1024 * 1024

P_LNV_G, P_LNV_B, P_CONV_W, P_CONV_B, P_BA, P_BX, P_LAM, P_GNA, P_GNB = 0, 1, 2, 6, 7, 8, 9, 10, 11
P_NORM1, P_NORM2, P_NORMF = 0, 1, 2

_f32 = jnp.float32
_bf16 = jnp.bfloat16


def _rms(x, g):
    return x * lax.rsqrt(jnp.mean(x * x, axis=-1, keepdims=True) + EPS) * g


def _layernorm(x, g, b):
    mu = jnp.mean(x, axis=-1, keepdims=True)
    xc = x - mu
    var = jnp.mean(xc * xc, axis=-1, keepdims=True)
    return xc * lax.rsqrt(var + EPS) * g + b


def _row(ref, r):
    return ref[r:r + 1, :]


def _init_sequence(hdr_ref, h0_ref, wmix_ref, xpad_ref, hcar_ref, wmixm_ref, *, chunk):
    n_tiles_a, _, kcat = wmix_ref.shape
    row = lax.broadcasted_iota(jnp.int32, (chunk, kcat), 0)
    col = lax.broadcasted_iota(jnp.int32, (chunk, kcat), 1) % chunk
    for q in range(n_tiles_a):
        wmixm_ref[q] = jnp.where(col <= row, wmix_ref[q], 0.0).astype(_bf16)
    xpad_ref[:, 0:SUBLANES, :] = hdr_ref[0]
    hcar_ref[...] = h0_ref[0]


def _mix_phases(x_ref, p5_ref, p10_ref, win_ref, bmix_ref, wgate_ref, v_ref,
                z_ref, mixed_ref, pre_ref, xpad_ref, a_ref, b_ref, hs_ref, hcar_ref, cat_ref, wmixm_ref,
                *, tm, chunk, seg, first_tile):
    nseg = tm // seg
    nchunk = tm // chunk
    ngroup = seg // SUBLANES
    w_a = bmix_ref.shape[-1]
    w_b = p5_ref.shape[-1]
    hd_a = w_a // H_A
    heads_per_tile = MXU_DIM // hd_a
    n_tiles_a = w_a // MXU_DIM

    xn = _rms(x_ref[...], _row(p10_ref, P_NORM1)).astype(_bf16)
    z_ref[...] = jnp.dot(xn, win_ref[...], preferred_element_type=_f32)
    yield

    v = _layernorm(jax.nn.gelu(z_ref[:, w_a:2 * w_a]), _row(p5_ref, P_LNV_G), _row(p5_ref, P_LNV_B))
    if v_ref is not None:
        v_ref[...] = v
    lane = lax.broadcasted_iota(jnp.int32, (chunk, MXU_DIM), 1)
    for c in range(nchunk):
        rows = slice(c * chunk, (c + 1) * chunk)
        for q in range(n_tiles_a):
            cols = slice(q * MXU_DIM, (q + 1) * MXU_DIM)
            vq = v[rows, cols]
            rhs = jnp.concatenate(
                [jnp.where((lane >= hd_a * j) & (lane < hd_a * (j + 1)), vq, 0.0)
                 for j in range(heads_per_tile)], axis=0).astype(_bf16)
            mixed_ref[rows, cols] = jnp.dot(wmixm_ref[q], rhs, preferred_element_type=_f32)

    for s in range(nseg):
        xpad_ref[s, SUBLANES:SUBLANES + seg, :] = z_ref[s * seg:(s + 1) * seg, 2 * w_a:2 * w_a + w_b]
    xcs = []
    for s in range(nseg):
        acc = _row(p5_ref, P_CONV_B)
        for k in range(CONV_W):
            off = SUBLANES - (CONV_W - 1) + k
            acc = acc + xpad_ref[s, off:off + seg, :] * _row(p5_ref, P_CONV_W + k)
        xcs.append(acc)
        xpad_ref[s, 0:SUBLANES, :] = xpad_ref[s, seg:seg + SUBLANES, :]
    xc = jnp.concatenate(xcs, axis=0) if nseg > 1 else xcs[0]
    b_ref[...] = xc
    xcb = xc.astype(_bf16)
    half = w_b // 2
    for j in range(2):
        res = jnp.dot(xcb[:, j * half:(j + 1) * half], wgate_ref[j], preferred_element_type=_f32)
        pre_ref[:, j * half:(j + 1) * half] = res[:, :half]
        pre_ref[:, w_b + j * half:w_b + (j + 1) * half] = res[:, half:]
    yield

    ya = jnp.concatenate([jax.nn.gelu(z_ref[c * chunk:(c + 1) * chunk, 0:w_a])
                          * (mixed_ref[c * chunk:(c + 1) * chunk, :] + bmix_ref[...])
                          for c in range(nchunk)], axis=0)
    cat_ref[:, 0:w_a] = _rms(ya, _row(p5_ref, P_GNA)).astype(_bf16)

    r = jax.nn.sigmoid(pre_ref[:, 0:w_b] + _row(p5_ref, P_BA))
    i = jax.nn.sigmoid(pre_ref[:, w_b:2 * w_b] + _row(p5_ref, P_BX))
    log_a = -LRU_C * r * jax.nn.softplus(-_row(p5_ref, P_LAM))
    a = jnp.exp(log_a)
    th = jnp.tanh(log_a)
    mult = jnp.sqrt(-2.0 * th / (1.0 - th))
    if first_tile is not None:
        first = (lax.broadcasted_iota(jnp.int32, (tm, w_b), 0) == 0) & first_tile
        mult = jnp.where(first, 1.0, mult)
    a_ref[...] = a
    b_ref[...] = mult * (i * b_ref[...])

    rowid = lax.broadcasted_iota(jnp.int32, (SUBLANES, w_b), 0)
    for s in range(nseg):
        def group(g, hp, s=s):
            r0 = pl.multiple_of(s * seg + g * SUBLANES, SUBLANES)
            ag = a_ref[pl.ds(r0, SUBLANES), :]
            bg = b_ref[pl.ds(r0, SUBLANES), :]
            for d in (1, 2, 4):
                keep = rowid >= d
                a_sh = jnp.where(keep, pltpu.roll(ag, d, 0), 1.0)
                b_sh = jnp.where(keep, pltpu.roll(bg, d, 0), 0.0)
                bg = bg + ag * b_sh
                ag = ag * a_sh
            hg = bg + ag * hp
            hs_ref[pl.ds(r0, SUBLANES), :] = hg
            return jnp.broadcast_to(hg[SUBLANES - 1:SUBLANES, :], (SUBLANES, w_b))

        hcar_ref[s] = lax.fori_loop(0, ngroup, group, hcar_ref[s], unroll=True)

    yb = hs_ref[...] * jax.nn.gelu(z_ref[:, 2 * w_a + w_b:2 * w_a + 2 * w_b])
    cat_ref[:, w_a:w_a + w_b] = _rms(yb, _row(p5_ref, P_GNB)).astype(_bf16)


def _mlp_phases(x_ref, cat_ref, p10_ref, wout_ref, wup_ref, wdn_ref, y_ref, acc_ref, h1_ref, hn_ref,
                *, final_norm, deferred_store):
    d_ff = wup_ref.shape[-1]

    def store_y():
        y_ref[...] = _rms(acc_ref[...], _row(p10_ref, P_NORMF)) if final_norm else acc_ref[...]

    if deferred_store:
        store_y()
    h1_ref[...] = x_ref[...] + jnp.dot(cat_ref[...], wout_ref[...], preferred_element_type=_f32)
    yield
    hn_ref[...] = _rms(h1_ref[...], _row(p10_ref, P_NORM2)).astype(_bf16)
    for c in range(d_ff // FF_CHUNK):
        up = jnp.dot(hn_ref[...], wup_ref[:, c * FF_CHUNK:(c + 1) * FF_CHUNK], preferred_element_type=_f32)
        yield
        hid = jnp.square(jnp.maximum(up, 0.0)).astype(_bf16)
        base = h1_ref[...] if c == 0 else acc_ref[...]
        acc_ref[...] = base + jnp.dot(hid, wdn_ref[c * FF_CHUNK:(c + 1) * FF_CHUNK, :],
                                      preferred_element_type=_f32)
        if c + 1 < d_ff // FF_CHUNK:
            yield
    if not deferred_store:
        store_y()


def _write_state(st_ref, xpad_ref, hcar_ref):
    st_ref[0, :, 0:SUBLANES, :] = xpad_ref[:, 0:SUBLANES, :]
    st_ref[0, :, SUBLANES:2 * SUBLANES, :] = hcar_ref[...]


_PIPELINE_ORDER = "FMFFMFFFMFFF"
_PIPELINE_DEPTH = 2


def _pipelined_body(x_ref, xprev_ref, hdr_ref, h0_ref, p5_ref, p10_ref, win_ref, wmix_ref, bmix_ref, wgate_ref,
                    wout_ref, wup_ref, wdn_ref, y_ref, st_ref,
                    z_ref, mixed_ref, pre_ref, xpad_ref, a_ref, b_ref, hs_ref, hcar_ref, cat_ref, acc_ref, h1_ref, hn_ref, wmixm_ref,
                    *, tm, chunk, n_tiles, tiles_per_seq, final_norm):
    i = pl.program_id(0)
    t_seq = jnp.minimum(i, n_tiles - 1) % tiles_per_seq

    @pl.when(i == 0)
    def _():
        cat_ref[...] = jnp.zeros_like(cat_ref)
        acc_ref[...] = jnp.zeros_like(acc_ref)

    @pl.when(t_seq == 0)
    def _():
        _init_sequence(hdr_ref, h0_ref, wmix_ref, xpad_ref, hcar_ref, wmixm_ref, chunk=chunk)

    mix = _mix_phases(x_ref, p5_ref, p10_ref, win_ref, bmix_ref, wgate_ref, None,
                      z_ref, mixed_ref, pre_ref, xpad_ref, a_ref, b_ref, hs_ref, hcar_ref, cat_ref, wmixm_ref,
                      tm=tm, chunk=chunk, seg=tm, first_tile=(t_seq == 0))
    mlp = _mlp_phases(xprev_ref, cat_ref, p10_ref, wout_ref, wup_ref, wdn_ref, y_ref, acc_ref, h1_ref, hn_ref,
                      final_norm=final_norm, deferred_store=True)
    for who in _PIPELINE_ORDER:
        next(mix if who == "M" else mlp, None)
    assert next(mix, "done") == "done" and next(mlp, "done") == "done"

    @pl.when((t_seq == tiles_per_seq - 1) & (i < n_tiles))
    def _():
        _write_state(st_ref, xpad_ref, hcar_ref)


def _single_body(x_ref, hdr_ref, h0_ref, p5_ref, p10_ref, win_ref, wmix_ref, bmix_ref, wgate_ref,
                 wout_ref, wup_ref, wdn_ref, y_ref, st_ref, v_ref,
                 z_ref, mixed_ref, pre_ref, xpad_ref, a_ref, b_ref, hs_ref, hcar_ref, cat_ref, acc_ref, h1_ref, hn_ref, wmixm_ref,
                 *, tm, chunk, seg, final_norm):
    _init_sequence(hdr_ref, h0_ref, wmix_ref, xpad_ref, hcar_ref, wmixm_ref, chunk=chunk)
    for _ in _mix_phases(x_ref, p5_ref, p10_ref, win_ref, bmix_ref, wgate_ref, v_ref,
                         z_ref, mixed_ref, pre_ref, xpad_ref, a_ref, b_ref, hs_ref, hcar_ref, cat_ref, wmixm_ref,
                         tm=tm, chunk=chunk, seg=seg, first_tile=None):
        pass
    for _ in _mlp_phases(x_ref, cat_ref, p10_ref, wout_ref, wup_ref, wdn_ref, y_ref, acc_ref, h1_ref, hn_ref,
                         final_norm=final_norm, deferred_store=False):
        pass
    _write_state(st_ref, xpad_ref, hcar_ref)


def _resident(shape):
    nd = len(shape)
    return pl.BlockSpec(shape, lambda i: (0,) * nd, pipeline_mode=pl.Buffered(1))


def _scratch(tm, nseg, seg, d_model, w_a, w_b, wmix_shape):
    return [
        pltpu.VMEM((tm, 2 * w_a + 2 * w_b), _f32),
        pltpu.VMEM((tm, w_a), _f32),
        pltpu.VMEM((tm, 2 * w_b), _f32),
        pltpu.VMEM((nseg, seg + SUBLANES, w_b), _f32),
        pltpu.VMEM((tm, w_b), _f32),
        pltpu.VMEM((tm, w_b), _f32),
        pltpu.VMEM((tm, w_b), _f32),
        pltpu.VMEM((nseg, SUBLANES, w_b), _f32),
        pltpu.VMEM((tm, w_a + w_b), _bf16),
        pltpu.VMEM((tm, d_model), _f32),
        pltpu.VMEM((tm, d_model), _f32),
        pltpu.VMEM((tm, d_model), _bf16),
        pltpu.VMEM(wmix_shape, _bf16),
    ]


def _run_prompt_layer(x, hdr, h0, weights, *, tm, final_norm):
    p5, p10, win, wmix, bmix, wgate, wout, wup, wdn = weights
    nb, T, d_model = x.shape
    w_b = p5.shape[-1]
    w_a = bmix.shape[-1]
    tiles_per_seq = T // tm
    n_tiles = nb * tiles_per_seq
    x2 = x.reshape(nb * T, d_model)
    body = functools.partial(_pipelined_body, tm=tm, chunk=GMLP_CHUNK, n_tiles=n_tiles,
                             tiles_per_seq=tiles_per_seq, final_norm=final_norm)
    tile = lambda lag: (lambda i: (jnp.clip(i - lag, 0, n_tiles - 1), 0))
    seq = lambda i: (jnp.minimum(i, n_tiles - 1) // tiles_per_seq, 0, 0, 0)
    in_specs = [
        pl.BlockSpec((tm, d_model), tile(0)),
        pl.BlockSpec((tm, d_model), tile(1)),
        pl.BlockSpec((1, 1, SUBLANES, w_b), seq),
        pl.BlockSpec((1, 1, SUBLANES, w_b), seq),
    ] + [_resident(w.shape) for w in weights]
    y, st = pl.pallas_call(
        body,
        grid=(n_tiles + _PIPELINE_DEPTH,),
        in_specs=in_specs,
        out_specs=[pl.BlockSpec((tm, d_model), tile(_PIPELINE_DEPTH)),
                   pl.BlockSpec((1, 1, 2 * SUBLANES, w_b), seq)],
        out_shape=[jax.ShapeDtypeStruct((nb * T, d_model), _f32),
                   jax.ShapeDtypeStruct((nb, 1, 2 * SUBLANES, w_b), _f32)],
        scratch_shapes=_scratch(tm, 1, tm, d_model, w_a, w_b, wmix.shape),
        compiler_params=pltpu.CompilerParams(dimension_semantics=("arbitrary",),
                                             vmem_limit_bytes=VMEM_LIMIT_BYTES),
        name="prompt_layer",
    )(x2, x2, hdr, h0, *weights)
    return y.reshape(nb, T, d_model), st


def _run_sample_layer(x, hdr, h0, weights, *, chunk, final_norm):
    p5, p10, win, wmix, bmix, wgate, wout, wup, wdn = weights
    ns, ts, d_model = x.shape
    w_b = p5.shape[-1]
    w_a = bmix.shape[-1]
    tm = ns * ts
    body = functools.partial(_single_body, tm=tm, chunk=chunk, seg=ts, final_norm=final_norm)
    whole = lambda shape: pl.BlockSpec(shape, lambda i: (0,) * len(shape))
    y, st, v = pl.pallas_call(
        body,
        grid=(1,),
        in_specs=[whole((tm, d_model)), whole(hdr.shape), whole(h0.shape)] + [_resident(w.shape) for w in weights],
        out_specs=[whole((tm, d_model)), whole((1, ns, 2 * SUBLANES, w_b)), whole((tm, w_a))],
        out_shape=[jax.ShapeDtypeStruct((tm, d_model), _f32),
                   jax.ShapeDtypeStruct((1, ns, 2 * SUBLANES, w_b), _f32),
                   jax.ShapeDtypeStruct((tm, w_a), _f32)],
        scratch_shapes=_scratch(tm, ns, ts, d_model, w_a, w_b, wmix.shape),
        compiler_params=pltpu.CompilerParams(dimension_semantics=("arbitrary",),
                                             vmem_limit_bytes=VMEM_LIMIT_BYTES),
        name="sample_layer",
    )(x.reshape(tm, d_model), hdr, h0, *weights)
    return y.reshape(ns, ts, d_model), st, v.reshape(ns, ts, w_a)


def _block_diag(w):
    n, k, _ = w.shape
    eye = jnp.eye(n, dtype=w.dtype)
    return (eye[:, None, :, None] * w[:, :, None, :]).reshape(n * k, n * k)


def _prep_weights(l, chunk, norm1_g, w_in, ln_v_g, ln_v_b, w_s, b_s, conv_w, conv_b, w_a, b_a, w_x, b_x,
                  lam, gn_a_g, gn_b_g, w_out, norm2_g, w_up, w_down, normf_g):
    w_b = conv_b.shape[-1]
    d_model = norm1_g.shape[-1]
    hd_a = gn_a_g.shape[-1] // H_A
    heads_per_tile = MXU_DIM // hd_a
    rows5 = [ln_v_g[l], ln_v_b[l], conv_w[l, 0], conv_w[l, 1], conv_w[l, 2], conv_w[l, 3], conv_b[l],
             b_a[l], b_x[l], lam[l], gn_a_g[l], gn_b_g[l]]
    p5 = jnp.stack(rows5 + [jnp.zeros((w_b,), _f32)] * (16 - len(rows5)))
    p10 = jnp.stack([norm1_g[l], norm2_g[l], normf_g] + [jnp.zeros((d_model,), _f32)] * 5)
    ws = w_s[l][:, :chunk, :chunk]
    wmix = ws.reshape(H_A // heads_per_tile, heads_per_tile, chunk, chunk).transpose(0, 2, 1, 3)
    wmix = wmix.reshape(H_A // heads_per_tile, chunk, heads_per_tile * chunk)
    bmix = jnp.repeat(b_s[l][:, :chunk].T, hd_a, axis=1)
    hh = H_B // 2
    wgate = jnp.stack([jnp.concatenate([_block_diag(w_a[l, j * hh:(j + 1) * hh]),
                                        _block_diag(w_x[l, j * hh:(j + 1) * hh])], axis=1)
                       for j in range(2)]).astype(_bf16)
    return (p5, p10, w_in[l].astype(_bf16), wmix, bmix, wgate, w_out[l].astype(_bf16),
            w_up[l].astype(_bf16), w_down[l].astype(_bf16))


def _state_rows(conv_state, h_state):
    n, k, w = conv_state.shape
    hdr = jnp.concatenate([jnp.zeros((n, SUBLANES - k, w), conv_state.dtype), conv_state], axis=1)
    h0 = jnp.broadcast_to(h_state[:, None, :], (n, SUBLANES, w))
    return hdr, h0


def kernel(x_prompt, x_sample, state_conv_b, state_h_b, norm1_g, w_in, ln_v_g, ln_v_b, w_s, b_s, conv_w, conv_b,
           w_a, b_a, w_x, b_x, lam, gn_a_g, gn_b_g, w_out, norm2_g, w_up, w_down, normf_g):
    depth = w_in.shape[0]
    nb = x_prompt.shape[0]
    ns, ts, _ = x_sample.shape
    w_b = conv_b.shape[-1]
    params = (norm1_g, w_in, ln_v_g, ln_v_b, w_s, b_s, conv_w, conv_b, w_a, b_a, w_x, b_x, lam, gn_a_g, gn_b_g,
              w_out, norm2_g, w_up, w_down, normf_g)
    chunk_s = GMLP_CHUNK if ts % GMLP_CHUNK == 0 else ts
    tail = slice(SUBLANES - (CONV_W - 1), SUBLANES)
    hp, hs = x_prompt, x_sample
    conv_p, hlast_p, conv_s, hlast_s, v_s = [], [], [], [], []
    for l in range(depth):
        last = l == depth - 1
        hdr, h0 = _state_rows(jnp.zeros((nb, CONV_W - 1, w_b), _f32), jnp.zeros((nb, w_b), _f32))
        hp, st = _run_prompt_layer(hp, hdr[:, None], h0[:, None], _prep_weights(l, GMLP_CHUNK, *params),
                                   tm=PROMPT_TM, final_norm=last)
        conv_p.append(st[:, 0, tail])
        hlast_p.append(st[:, 0, 2 * SUBLANES - 1])
        hdr, h0 = _state_rows(state_conv_b[l], state_h_b[l])
        hs, st, vv = _run_sample_layer(hs, hdr[None], h0[None], _prep_weights(l, chunk_s, *params),
                                       chunk=chunk_s, final_norm=last)
        conv_s.append(st[0, :, tail])
        hlast_s.append(st[0, :, 2 * SUBLANES - 1])
        v_s.append(vv)
    return (hp, hs, jnp.stack(conv_p), jnp.stack(hlast_p), jnp.stack(conv_s), jnp.stack(hlast_s), jnp.stack(v_s))
```

```python
import functools

import jax
import jax.numpy as jnp
from jax import lax
from jax.experimental import pallas as pl
from jax.experimental.pallas import tpu as pltpu

H_A = 8
H_B = 8
GMLP_CHUNK = 128
CONV_W = 4
LRU_C = 8.0
EPS = 1e-6

SUBLANES = 8
PACK_ROWS = 16
MXU_DIM = 256
FF_CHUNK = 1024
PROMPT_TM = 256
STAGE_ROWS, STAGE_COLS = 256, 1024
VMEM_LIMIT_BYTES = 56 * 1024 * 1024

P_LNV_G, P_LNV_B, P_CONV_W, P_CONV_B, P_BA, P_BX, P_LAM, P_GNA, P_GNB = 0, 1, 2, 6, 7, 8, 9, 10, 11
P_NORM1, P_NORM2, P_NORMF = 0, 1, 2

_f32 = jnp.float32
_bf16 = jnp.bfloat16


def _rms(x, g):
    return x * lax.rsqrt(jnp.mean(x * x, axis=-1, keepdims=True) + EPS) * g


def _layernorm(x, g, b):
    mu = jnp.mean(x, axis=-1, keepdims=True)
    xc = x - mu
    var = jnp.mean(xc * xc, axis=-1, keepdims=True)
    return xc * lax.rsqrt(var + EPS) * g + b


def _sigmoid(x):
    return 0.5 * (1.0 + jnp.tanh(0.5 * x))


def _row(ref, r):
    return ref[r:r + 1, :]


def _load_weight_bf16(src_hbm, dst_ref, stage_ref, sem):
    n_rows, n_cols = src_hbm.shape
    _, stage_rows, stage_cols = stage_ref.shape
    n_chunks = n_rows // stage_rows

    for c0 in range(0, n_cols, stage_cols):
        def chunk_copy(k, slot, c0=c0):
            r0 = pl.multiple_of(k * stage_rows, stage_rows)
            return pltpu.make_async_copy(src_hbm.at[pl.ds(r0, stage_rows), pl.ds(c0, stage_cols)],
                                         stage_ref.at[slot], sem.at[slot])

        chunk_copy(0, 0).start()

        def body(k, carry, c0=c0, chunk_copy=chunk_copy):
            slot = k % 2

            @pl.when(k + 1 < n_chunks)
            def _():
                chunk_copy(k + 1, 1 - slot).start()

            chunk_copy(k, slot).wait()
            r0 = pl.multiple_of(k * stage_rows, stage_rows)
            dst_ref[pl.ds(r0, stage_rows), c0:c0 + stage_cols] = stage_ref[slot].astype(_bf16)
            return carry

        lax.fori_loop(0, n_chunks, body, 0)


def _init_proj(hdr_ref, wmix_ref, xpad_ref, wmixm_ref, *, chunk):
    n_tiles_a, _, kcat = wmix_ref.shape
    row = lax.broadcasted_iota(jnp.int32, (chunk, kcat), 0)
    col = lax.broadcasted_iota(jnp.int32, (chunk, kcat), 1) % chunk
    for q in range(n_tiles_a):
        wmixm_ref[q] = jnp.where(col <= row, wmix_ref[q], 0.0).astype(_bf16)
    xpad_ref[:, 0:SUBLANES, :] = hdr_ref[0]


def _proj_phases(x_ref, p5_ref, p10_ref, win_ref, wgate_ref, v_ref, wmixm_ref,
                 xn_ref, z_ref, xpad_ref, gu_ref, gg_ref, mixed_ref, xc_ref, pre_ref, *, tm, chunk, seg):
    nseg = tm // seg
    nchunk = tm // chunk
    w_a = gu_ref.shape[-1]
    w_b = p5_ref.shape[-1]
    hd_a = w_a // H_A
    heads_per_tile = MXU_DIM // hd_a
    n_tiles_a = w_a // MXU_DIM

    xn_ref[...] = _rms(x_ref[...], _row(p10_ref, P_NORM1)).astype(_bf16)
    z_ref[...] = jnp.dot(xn_ref[...], win_ref[...], preferred_element_type=_f32)
    yield

    gu_ref[...] = z_ref[:, 0:w_a]
    gg_ref[...] = z_ref[:, 2 * w_a + w_b:2 * w_a + 2 * w_b]

    v = _layernorm(jax.nn.gelu(z_ref[:, w_a:2 * w_a]), _row(p5_ref, P_LNV_G), _row(p5_ref, P_LNV_B))
    if v_ref is not None:
        v_ref[...] = v
    lane = lax.broadcasted_iota(jnp.int32, (chunk, MXU_DIM), 1)
    for c in range(nchunk):
        rows = slice(c * chunk, (c + 1) * chunk)
        for q in range(n_tiles_a):
            cols = slice(q * MXU_DIM, (q + 1) * MXU_DIM)
            vq = v[rows, cols]
            rhs = jnp.concatenate(
                [jnp.where((lane >= hd_a * j) & (lane < hd_a * (j + 1)), vq, 0.0)
                 for j in range(heads_per_tile)], axis=0).astype(_bf16)
            mixed_ref[rows, cols] = jnp.dot(wmixm_ref[q], rhs, preferred_element_type=_f32)

    for s in range(nseg):
        xpad_ref[s, SUBLANES:SUBLANES + seg, :] = z_ref[s * seg:(s + 1) * seg, 2 * w_a:2 * w_a + w_b]
    xcs = []
    for s in range(nseg):
        acc = _row(p5_ref, P_CONV_B)
        for k in range(CONV_W):
            off = SUBLANES - (CONV_W - 1) + k
            acc = acc + xpad_ref[s, off:off + seg, :] * _row(p5_ref, P_CONV_W + k)
        xcs.append(acc)
        xpad_ref[s, 0:SUBLANES, :] = xpad_ref[s, seg:seg + SUBLANES, :]
    xc = jnp.concatenate(xcs, axis=0) if nseg > 1 else xcs[0]
    xc_ref[...] = xc
    xcb = xc.astype(_bf16)
    half = w_b // 2
    for j in range(2):
        res = jnp.dot(xcb[:, j * half:(j + 1) * half], wgate_ref[j], preferred_element_type=_f32)
        pre_ref[:, j * half:(j + 1) * half] = res[:, :half]
        pre_ref[:, w_b + j * half:w_b + (j + 1) * half] = res[:, half:]


def _gate_stage(p5_ref, bmix_ref, gu_ref, gg_ref, mixed_ref, xc_ref, pre_ref, hcar_ref, cat_ref,
                *, tm, chunk, seg, first_tile, never=None):
    nseg = tm // seg
    w_a = gu_ref.shape[-1]
    w_b = p5_ref.shape[-1]

    rowid = lax.broadcasted_iota(jnp.int32, (SUBLANES, w_b), 0)
    for s in range(nseg):
        hp = hcar_ref[s]
        for r0 in range(s * seg, (s + 1) * seg, PACK_ROWS):
            ybs = []
            for g0 in range(r0, r0 + PACK_ROWS, SUBLANES):
                rows = slice(g0, g0 + SUBLANES)
                pre_r = pre_ref[rows, 0:w_b]
                pre_i = pre_ref[rows, w_b:2 * w_b]
                if never is not None and g0 > s * seg:
                    pre_r = jnp.where(never, hp, pre_r)
                    pre_i = jnp.where(never, hp, pre_i)
                r = _sigmoid(pre_r + _row(p5_ref, P_BA))
                i = _sigmoid(pre_i + _row(p5_ref, P_BX))
                log_a = -LRU_C * r * jax.nn.softplus(-_row(p5_ref, P_LAM))
                ag = jnp.exp(log_a)
                th = jnp.tanh(log_a)
                n = -2.0 * th
                mult = jnp.where(n > 0.0, n * lax.rsqrt(n * (1.0 - th)), 0.0)
                if first_tile is not None and g0 == s * seg:
                    mult = jnp.where((rowid == 0) & first_tile, 1.0, mult)
                bg = mult * (i * xc_ref[rows, :])
                for d in (1, 2, 4):
                    keep = rowid >= d
                    a_sh = jnp.where(keep, pltpu.roll(ag, d, 0), 1.0)
                    b_sh = jnp.where(keep, pltpu.roll(bg, d, 0), 0.0)
                    bg = bg + ag * b_sh
                    ag = ag * a_sh
                hg = bg + ag * hp
                hp = jnp.broadcast_to(hg[SUBLANES - 1:SUBLANES, :], (SUBLANES, w_b))
                ybs.append(hg * jax.nn.gelu(gg_ref[rows, :]))
            rows = slice(r0, r0 + PACK_ROWS)
            yb = jnp.concatenate(ybs, axis=0)
            cat_ref[rows, w_a:w_a + w_b] = _rms(yb, _row(p5_ref, P_GNB)).astype(_bf16)
            u = gu_ref[rows, :]
            if never is not None:
                u = jnp.where(never, jnp.concatenate([hp, hp], axis=0), u)
            ya = jax.nn.gelu(u) * (mixed_ref[rows, :] + bmix_ref[r0 % chunk:r0 % chunk + PACK_ROWS, :])
            cat_ref[rows, 0:w_a] = _rms(ya, _row(p5_ref, P_GNA)).astype(_bf16)
        hcar_ref[s] = hp


def _down_stage(h1_ref, hid_ref, p10_ref, wdn_ref, y_ref, *, final_norm):
    out = h1_ref[...] + jnp.dot(hid_ref[...], wdn_ref[...], preferred_element_type=_f32)
    y_ref[...] = _rms(out, _row(p10_ref, P_NORMF)) if final_norm else out


def _up_phases(x_ref, cat_ref, p10_ref, wout_ref, wup_ref, h1_ref, hn_ref, hid_ref, anchor=None):
    d_ff = wup_ref.shape[-1]
    x = x_ref[...]
    if anchor is not None:
        never, anchored_ref = anchor
        x = jnp.where(never, anchored_ref[...], x)
    h1_ref[...] = x + jnp.dot(cat_ref[...], wout_ref[...], preferred_element_type=_f32)
    yield
    hn_ref[...] = _rms(h1_ref[...], _row(p10_ref, P_NORM2)).astype(_bf16)
    for c in range(d_ff // FF_CHUNK):
        if c:
            yield
        cols = slice(c * FF_CHUNK, (c + 1) * FF_CHUNK)
        up = jnp.dot(hn_ref[...], wup_ref[:, cols], preferred_element_type=_f32)
        hid_ref[:, cols] = jnp.square(jnp.maximum(up.astype(_bf16), 0.0))


_PIPELINE_ORDER = "UPUUUUP"
_PIPELINE_DEPTH = 2


def _pipelined_body(x_ref, xres_ref, hdr_ref, h0_ref, p5_ref, p10_ref, win_hbm, wmix_ref, bmix_ref, wgate_ref,
                    wout_hbm, wup_hbm, wdn_hbm, y_ref, stc_ref, sth_ref,
                    xn_ref, z_ref, xpad_ref, gu_ref, gg_ref, mixed_ref, xc_ref, pre_ref, hcar_ref, cat_ref,
                    h1_ref, hn_ref, hid_ref, wmixm_ref, win_ref, wout_ref, wup_ref, wdn_ref,
                    stage_ref, dma_sem, *, tm, chunk, n_tiles, tiles_per_seq, final_norm):
    s = pl.program_id(0)
    t_proj = jnp.minimum(s, n_tiles - 1) % tiles_per_seq
    t_gate = jnp.clip(s - 1, 0, n_tiles - 1) % tiles_per_seq

    @pl.when(s == 0)
    def _():
        for ref in (gu_ref, gg_ref, mixed_ref, xc_ref, pre_ref, h1_ref, hid_ref):
            ref[...] = jnp.zeros_like(ref)
        for src_hbm, dst_ref in ((win_hbm, win_ref), (wout_hbm, wout_ref), (wup_hbm, wup_ref), (wdn_hbm, wdn_ref)):
            _load_weight_bf16(src_hbm, dst_ref, stage_ref, dma_sem)

    @pl.when(t_proj == 0)
    def _():
        _init_proj(hdr_ref, wmix_ref, xpad_ref, wmixm_ref, chunk=chunk)

    @pl.when(t_gate == 0)
    def _():
        hcar_ref[...] = h0_ref[0]

    _gate_stage(p5_ref, bmix_ref, gu_ref, gg_ref, mixed_ref, xc_ref, pre_ref, hcar_ref, cat_ref,
                tm=tm, chunk=chunk, seg=tm, first_tile=(t_gate == 0), never=(s < 0))
    _down_stage(h1_ref, hid_ref, p10_ref, wdn_ref, y_ref, final_norm=final_norm)
    up = _up_phases(xres_ref, cat_ref, p10_ref, wout_ref, wup_ref, h1_ref, hn_ref, hid_ref, anchor=(s < 0, y_ref))
    proj = _proj_phases(x_ref, p5_ref, p10_ref, win_ref, wgate_ref, None, wmixm_ref,
                        xn_ref, z_ref, xpad_ref, gu_ref, gg_ref, mixed_ref, xc_ref, pre_ref,
                        tm=tm, chunk=chunk, seg=tm)
    for who in _PIPELINE_ORDER:
        next(proj if who == "P" else up, None)
    assert next(proj, "done") == "done" and next(up, "done") == "done"

    @pl.when((t_proj == tiles_per_seq - 1) & (s < n_tiles))
    def _():
        stc_ref[0] = xpad_ref[:, 0:SUBLANES, :]

    @pl.when((t_gate == tiles_per_seq - 1) & (s >= 1) & (s <= n_tiles))
    def _():
        sth_ref[0] = hcar_ref[...]


def _single_body(x_ref, hdr_ref, h0_ref, p5_ref, p10_ref, win_ref, wmix_ref, bmix_ref, wgate_ref,
                 wout_ref, wup_ref, wdn_ref, y_ref, stc_ref, sth_ref, v_ref,
                 xn_ref, z_ref, xpad_ref, gu_ref, gg_ref, mixed_ref, xc_ref, pre_ref, hcar_ref, cat_ref,
                 h1_ref, hn_ref, hid_ref, wmixm_ref, *, tm, chunk, seg, final_norm):
    _init_proj(hdr_ref, wmix_ref, xpad_ref, wmixm_ref, chunk=chunk)
    hcar_ref[...] = h0_ref[0]
    for _ in _proj_phases(x_ref, p5_ref, p10_ref, win_ref, wgate_ref, v_ref, wmixm_ref,
                          xn_ref, z_ref, xpad_ref, gu_ref, gg_ref, mixed_ref, xc_ref, pre_ref,
                          tm=tm, chunk=chunk, seg=seg):
        pass
    _gate_stage(p5_ref, bmix_ref, gu_ref, gg_ref, mixed_ref, xc_ref, pre_ref, hcar_ref, cat_ref,
                tm=tm, chunk=chunk, seg=seg, first_tile=None)
    for _ in _up_phases(x_ref, cat_ref, p10_ref, wout_ref, wup_ref, h1_ref, hn_ref, hid_ref):
        pass
    _down_stage(h1_ref, hid_ref, p10_ref, wdn_ref, y_ref, final_norm=final_norm)
    stc_ref[0] = xpad_ref[:, 0:SUBLANES, :]
    sth_ref[0] = hcar_ref[...]


def _resident(shape):
    nd = len(shape)
    return pl.BlockSpec(shape, lambda i: (0,) * nd, pipeline_mode=pl.Buffered(1))


def _scratch(tm, nseg, seg, d_model, d_ff, w_a, w_b, wmix_shape):
    return [
        pltpu.VMEM((tm, d_model), _bf16),
        pltpu.VMEM((tm, 2 * w_a + 2 * w_b), _f32),
        pltpu.VMEM((nseg, seg + SUBLANES, w_b), _f32),
        pltpu.VMEM((tm, w_a), _f32),
        pltpu.VMEM((tm, w_b), _f32),
        pltpu.VMEM((tm, w_a), _f32),
        pltpu.VMEM((tm, w_b), _f32),
        pltpu.VMEM((tm, 2 * w_b), _f32),
        pltpu.VMEM((nseg, SUBLANES, w_b), _f32),
        pltpu.VMEM((tm, w_a + w_b), _bf16),
        pltpu.VMEM((tm, d_model), _f32),
        pltpu.VMEM((tm, d_model), _bf16),
        pltpu.VMEM((tm, d_ff), _bf16),
        pltpu.VMEM(wmix_shape, _bf16),
    ]


def _run_prompt_layer(x, hdr, h0, weights, *, tm, final_norm):
    p5, p10, win, wmix, bmix, wgate, wout, wup, wdn = weights
    nb, T, d_model = x.shape
    w_b = p5.shape[-1]
    w_a = bmix.shape[-1]
    tiles_per_seq = T // tm
    n_tiles = nb * tiles_per_seq
    x2 = x.reshape(nb * T, d_model)
    body = functools.partial(_pipelined_body, tm=tm, chunk=GMLP_CHUNK, n_tiles=n_tiles,
                             tiles_per_seq=tiles_per_seq, final_norm=final_norm)
    tile = lambda lag: (lambda s: (jnp.clip(s - lag, 0, n_tiles - 1), 0))
    seq = lambda lag: (lambda s: (jnp.clip(s - lag, 0, n_tiles - 1) // tiles_per_seq, 0, 0, 0))
    in_hbm = pl.BlockSpec(memory_space=pl.ANY)
    state_block = (1, 1, SUBLANES, w_b)
    in_specs = [
        pl.BlockSpec((tm, d_model), tile(0)),
        pl.BlockSpec((tm, d_model), tile(1)),
        pl.BlockSpec(state_block, seq(0)),
        pl.BlockSpec(state_block, seq(1)),
        _resident(p5.shape), _resident(p10.shape), in_hbm, _resident(wmix.shape), _resident(bmix.shape),
        _resident(wgate.shape), in_hbm, in_hbm, in_hbm,
    ]
    y, stc, sth = pl.pallas_call(
        body,
        grid=(n_tiles + _PIPELINE_DEPTH,),
        in_specs=in_specs,
        out_specs=[pl.BlockSpec((tm, d_model), tile(_PIPELINE_DEPTH)),
                   pl.BlockSpec(state_block, seq(0)), pl.BlockSpec(state_block, seq(1))],
        out_shape=[jax.ShapeDtypeStruct((nb * T, d_model), _f32),
                   jax.ShapeDtypeStruct((nb, 1, SUBLANES, w_b), _f32),
                   jax.ShapeDtypeStruct((nb, 1, SUBLANES, w_b), _f32)],
        scratch_shapes=_scratch(tm, 1, tm, d_model, wup.shape[-1], w_a, w_b, wmix.shape)
        + [pltpu.VMEM(w.shape, _bf16) for w in (win, wout, wup, wdn)]
        + [pltpu.VMEM((2, STAGE_ROWS, STAGE_COLS), _f32), pltpu.SemaphoreType.DMA((2,))],
        compiler_params=pltpu.CompilerParams(dimension_semantics=("arbitrary",),
                                             vmem_limit_bytes=VMEM_LIMIT_BYTES),
        name="prompt_layer",
    )(x2, x2, hdr, h0, *weights)
    return y.reshape(nb, T, d_model), stc, sth


def _run_sample_layer(x, hdr, h0, weights, *, chunk, final_norm):
    p5, p10, win, wmix, bmix, wgate, wout, wup, wdn = weights
    ns, ts, d_model = x.shape
    w_b = p5.shape[-1]
    w_a = bmix.shape[-1]
    tm = ns * ts
    body = functools.partial(_single_body, tm=tm, chunk=chunk, seg=ts, final_norm=final_norm)
    whole = lambda shape: pl.BlockSpec(shape, lambda i: (0,) * len(shape))
    state_shape = (1, ns, SUBLANES, w_b)
    y, stc, sth, v = pl.pallas_call(
        body,
        grid=(1,),
        in_specs=[whole((tm, d_model)), whole(hdr.shape), whole(h0.shape)] + [_resident(w.shape) for w in weights],
        out_specs=[whole((tm, d_model)), whole(state_shape), whole(state_shape), whole((tm, w_a))],
        out_shape=[jax.ShapeDtypeStruct((tm, d_model), _f32),
                   jax.ShapeDtypeStruct(state_shape, _f32),
                   jax.ShapeDtypeStruct(state_shape, _f32),
                   jax.ShapeDtypeStruct((tm, w_a), _f32)],
        scratch_shapes=_scratch(tm, ns, ts, d_model, wup.shape[-1], w_a, w_b, wmix.shape),
        compiler_params=pltpu.CompilerParams(dimension_semantics=("arbitrary",),
                                             vmem_limit_bytes=VMEM_LIMIT_BYTES),
        name="sample_layer",
    )(x.reshape(tm, d_model), hdr, h0, *weights)
    return y.reshape(ns, ts, d_model), stc, sth, v.reshape(ns, ts, w_a)


def _block_diag(w):
    n, k, _ = w.shape
    eye = jnp.eye(n, dtype=w.dtype)
    return (eye[:, None, :, None] * w[:, :, None, :]).reshape(n * k, n * k)


def _prep_weights(l, chunk, big_dtype, norm1_g, w_in, ln_v_g, ln_v_b, w_s, b_s, conv_w, conv_b, w_a, b_a, w_x, b_x,
                  lam, gn_a_g, gn_b_g, w_out, norm2_g, w_up, w_down, normf_g):
    w_b = conv_b.shape[-1]
    d_model = norm1_g.shape[-1]
    hd_a = gn_a_g.shape[-1] // H_A
    heads_per_tile = MXU_DIM // hd_a
    rows5 = [ln_v_g[l], ln_v_b[l], conv_w[l, 0], conv_w[l, 1], conv_w[l, 2], conv_w[l, 3], conv_b[l],
             b_a[l], b_x[l], lam[l], gn_a_g[l], gn_b_g[l]]
    p5 = jnp.stack(rows5 + [jnp.zeros((w_b,), _f32)] * (16 - len(rows5)))
    p10 = jnp.stack([norm1_g[l], norm2_g[l], normf_g] + [jnp.zeros((d_model,), _f32)] * 5)
    ws = w_s[l][:, :chunk, :chunk]
    wmix = ws.reshape(H_A // heads_per_tile, heads_per_tile, chunk, chunk).transpose(0, 2, 1, 3)
    wmix = wmix.reshape(H_A // heads_per_tile, chunk, heads_per_tile * chunk)
    bmix = jnp.repeat(b_s[l][:, :chunk].T, hd_a, axis=1)
    hh = H_B // 2
    wgate = jnp.stack([jnp.concatenate([_block_diag(w_a[l, j * hh:(j + 1) * hh]),
                                        _block_diag(w_x[l, j * hh:(j + 1) * hh])], axis=1)
                       for j in range(2)]).astype(_bf16)
    big = (w_in[l], w_out[l], w_up[l], w_down[l])
    if big_dtype is not None:
        big = tuple(w.astype(big_dtype) for w in big)
    return (p5, p10, big[0], wmix, bmix, wgate, big[1], big[2], big[3])


def _state_rows(conv_state, h_state):
    n, k, w = conv_state.shape
    hdr = jnp.concatenate([jnp.zeros((n, SUBLANES - k, w), conv_state.dtype), conv_state], axis=1)
    h0 = jnp.broadcast_to(h_state[:, None, :], (n, SUBLANES, w))
    return hdr, h0


def kernel(x_prompt, x_sample, state_conv_b, state_h_b, norm1_g, w_in, ln_v_g, ln_v_b, w_s, b_s, conv_w, conv_b,
           w_a, b_a, w_x, b_x, lam, gn_a_g, gn_b_g, w_out, norm2_g, w_up, w_down, normf_g):
    depth = w_in.shape[0]
    nb = x_prompt.shape[0]
    ns, ts, _ = x_sample.shape
    w_b = conv_b.shape[-1]
    params = (norm1_g, w_in, ln_v_g, ln_v_b, w_s, b_s, conv_w, conv_b, w_a, b_a, w_x, b_x, lam, gn_a_g, gn_b_g,
              w_out, norm2_g, w_up, w_down, normf_g)
    chunk_s = GMLP_CHUNK if ts % GMLP_CHUNK == 0 else ts
    tail = slice(SUBLANES - (CONV_W - 1), SUBLANES)
    hp, hs = x_prompt, x_sample
    conv_p, hlast_p, conv_s, hlast_s, v_s = [], [], [], [], []
    for l in range(depth):
        last = l == depth - 1
        hdr, h0 = _state_rows(jnp.zeros((nb, CONV_W - 1, w_b), _f32), jnp.zeros((nb, w_b), _f32))
        hp, stc, sth = _run_prompt_layer(hp, hdr[:, None], h0[:, None],
                                         _prep_weights(l, GMLP_CHUNK, None, *params),
                                         tm=PROMPT_TM, final_norm=last)
        conv_p.append(stc[:, 0, tail])
        hlast_p.append(sth[:, 0, SUBLANES - 1])
        hdr, h0 = _state_rows(state_conv_b[l], state_h_b[l])
        hs, stc, sth, vv = _run_sample_layer(hs, hdr[None], h0[None], _prep_weights(l, chunk_s, _bf16, *params),
                                             chunk=chunk_s, final_norm=last)
        conv_s.append(stc[0, :, tail])
        hlast_s.append(sth[0, :, SUBLANES - 1])
        v_s.append(vv)
    return (hp, hs, jnp.stack(conv_p), jnp.stack(hlast_p), jnp.stack(conv_s), jnp.stack(hlast_s), jnp.stack(v_s))
```

```python
import functools

import jax
import jax.numpy as jnp
from jax import lax
from jax.experimental import pallas as pl
from jax.experimental.pallas import tpu as pltpu

H_A = 8
H_B = 8
GMLP_CHUNK = 128
CONV_W = 4
LRU_C = 8.0
EPS = 1e-6

SUBLANES = 8
PACK_ROWS = 16
MXU_DIM = 256
FF_CHUNK = 1024
PROMPT_TM = 256
STAGE_ROWS, STAGE_COLS = 256, 1024
VMEM_LIMIT_BYTES = 56 * 1024 * 1024

P_LNV_G, P_LNV_B, P_CONV_W, P_CONV_B, P_BA, P_BX, P_LAM, P_GNA, P_GNB = 0, 1, 2, 6, 7, 8, 9, 10, 11
P_NORM1, P_NORM2, P_NORMF = 0, 1, 2

_f32 = jnp.float32
_bf16 = jnp.bfloat16


def _rms(x, g):
    return x * lax.rsqrt(jnp.mean(x * x, axis=-1, keepdims=True) + EPS) * g


def _layernorm(x, g, b):
    mu = jnp.mean(x, axis=-1, keepdims=True)
    xc = x - mu
    var = jnp.mean(xc * xc, axis=-1, keepdims=True)
    return xc * lax.rsqrt(var + EPS) * g + b


def _sigmoid(x):
    return 0.5 * (1.0 + jnp.tanh(0.5 * x))


def _row(ref, r):
    return ref[r:r + 1, :]


def _load_weight_bf16(src_hbm, dst_ref, stage_ref, sem):
    n_rows, n_cols = src_hbm.shape
    _, stage_rows, stage_cols = stage_ref.shape
    n_chunks = n_rows // stage_rows

    for c0 in range(0, n_cols, stage_cols):
        def chunk_copy(k, slot, c0=c0):
            r0 = pl.multiple_of(k * stage_rows, stage_rows)
            return pltpu.make_async_copy(src_hbm.at[pl.ds(r0, stage_rows), pl.ds(c0, stage_cols)],
                                         stage_ref.at[slot], sem.at[slot])

        chunk_copy(0, 0).start()

        def body(k, carry, c0=c0, chunk_copy=chunk_copy):
            slot = k % 2

            @pl.when(k + 1 < n_chunks)
            def _():
                chunk_copy(k + 1, 1 - slot).start()

            chunk_copy(k, slot).wait()
            r0 = pl.multiple_of(k * stage_rows, stage_rows)
            dst_ref[pl.ds(r0, stage_rows), c0:c0 + stage_cols] = stage_ref[slot].astype(_bf16)
            return carry

        lax.fori_loop(0, n_chunks, body, 0)


def _init_proj(hdr_ref, wmix_ref, xpad_ref, wmixm_ref, *, chunk):
    n_tiles_a, _, kcat = wmix_ref.shape
    row = lax.broadcasted_iota(jnp.int32, (chunk, kcat), 0)
    col = lax.broadcasted_iota(jnp.int32, (chunk, kcat), 1) % chunk
    for q in range(n_tiles_a):
        wmixm_ref[q] = jnp.where(col <= row, wmix_ref[q], 0.0).astype(_bf16)
    xpad_ref[:, 0:SUBLANES, :] = hdr_ref[0]


def _proj_phases(x_ref, p5_ref, p10_ref, win_ref, wgate_ref, v_ref, wmixm_ref,
                 xn_ref, z_ref, xpad_ref, gu_ref, gg_ref, mixed_ref, xc_ref, pre_ref, *, tm, chunk, seg):
    nseg = tm // seg
    nchunk = tm // chunk
    w_a = gu_ref.shape[-1]
    w_b = p5_ref.shape[-1]
    hd_a = w_a // H_A
    heads_per_tile = MXU_DIM // hd_a
    n_tiles_a = w_a // MXU_DIM

    x = x_ref[...]
    xn_ref[...] = (x * _row(p10_ref, P_NORM1)).astype(_bf16)
    r1 = lax.rsqrt(jnp.mean(x * x, axis=-1, keepdims=True) + EPS)
    d_mix = w_a + w_b
    z_ref[:, 0:d_mix] = jnp.dot(xn_ref[...], win_ref[:, 0:d_mix], preferred_element_type=_f32) * r1
    yield
    z_ref[:, d_mix:2 * d_mix] = jnp.dot(xn_ref[...], win_ref[:, d_mix:2 * d_mix], preferred_element_type=_f32) * r1
    yield

    gu_ref[...] = z_ref[:, 0:w_a]
    gg_ref[...] = z_ref[:, 2 * w_a + w_b:2 * w_a + 2 * w_b]

    v = _layernorm(jax.nn.gelu(z_ref[:, w_a:2 * w_a]), _row(p5_ref, P_LNV_G), _row(p5_ref, P_LNV_B))
    if v_ref is not None:
        v_ref[...] = v
    lane = lax.broadcasted_iota(jnp.int32, (chunk, MXU_DIM), 1)
    for c in range(nchunk):
        rows = slice(c * chunk, (c + 1) * chunk)
        for q in range(n_tiles_a):
            cols = slice(q * MXU_DIM, (q + 1) * MXU_DIM)
            vq = v[rows, cols]
            rhs = jnp.concatenate(
                [jnp.where((lane >= hd_a * j) & (lane < hd_a * (j + 1)), vq, 0.0)
                 for j in range(heads_per_tile)], axis=0).astype(_bf16)
            mixed_ref[rows, cols] = jnp.dot(wmixm_ref[q], rhs, preferred_element_type=_f32)

    for s in range(nseg):
        xpad_ref[s, SUBLANES:SUBLANES + seg, :] = z_ref[s * seg:(s + 1) * seg, 2 * w_a:2 * w_a + w_b]
    xcs = []
    for s in range(nseg):
        acc = _row(p5_ref, P_CONV_B)
        for k in range(CONV_W):
            off = SUBLANES - (CONV_W - 1) + k
            acc = acc + xpad_ref[s, off:off + seg, :] * _row(p5_ref, P_CONV_W + k)
        xcs.append(acc)
        xpad_ref[s, 0:SUBLANES, :] = xpad_ref[s, seg:seg + SUBLANES, :]
    xc = jnp.concatenate(xcs, axis=0) if nseg > 1 else xcs[0]
    xc_ref[...] = xc
    xcb = xc.astype(_bf16)
    half = w_b // 2
    for j in range(2):
        res = jnp.dot(xcb[:, j * half:(j + 1) * half], wgate_ref[j], preferred_element_type=_f32)
        pre_ref[:, j * half:(j + 1) * half] = res[:, :half]
        pre_ref[:, w_b + j * half:w_b + (j + 1) * half] = res[:, half:]


def _gate_stage(p5_ref, bmix_ref, gu_ref, gg_ref, mixed_ref, xc_ref, pre_ref, hcar_ref, cat_ref,
                *, tm, chunk, seg, first_tile, never=None):
    nseg = tm // seg
    w_a = gu_ref.shape[-1]
    w_b = p5_ref.shape[-1]

    rowid = lax.broadcasted_iota(jnp.int32, (SUBLANES, w_b), 0)
    for s in range(nseg):
        hp = hcar_ref[s]
        for r0 in range(s * seg, (s + 1) * seg, PACK_ROWS):
            ybs = []
            for g0 in range(r0, r0 + PACK_ROWS, SUBLANES):
                rows = slice(g0, g0 + SUBLANES)
                pre_r = pre_ref[rows, 0:w_b]
                pre_i = pre_ref[rows, w_b:2 * w_b]
                if never is not None and g0 > s * seg:
                    pre_r = jnp.where(never, hp, pre_r)
                    pre_i = jnp.where(never, hp, pre_i)
                r = _sigmoid(pre_r + _row(p5_ref, P_BA))
                i = _sigmoid(pre_i + _row(p5_ref, P_BX))
                log_a = -LRU_C * r * jax.nn.softplus(-_row(p5_ref, P_LAM))
                ag = jnp.exp(log_a)
                th = jnp.tanh(log_a)
                n = -2.0 * th
                mult = jnp.where(n > 0.0, n * lax.rsqrt(n * (1.0 - th)), 0.0)
                if first_tile is not None and g0 == s * seg:
                    mult = jnp.where((rowid == 0) & first_tile, 1.0, mult)
                bg = mult * (i * xc_ref[rows, :])
                for d in (1, 2, 4):
                    keep = rowid >= d
                    a_sh = jnp.where(keep, pltpu.roll(ag, d, 0), 1.0)
                    b_sh = jnp.where(keep, pltpu.roll(bg, d, 0), 0.0)
                    bg = bg + ag * b_sh
                    ag = ag * a_sh
                hg = bg + ag * hp
                hp = jnp.broadcast_to(hg[SUBLANES - 1:SUBLANES, :], (SUBLANES, w_b))
                ybs.append(hg * jax.nn.gelu(gg_ref[rows, :]))
            rows = slice(r0, r0 + PACK_ROWS)
            yb = jnp.concatenate(ybs, axis=0)
            cat_ref[rows, w_a:w_a + w_b] = _rms(yb, _row(p5_ref, P_GNB)).astype(_bf16)
            u = gu_ref[rows, :]
            if never is not None:
                u = jnp.where(never, jnp.concatenate([hp, hp], axis=0), u)
            ya = jax.nn.gelu(u) * (mixed_ref[rows, :] + bmix_ref[r0 % chunk:r0 % chunk + PACK_ROWS, :])
            cat_ref[rows, 0:w_a] = _rms(ya, _row(p5_ref, P_GNA)).astype(_bf16)
        hcar_ref[s] = hp


def _down_stage(h1_ref, hid_ref, r2_ref, p10_ref, wdn_ref, y_ref, *, final_norm):
    out = h1_ref[...] + jnp.dot(hid_ref[...], wdn_ref[...], preferred_element_type=_f32) * r2_ref[:, 0:1]
    y_ref[...] = _rms(out, _row(p10_ref, P_NORMF)) if final_norm else out


def _up_phases(x_ref, cat_ref, p10_ref, wout_ref, wup_ref, h1_ref, hn_ref, hid_ref, r2_ref, anchor=None):
    d_ff = wup_ref.shape[-1]
    x = x_ref[...]
    if anchor is not None:
        never, anchored_ref = anchor
        x = jnp.where(never, anchored_ref[...], x)
    h1_ref[...] = x + jnp.dot(cat_ref[...], wout_ref[...], preferred_element_type=_f32)
    yield
    h1 = h1_ref[...]
    hn_ref[...] = (h1 * _row(p10_ref, P_NORM2)).astype(_bf16)
    r2_ref[...] = jnp.broadcast_to(1.0 / (jnp.mean(h1 * h1, axis=-1, keepdims=True) + EPS), r2_ref.shape)
    for c in range(d_ff // FF_CHUNK):
        if c:
            yield
        cols = slice(c * FF_CHUNK, (c + 1) * FF_CHUNK)
        up = jnp.dot(hn_ref[...], wup_ref[:, cols], preferred_element_type=_f32)
        hid_ref[:, cols] = jnp.square(jnp.maximum(up.astype(_bf16), 0.0))


_PIPELINE_ORDER = "PUPUUUUP"
_PIPELINE_DEPTH = 2


def _pipelined_body(x_ref, xres_ref, hdr_ref, h0_ref, p5_ref, p10_ref, win_hbm, wmix_ref, bmix_ref, wgate_ref,
                    wout_hbm, wup_hbm, wdn_hbm, y_ref, stc_ref, sth_ref,
                    xn_ref, z_ref, xpad_ref, gu_ref, gg_ref, mixed_ref, xc_ref, pre_ref, hcar_ref, cat_ref,
                    h1_ref, hn_ref, hid_ref, r2_ref, wmixm_ref, win_ref, wout_ref, wup_ref, wdn_ref,
                    stage_ref, dma_sem, *, tm, chunk, n_tiles, tiles_per_seq, final_norm):
    s = pl.program_id(0)
    t_proj = jnp.minimum(s, n_tiles - 1) % tiles_per_seq
    t_gate = jnp.clip(s - 1, 0, n_tiles - 1) % tiles_per_seq

    @pl.when(s == 0)
    def _():
        for ref in (gu_ref, gg_ref, mixed_ref, xc_ref, pre_ref, h1_ref, hid_ref, r2_ref):
            ref[...] = jnp.zeros_like(ref)
        for src_hbm, dst_ref in ((win_hbm, win_ref), (wout_hbm, wout_ref), (wup_hbm, wup_ref), (wdn_hbm, wdn_ref)):
            _load_weight_bf16(src_hbm, dst_ref, stage_ref, dma_sem)

    @pl.when(t_proj == 0)
    def _():
        _init_proj(hdr_ref, wmix_ref, xpad_ref, wmixm_ref, chunk=chunk)

    @pl.when(t_gate == 0)
    def _():
        hcar_ref[...] = h0_ref[0]

    _gate_stage(p5_ref, bmix_ref, gu_ref, gg_ref, mixed_ref, xc_ref, pre_ref, hcar_ref, cat_ref,
                tm=tm, chunk=chunk, seg=tm, first_tile=(t_gate == 0), never=(s < 0))
    _down_stage(h1_ref, hid_ref, r2_ref, p10_ref, wdn_ref, y_ref, final_norm=final_norm)
    up = _up_phases(xres_ref, cat_ref, p10_ref, wout_ref, wup_ref, h1_ref, hn_ref, hid_ref, r2_ref,
                    anchor=(s < 0, y_ref))
    proj = _proj_phases(x_ref, p5_ref, p10_ref, win_ref, wgate_ref, None, wmixm_ref,
                        xn_ref, z_ref, xpad_ref, gu_ref, gg_ref, mixed_ref, xc_ref, pre_ref,
                        tm=tm, chunk=chunk, seg=tm)
    for who in _PIPELINE_ORDER:
        next(proj if who == "P" else up, None)
    assert next(proj, "done") == "done" and next(up, "done") == "done"

    @pl.when((t_proj == tiles_per_seq - 1) & (s < n_tiles))
    def _():
        stc_ref[0] = xpad_ref[:, 0:SUBLANES, :]

    @pl.when((t_gate == tiles_per_seq - 1) & (s >= 1) & (s <= n_tiles))
    def _():
        sth_ref[0] = hcar_ref[...]


def _single_body(x_ref, hdr_ref, h0_ref, p5_ref, p10_ref, win_ref, wmix_ref, bmix_ref, wgate_ref,
                 wout_ref, wup_ref, wdn_ref, y_ref, stc_ref, sth_ref, v_ref,
                 xn_ref, z_ref, xpad_ref, gu_ref, gg_ref, mixed_ref, xc_ref, pre_ref, hcar_ref, cat_ref,
                 h1_ref, hn_ref, hid_ref, r2_ref, wmixm_ref, *, tm, chunk, seg, final_norm):
    _init_proj(hdr_ref, wmix_ref, xpad_ref, wmixm_ref, chunk=chunk)
    hcar_ref[...] = h0_ref[0]
    for _ in _proj_phases(x_ref, p5_ref, p10_ref, win_ref, wgate_ref, v_ref, wmixm_ref,
                          xn_ref, z_ref, xpad_ref, gu_ref, gg_ref, mixed_ref, xc_ref, pre_ref,
                          tm=tm, chunk=chunk, seg=seg):
        pass
    _gate_stage(p5_ref, bmix_ref, gu_ref, gg_ref, mixed_ref, xc_ref, pre_ref, hcar_ref, cat_ref,
                tm=tm, chunk=chunk, seg=seg, first_tile=None)
    for _ in _up_phases(x_ref, cat_ref, p10_ref, wout_ref, wup_ref, h1_ref, hn_ref, hid_ref, r2_ref):
        pass
    _down_stage(h1_ref, hid_ref, r2_ref, p10_ref, wdn_ref, y_ref, final_norm=final_norm)
    stc_ref[0] = xpad_ref[:, 0:SUBLANES, :]
    sth_ref[0] = hcar_ref[...]


def _resident(shape):
    nd = len(shape)
    return pl.BlockSpec(shape, lambda i: (0,) * nd, pipeline_mode=pl.Buffered(1))


def _scratch(tm, nseg, seg, d_model, d_ff, w_a, w_b, wmix_shape):
    return [
        pltpu.VMEM((tm, d_model), _bf16),
        pltpu.VMEM((tm, 2 * w_a + 2 * w_b), _f32),
        pltpu.VMEM((nseg, seg + SUBLANES, w_b), _f32),
        pltpu.VMEM((tm, w_a), _f32),
        pltpu.VMEM((tm, w_b), _f32),
        pltpu.VMEM((tm, w_a), _f32),
        pltpu.VMEM((tm, w_b), _f32),
        pltpu.VMEM((tm, 2 * w_b), _f32),
        pltpu.VMEM((nseg, SUBLANES, w_b), _f32),
        pltpu.VMEM((tm, w_a + w_b), _bf16),
        pltpu.VMEM((tm, d_model), _f32),
        pltpu.VMEM((tm, d_model), _bf16),
        pltpu.VMEM((tm, d_ff), _bf16),
        pltpu.VMEM((tm, 128), _f32),
        pltpu.VMEM(wmix_shape, _bf16),
    ]


def _run_prompt_layer(x, hdr, h0, weights, *, tm, final_norm):
    p5, p10, win, wmix, bmix, wgate, wout, wup, wdn = weights
    nb, T, d_model = x.shape
    w_b = p5.shape[-1]
    w_a = bmix.shape[-1]
    tiles_per_seq = T // tm
    n_tiles = nb * tiles_per_seq
    x2 = x.reshape(nb * T, d_model)
    body = functools.partial(_pipelined_body, tm=tm, chunk=GMLP_CHUNK, n_tiles=n_tiles,
                             tiles_per_seq=tiles_per_seq, final_norm=final_norm)
    tile = lambda lag: (lambda s: (jnp.clip(s - lag, 0, n_tiles - 1), 0))
    seq = lambda lag: (lambda s: (jnp.clip(s - lag, 0, n_tiles - 1) // tiles_per_seq, 0, 0, 0))
    in_hbm = pl.BlockSpec(memory_space=pl.ANY)
    state_block = (1, 1, SUBLANES, w_b)
    in_specs = [
        pl.BlockSpec((tm, d_model), tile(0)),
        pl.BlockSpec((tm, d_model), tile(1)),
        pl.BlockSpec(state_block, seq(0)),
        pl.BlockSpec(state_block, seq(1)),
        _resident(p5.shape), _resident(p10.shape), in_hbm, _resident(wmix.shape), _resident(bmix.shape),
        _resident(wgate.shape), in_hbm, in_hbm, in_hbm,
    ]
    y, stc, sth = pl.pallas_call(
        body,
        grid=(n_tiles + _PIPELINE_DEPTH,),
        in_specs=in_specs,
        out_specs=[pl.BlockSpec((tm, d_model), tile(_PIPELINE_DEPTH)),
                   pl.BlockSpec(state_block, seq(0)), pl.BlockSpec(state_block, seq(1))],
        out_shape=[jax.ShapeDtypeStruct((nb * T, d_model), _f32),
                   jax.ShapeDtypeStruct((nb, 1, SUBLANES, w_b), _f32),
                   jax.ShapeDtypeStruct((nb, 1, SUBLANES, w_b), _f32)],
        scratch_shapes=_scratch(tm, 1, tm, d_model, wup.shape[-1], w_a, w_b, wmix.shape)
        + [pltpu.VMEM(w.shape, _bf16) for w in (win, wout, wup, wdn)]
        + [pltpu.VMEM((2, STAGE_ROWS, STAGE_COLS), _f32), pltpu.SemaphoreType.DMA((2,))],
        compiler_params=pltpu.CompilerParams(dimension_semantics=("arbitrary",),
                                             vmem_limit_bytes=VMEM_LIMIT_BYTES),
        name="prompt_layer",
    )(x2, x2, hdr, h0, *weights)
    return y.reshape(nb, T, d_model), stc, sth


def _run_sample_layer(x, hdr, h0, weights, *, chunk, final_norm):
    p5, p10, win, wmix, bmix, wgate, wout, wup, wdn = weights
    ns, ts, d_model = x.shape
    w_b = p5.shape[-1]
    w_a = bmix.shape[-1]
    tm = ns * ts
    body = functools.partial(_single_body, tm=tm, chunk=chunk, seg=ts, final_norm=final_norm)
    whole = lambda shape: pl.BlockSpec(shape, lambda i: (0,) * len(shape))
    state_shape = (1, ns, SUBLANES, w_b)
    y, stc, sth, v = pl.pallas_call(
        body,
        grid=(1,),
        in_specs=[whole((tm, d_model)), whole(hdr.shape), whole(h0.shape)] + [_resident(w.shape) for w in weights],
        out_specs=[whole((tm, d_model)), whole(state_shape), whole(state_shape), whole((tm, w_a))],
        out_shape=[jax.ShapeDtypeStruct((tm, d_model), _f32),
                   jax.ShapeDtypeStruct(state_shape, _f32),
                   jax.ShapeDtypeStruct(state_shape, _f32),
                   jax.ShapeDtypeStruct((tm, w_a), _f32)],
        scratch_shapes=_scratch(tm, ns, ts, d_model, wup.shape[-1], w_a, w_b, wmix.shape),
        compiler_params=pltpu.CompilerParams(dimension_semantics=("arbitrary",),
                                             vmem_limit_bytes=VMEM_LIMIT_BYTES),
        name="sample_layer",
    )(x.reshape(tm, d_model), hdr, h0, *weights)
    return y.reshape(ns, ts, d_model), stc, sth, v.reshape(ns, ts, w_a)


def _block_diag(w):
    n, k, _ = w.shape
    eye = jnp.eye(n, dtype=w.dtype)
    return (eye[:, None, :, None] * w[:, :, None, :]).reshape(n * k, n * k)


def _prep_weights(l, chunk, big_dtype, norm1_g, w_in, ln_v_g, ln_v_b, w_s, b_s, conv_w, conv_b, w_a, b_a, w_x, b_x,
                  lam, gn_a_g, gn_b_g, w_out, norm2_g, w_up, w_down, normf_g):
    w_b = conv_b.shape[-1]
    d_model = norm1_g.shape[-1]
    hd_a = gn_a_g.shape[-1] // H_A
    heads_per_tile = MXU_DIM // hd_a
    rows5 = [ln_v_g[l], ln_v_b[l], conv_w[l, 0], conv_w[l, 1], conv_w[l, 2], conv_w[l, 3], conv_b[l],
             b_a[l], b_x[l], lam[l], gn_a_g[l], gn_b_g[l]]
    p5 = jnp.stack(rows5 + [jnp.zeros((w_b,), _f32)] * (16 - len(rows5)))
    p10 = jnp.stack([norm1_g[l], norm2_g[l], normf_g] + [jnp.zeros((d_model,), _f32)] * 5)
    ws = w_s[l][:, :chunk, :chunk]
    wmix = ws.reshape(H_A // heads_per_tile, heads_per_tile, chunk, chunk).transpose(0, 2, 1, 3)
    wmix = wmix.reshape(H_A // heads_per_tile, chunk, heads_per_tile * chunk)
    bmix = jnp.repeat(b_s[l][:, :chunk].T, hd_a, axis=1)
    hh = H_B // 2
    wgate = jnp.stack([jnp.concatenate([_block_diag(w_a[l, j * hh:(j + 1) * hh]),
                                        _block_diag(w_x[l, j * hh:(j + 1) * hh])], axis=1)
                       for j in range(2)]).astype(_bf16)
    big = (w_in[l], w_out[l], w_up[l], w_down[l])
    if big_dtype is not None:
        big = tuple(w.astype(big_dtype) for w in big)
    return (p5, p10, big[0], wmix, bmix, wgate, big[1], big[2], big[3])


def _state_rows(conv_state, h_state):
    n, k, w = conv_state.shape
    hdr = jnp.concatenate([jnp.zeros((n, SUBLANES - k, w), conv_state.dtype), conv_state], axis=1)
    h0 = jnp.broadcast_to(h_state[:, None, :], (n, SUBLANES, w))
    return hdr, h0


def kernel(x_prompt, x_sample, state_conv_b, state_h_b, norm1_g, w_in, ln_v_g, ln_v_b, w_s, b_s, conv_w, conv_b,
           w_a, b_a, w_x, b_x, lam, gn_a_g, gn_b_g, w_out, norm2_g, w_up, w_down, normf_g):
    depth = w_in.shape[0]
    nb = x_prompt.shape[0]
    ns, ts, _ = x_sample.shape
    w_b = conv_b.shape[-1]
    params = (norm1_g, w_in, ln_v_g, ln_v_b, w_s, b_s, conv_w, conv_b, w_a, b_a, w_x, b_x, lam, gn_a_g, gn_b_g,
              w_out, norm2_g, w_up, w_down, normf_g)
    chunk_s = GMLP_CHUNK if ts % GMLP_CHUNK == 0 else ts
    tail = slice(SUBLANES - (CONV_W - 1), SUBLANES)
    hp, hs = x_prompt, x_sample
    conv_p, hlast_p, conv_s, hlast_s, v_s = [], [], [], [], []
    for l in range(depth):
        last = l == depth - 1
        hdr, h0 = _state_rows(jnp.zeros((nb, CONV_W - 1, w_b), _f32), jnp.zeros((nb, w_b), _f32))
        hp, stc, sth = _run_prompt_layer(hp, hdr[:, None], h0[:, None],
                                         _prep_weights(l, GMLP_CHUNK, None, *params),
                                         tm=PROMPT_TM, final_norm=last)
        conv_p.append(stc[:, 0, tail])
        hlast_p.append(sth[:, 0, SUBLANES - 1])
        hdr, h0 = _state_rows(state_conv_b[l], state_h_b[l])
        hs, stc, sth, vv = _run_sample_layer(hs, hdr[None], h0[None], _prep_weights(l, chunk_s, _bf16, *params),
                                             chunk=chunk_s, final_norm=last)
        conv_s.append(stc[0, :, tail])
        hlast_s.append(sth[0, :, SUBLANES - 1])
        v_s.append(vv)
    return (hp, hs, jnp.stack(conv_p), jnp.stack(hlast_p), jnp.stack(conv_s), jnp.stack(hlast_s), jnp.stack(v_s))
```

```python
import functools

import jax
import jax.numpy as jnp
from jax import lax
from jax.experimental import pallas as pl
from jax.experimental.pallas import tpu as pltpu

H_A = 8
H_B = 8
GMLP_CHUNK = 128
CONV_W = 4
LRU_C = 8.0
EPS = 1e-6

SUBLANES = 8
PACK_ROWS = 16
MXU_DIM = 256
FF_CHUNK = 1024
PROMPT_TM = 256
STAGE_ROWS, STAGE_COLS = 256, 1024
VMEM_LIMIT_BYTES = 56 * 1024 * 1024

P_LNV_G, P_LNV_B, P_CONV_W, P_CONV_B, P_BA, P_BX, P_LAM, P_GNA, P_GNB = 0, 1, 2, 6, 7, 8, 9, 10, 11
P_NORM1, P_NORM2, P_NORMF = 0, 1, 2

_f32 = jnp.float32
_bf16 = jnp.bfloat16


def _rms(x, g):
    return x * lax.rsqrt(jnp.mean(x * x, axis=-1, keepdims=True) + EPS) * g


def _layernorm(x, g, b):
    mu = jnp.mean(x, axis=-1, keepdims=True)
    xc = x - mu
    var = jnp.mean(xc * xc, axis=-1, keepdims=True)
    return xc * lax.rsqrt(var + EPS) * g + b


def _sigmoid(x):
    return 0.5 * (1.0 + jnp.tanh(0.5 * x))


def _row(ref, r):
    return ref[r:r + 1, :]


def _load_weight_bf16(src_hbm, dst_ref, stage_ref, sem):
    n_rows, n_cols = src_hbm.shape
    _, stage_rows, stage_cols = stage_ref.shape
    n_chunks = n_rows // stage_rows

    for c0 in range(0, n_cols, stage_cols):
        def chunk_copy(k, slot, c0=c0):
            r0 = pl.multiple_of(k * stage_rows, stage_rows)
            return pltpu.make_async_copy(src_hbm.at[pl.ds(r0, stage_rows), pl.ds(c0, stage_cols)],
                                         stage_ref.at[slot], sem.at[slot])

        chunk_copy(0, 0).start()

        def body(k, carry, c0=c0, chunk_copy=chunk_copy):
            slot = k % 2

            @pl.when(k + 1 < n_chunks)
            def _():
                chunk_copy(k + 1, 1 - slot).start()

            chunk_copy(k, slot).wait()
            r0 = pl.multiple_of(k * stage_rows, stage_rows)
            dst_ref[pl.ds(r0, stage_rows), c0:c0 + stage_cols] = stage_ref[slot].astype(_bf16)
            return carry

        lax.fori_loop(0, n_chunks, body, 0)


def _init_proj(hdr_ref, wmix_ref, xpad_ref, wmixm_ref, *, chunk):
    n_tiles_a, _, kcat = wmix_ref.shape
    row = lax.broadcasted_iota(jnp.int32, (chunk, kcat), 0)
    col = lax.broadcasted_iota(jnp.int32, (chunk, kcat), 1) % chunk
    for q in range(n_tiles_a):
        wmixm_ref[q] = jnp.where(col <= row, wmix_ref[q], 0.0).astype(_bf16)
    xpad_ref[:, 0:SUBLANES, :] = hdr_ref[0]


def _proj_phases(x_ref, p5_ref, p10_ref, win_ref, wgate_ref, v_ref, wmixm_ref,
                 xn_ref, z_ref, xpad_ref, gu_ref, gg_ref, mixed_ref, xc_ref, pre_ref, *, tm, chunk, seg):
    nseg = tm // seg
    nchunk = tm // chunk
    w_a = gu_ref.shape[-1]
    w_b = p5_ref.shape[-1]
    hd_a = w_a // H_A
    heads_per_tile = MXU_DIM // hd_a
    n_tiles_a = w_a // MXU_DIM

    x = x_ref[...]
    xn_ref[...] = (x * _row(p10_ref, P_NORM1)).astype(_bf16)
    r1 = lax.rsqrt(jnp.mean(x * x, axis=-1, keepdims=True) + EPS)
    d_mix = w_a + w_b
    z_ref[:, 0:d_mix] = jnp.dot(xn_ref[...], win_ref[:, 0:d_mix], preferred_element_type=_f32) * r1
    yield
    z_ref[:, d_mix:2 * d_mix] = jnp.dot(xn_ref[...], win_ref[:, d_mix:2 * d_mix], preferred_element_type=_f32) * r1
    yield

    gu_ref[...] = z_ref[:, 0:w_a]
    gg_ref[...] = z_ref[:, 2 * w_a + w_b:2 * w_a + 2 * w_b]

    v = _layernorm(jax.nn.gelu(z_ref[:, w_a:2 * w_a]), _row(p5_ref, P_LNV_G), _row(p5_ref, P_LNV_B))
    if v_ref is not None:
        v_ref[...] = v
    lane = lax.broadcasted_iota(jnp.int32, (chunk, MXU_DIM), 1)
    for c in range(nchunk):
        rows = slice(c * chunk, (c + 1) * chunk)
        for q in range(n_tiles_a):
            cols = slice(q * MXU_DIM, (q + 1) * MXU_DIM)
            vq = v[rows, cols]
            rhs = jnp.concatenate(
                [jnp.where((lane >= hd_a * j) & (lane < hd_a * (j + 1)), vq, 0.0)
                 for j in range(heads_per_tile)], axis=0).astype(_bf16)
            mixed_ref[rows, cols] = jnp.dot(wmixm_ref[q], rhs, preferred_element_type=_f32)

    for s in range(nseg):
        xpad_ref[s, SUBLANES:SUBLANES + seg, :] = z_ref[s * seg:(s + 1) * seg, 2 * w_a:2 * w_a + w_b]
    xcs = []
    for s in range(nseg):
        acc = _row(p5_ref, P_CONV_B)
        for k in range(CONV_W):
            off = SUBLANES - (CONV_W - 1) + k
            acc = acc + xpad_ref[s, off:off + seg, :] * _row(p5_ref, P_CONV_W + k)
        xcs.append(acc)
        xpad_ref[s, 0:SUBLANES, :] = xpad_ref[s, seg:seg + SUBLANES, :]
    xc = jnp.concatenate(xcs, axis=0) if nseg > 1 else xcs[0]
    xc_ref[...] = xc
    xcb = xc.astype(_bf16)
    half = w_b // 2
    for j in range(2):
        res = jnp.dot(xcb[:, j * half:(j + 1) * half], wgate_ref[j], preferred_element_type=_f32)
        pre_ref[:, j * half:(j + 1) * half] = res[:, :half]
        pre_ref[:, w_b + j * half:w_b + (j + 1) * half] = res[:, half:]


def _gate_stage(p5_ref, bmix_ref, gu_ref, gg_ref, mixed_ref, xc_ref, pre_ref, hcar_ref, cat_ref,
                *, tm, chunk, seg, first_tile, never=None):
    nseg = tm // seg
    w_a = gu_ref.shape[-1]
    w_b = p5_ref.shape[-1]

    rowid = lax.broadcasted_iota(jnp.int32, (SUBLANES, w_b), 0)
    for s in range(nseg):
        hp = hcar_ref[s]
        link = None
        for r0 in range(s * seg, (s + 1) * seg, PACK_ROWS):
            ybs = []
            for g0 in range(r0, r0 + PACK_ROWS, SUBLANES):
                rows = slice(g0, g0 + SUBLANES)
                pre_r = pre_ref[rows, 0:w_b]
                pre_i = pre_ref[rows, w_b:2 * w_b]
                if never is not None and link is not None:
                    pre_r = jnp.where(never, link[0], pre_r)
                    pre_i = jnp.where(never, link[1], pre_i)
                r = _sigmoid(pre_r + _row(p5_ref, P_BA))
                i = _sigmoid(pre_i + _row(p5_ref, P_BX))
                log_a = -LRU_C * r * jax.nn.softplus(-_row(p5_ref, P_LAM))
                ag = jnp.exp(log_a)
                th = jnp.tanh(log_a)
                n = -2.0 * th
                mult = jnp.where(n > 0.0, n * lax.rsqrt(n * (1.0 - th)), 0.0)
                if first_tile is not None and g0 == s * seg:
                    mult = jnp.where((rowid == 0) & first_tile, 1.0, mult)
                bg = mult * (i * xc_ref[rows, :])
                for d in (1, 2, 4):
                    keep = rowid >= d
                    a_sh = jnp.where(keep, pltpu.roll(ag, d, 0), 1.0)
                    b_sh = jnp.where(keep, pltpu.roll(bg, d, 0), 0.0)
                    bg = bg + ag * b_sh
                    ag = ag * a_sh
                hg = bg + ag * hp
                hp = jnp.broadcast_to(hg[SUBLANES - 1:SUBLANES, :], (SUBLANES, w_b))
                link = (hp, hp)
                ybs.append(hg * jax.nn.gelu(gg_ref[rows, :]))
            rows = slice(r0, r0 + PACK_ROWS)
            yb = jnp.concatenate(ybs, axis=0)
            cat_ref[rows, w_a:w_a + w_b] = _rms(yb, _row(p5_ref, P_GNB)).astype(_bf16)
            u = gu_ref[rows, :]
            if never is not None:
                u = jnp.where(never, jnp.concatenate([hp, hp], axis=0), u)
            ya = jax.nn.gelu(u) * (mixed_ref[rows, :] + bmix_ref[r0 % chunk:r0 % chunk + PACK_ROWS, :])
            link = (ya[0:SUBLANES], ya[SUBLANES:PACK_ROWS])
            cat_ref[rows, 0:w_a] = _rms(ya, _row(p5_ref, P_GNA)).astype(_bf16)
        hcar_ref[s] = hp


def _down_stage(h1_ref, hid_ref, r2_ref, p10_ref, wdn_ref, y_ref, *, final_norm):
    out = h1_ref[...] + jnp.dot(hid_ref[...], wdn_ref[...], preferred_element_type=_f32) * r2_ref[:, 0:1]
    y_ref[...] = _rms(out, _row(p10_ref, P_NORMF)) if final_norm else out


def _up_phases(x_ref, cat_ref, p10_ref, wout_ref, wup_ref, h1_ref, hn_ref, hid_ref, r2_ref, anchor=None):
    d_ff = wup_ref.shape[-1]
    x = x_ref[...]
    if anchor is not None:
        never, anchored_ref = anchor
        x = jnp.where(never, anchored_ref[...], x)
    h1_ref[...] = x + jnp.dot(cat_ref[...], wout_ref[...], preferred_element_type=_f32)
    yield
    h1 = h1_ref[...]
    hn_ref[...] = (h1 * _row(p10_ref, P_NORM2)).astype(_bf16)
    r2_ref[...] = jnp.broadcast_to(1.0 / (jnp.mean(h1 * h1, axis=-1, keepdims=True) + EPS), r2_ref.shape)
    for c in range(d_ff // FF_CHUNK):
        if c:
            yield
        cols = slice(c * FF_CHUNK, (c + 1) * FF_CHUNK)
        up = jnp.dot(hn_ref[...], wup_ref[:, cols], preferred_element_type=_f32)
        hid_ref[:, cols] = jnp.square(jnp.maximum(up.astype(_bf16), 0.0))


_PIPELINE_ORDER = "PUPUUUUP"
_PIPELINE_DEPTH = 2


def _pipelined_body(x_ref, xres_ref, hdr_ref, h0_ref, p5_ref, p10_ref, win_hbm, wmix_ref, bmix_ref, wgate_ref,
                    wout_hbm, wup_hbm, wdn_hbm, y_ref, stc_ref, sth_ref,
                    xn_ref, z_ref, xpad_ref, gu_ref, gg_ref, mixed_ref, xc_ref, pre_ref, hcar_ref, cat_ref,
                    h1_ref, hn_ref, hid_ref, r2_ref, wmixm_ref, win_ref, wout_ref, wup_ref, wdn_ref,
                    stage_ref, dma_sem, *, tm, chunk, n_tiles, tiles_per_seq, final_norm):
    s = pl.program_id(0)
    t_proj = jnp.minimum(s, n_tiles - 1) % tiles_per_seq
    t_gate = jnp.clip(s - 1, 0, n_tiles - 1) % tiles_per_seq

    @pl.when(s == 0)
    def _():
        for ref in (gu_ref, gg_ref, mixed_ref, xc_ref, pre_ref, h1_ref, hid_ref, r2_ref):
            ref[...] = jnp.zeros_like(ref)
        for src_hbm, dst_ref in ((win_hbm, win_ref), (wout_hbm, wout_ref), (wup_hbm, wup_ref), (wdn_hbm, wdn_ref)):
            _load_weight_bf16(src_hbm, dst_ref, stage_ref, dma_sem)

    @pl.when(t_proj == 0)
    def _():
        _init_proj(hdr_ref, wmix_ref, xpad_ref, wmixm_ref, chunk=chunk)

    @pl.when(t_gate == 0)
    def _():
        hcar_ref[...] = h0_ref[0]

    _gate_stage(p5_ref, bmix_ref, gu_ref, gg_ref, mixed_ref, xc_ref, pre_ref, hcar_ref, cat_ref,
                tm=tm, chunk=chunk, seg=tm, first_tile=(t_gate == 0), never=(s < 0))
    _down_stage(h1_ref, hid_ref, r2_ref, p10_ref, wdn_ref, y_ref, final_norm=final_norm)
    up = _up_phases(xres_ref, cat_ref, p10_ref, wout_ref, wup_ref, h1_ref, hn_ref, hid_ref, r2_ref,
                    anchor=(s < 0, y_ref))
    proj = _proj_phases(x_ref, p5_ref, p10_ref, win_ref, wgate_ref, None, wmixm_ref,
                        xn_ref, z_ref, xpad_ref, gu_ref, gg_ref, mixed_ref, xc_ref, pre_ref,
                        tm=tm, chunk=chunk, seg=tm)
    for who in _PIPELINE_ORDER:
        next(proj if who == "P" else up, None)
    assert next(proj, "done") == "done" and next(up, "done") == "done"

    @pl.when((t_proj == tiles_per_seq - 1) & (s < n_tiles))
    def _():
        stc_ref[0] = xpad_ref[:, 0:SUBLANES, :]

    @pl.when((t_gate == tiles_per_seq - 1) & (s >= 1) & (s <= n_tiles))
    def _():
        sth_ref[0] = hcar_ref[...]


def _single_body(x_ref, hdr_ref, h0_ref, p5_ref, p10_ref, win_ref, wmix_ref, bmix_ref, wgate_ref,
                 wout_ref, wup_ref, wdn_ref, y_ref, stc_ref, sth_ref, v_ref,
                 xn_ref, z_ref, xpad_ref, gu_ref, gg_ref, mixed_ref, xc_ref, pre_ref, hcar_ref, cat_ref,
                 h1_ref, hn_ref, hid_ref, r2_ref, wmixm_ref, *, tm, chunk, seg, final_norm):
    _init_proj(hdr_ref, wmix_ref, xpad_ref, wmixm_ref, chunk=chunk)
    hcar_ref[...] = h0_ref[0]
    for _ in _proj_phases(x_ref, p5_ref, p10_ref, win_ref, wgate_ref, v_ref, wmixm_ref,
                          xn_ref, z_ref, xpad_ref, gu_ref, gg_ref, mixed_ref, xc_ref, pre_ref,
                          tm=tm, chunk=chunk, seg=seg):
        pass
    _gate_stage(p5_ref, bmix_ref, gu_ref, gg_ref, mixed_ref, xc_ref, pre_ref, hcar_ref, cat_ref,
                tm=tm, chunk=chunk, seg=seg, first_tile=None)
    for _ in _up_phases(x_ref, cat_ref, p10_ref, wout_ref, wup_ref, h1_ref, hn_ref, hid_ref, r2_ref):
        pass
    _down_stage(h1_ref, hid_ref, r2_ref, p10_ref, wdn_ref, y_ref, final_norm=final_norm)
    stc_ref[0] = xpad_ref[:, 0:SUBLANES, :]
    sth_ref[0] = hcar_ref[...]


def _resident(shape):
    nd = len(shape)
    return pl.BlockSpec(shape, lambda i: (0,) * nd, pipeline_mode=pl.Buffered(1))


def _scratch(tm, nseg, seg, d_model, d_ff, w_a, w_b, wmix_shape):
    return [
        pltpu.VMEM((tm, d_model), _bf16),
        pltpu.VMEM((tm, 2 * w_a + 2 * w_b), _f32),
        pltpu.VMEM((nseg, seg + SUBLANES, w_b), _f32),
        pltpu.VMEM((tm, w_a), _f32),
        pltpu.VMEM((tm, w_b), _f32),
        pltpu.VMEM((tm, w_a), _f32),
        pltpu.VMEM((tm, w_b), _f32),
        pltpu.VMEM((tm, 2 * w_b), _f32),
        pltpu.VMEM((nseg, SUBLANES, w_b), _f32),
        pltpu.VMEM((tm, w_a + w_b), _bf16),
        pltpu.VMEM((tm, d_model), _f32),
        pltpu.VMEM((tm, d_model), _bf16),
        pltpu.VMEM((tm, d_ff), _bf16),
        pltpu.VMEM((tm, 128), _f32),
        pltpu.VMEM(wmix_shape, _bf16),
    ]


def _run_prompt_layer(x, hdr, h0, weights, *, tm, final_norm):
    p5, p10, win, wmix, bmix, wgate, wout, wup, wdn = weights
    nb, T, d_model = x.shape
    w_b = p5.shape[-1]
    w_a = bmix.shape[-1]
    tiles_per_seq = T // tm
    n_tiles = nb * tiles_per_seq
    x2 = x.reshape(nb * T, d_model)
    body = functools.partial(_pipelined_body, tm=tm, chunk=GMLP_CHUNK, n_tiles=n_tiles,
                             tiles_per_seq=tiles_per_seq, final_norm=final_norm)
    tile = lambda lag: (lambda s: (jnp.clip(s - lag, 0, n_tiles - 1), 0))
    seq = lambda lag: (lambda s: (jnp.clip(s - lag, 0, n_tiles - 1) // tiles_per_seq, 0, 0, 0))
    in_hbm = pl.BlockSpec(memory_space=pl.ANY)
    state_block = (1, 1, SUBLANES, w_b)
    in_specs = [
        pl.BlockSpec((tm, d_model), tile(0)),
        pl.BlockSpec((tm, d_model), tile(1)),
        pl.BlockSpec(state_block, seq(0)),
        pl.BlockSpec(state_block, seq(1)),
        _resident(p5.shape), _resident(p10.shape), in_hbm, _resident(wmix.shape), _resident(bmix.shape),
        _resident(wgate.shape), in_hbm, in_hbm, in_hbm,
    ]
    y, stc, sth = pl.pallas_call(
        body,
        grid=(n_tiles + _PIPELINE_DEPTH,),
        in_specs=in_specs,
        out_specs=[pl.BlockSpec((tm, d_model), tile(_PIPELINE_DEPTH)),
                   pl.BlockSpec(state_block, seq(0)), pl.BlockSpec(state_block, seq(1))],
        out_shape=[jax.ShapeDtypeStruct((nb * T, d_model), _f32),
                   jax.ShapeDtypeStruct((nb, 1, SUBLANES, w_b), _f32),
                   jax.ShapeDtypeStruct((nb, 1, SUBLANES, w_b), _f32)],
        scratch_shapes=_scratch(tm, 1, tm, d_model, wup.shape[-1], w_a, w_b, wmix.shape)
        + [pltpu.VMEM(w.shape, _bf16) for w in (win, wout, wup, wdn)]
        + [pltpu.VMEM((2, STAGE_ROWS, STAGE_COLS), _f32), pltpu.SemaphoreType.DMA((2,))],
        compiler_params=pltpu.CompilerParams(dimension_semantics=("arbitrary",),
                                             vmem_limit_bytes=VMEM_LIMIT_BYTES),
        name="prompt_layer",
    )(x2, x2, hdr, h0, *weights)
    return y.reshape(nb, T, d_model), stc, sth


def _run_sample_layer(x, hdr, h0, weights, *, chunk, final_norm):
    p5, p10, win, wmix, bmix, wgate, wout, wup, wdn = weights
    ns, ts, d_model = x.shape
    w_b = p5.shape[-1]
    w_a = bmix.shape[-1]
    tm = ns * ts
    body = functools.partial(_single_body, tm=tm, chunk=chunk, seg=ts, final_norm=final_norm)
    whole = lambda shape: pl.BlockSpec(shape, lambda i: (0,) * len(shape))
    state_shape = (1, ns, SUBLANES, w_b)
    y, stc, sth, v = pl.pallas_call(
        body,
        grid=(1,),
        in_specs=[whole((tm, d_model)), whole(hdr.shape), whole(h0.shape)] + [_resident(w.shape) for w in weights],
        out_specs=[whole((tm, d_model)), whole(state_shape), whole(state_shape), whole((tm, w_a))],
        out_shape=[jax.ShapeDtypeStruct((tm, d_model), _f32),
                   jax.ShapeDtypeStruct(state_shape, _f32),
                   jax.ShapeDtypeStruct(state_shape, _f32),
                   jax.ShapeDtypeStruct((tm, w_a), _f32)],
        scratch_shapes=_scratch(tm, ns, ts, d_model, wup.shape[-1], w_a, w_b, wmix.shape),
        compiler_params=pltpu.CompilerParams(dimension_semantics=("arbitrary",),
                                             vmem_limit_bytes=VMEM_LIMIT_BYTES),
        name="sample_layer",
    )(x.reshape(tm, d_model), hdr, h0, *weights)
    return y.reshape(ns, ts, d_model), stc, sth, v.reshape(ns, ts, w_a)


def _block_diag(w):
    n, k, _ = w.shape
    eye = jnp.eye(n, dtype=w.dtype)
    return (eye[:, None, :, None] * w[:, :, None, :]).reshape(n * k, n * k)


def _prep_weights(l, chunk, big_dtype, norm1_g, w_in, ln_v_g, ln_v_b, w_s, b_s, conv_w, conv_b, w_a, b_a, w_x, b_x,
                  lam, gn_a_g, gn_b_g, w_out, norm2_g, w_up, w_down, normf_g):
    w_b = conv_b.shape[-1]
    d_model = norm1_g.shape[-1]
    hd_a = gn_a_g.shape[-1] // H_A
    heads_per_tile = MXU_DIM // hd_a
    rows5 = [ln_v_g[l], ln_v_b[l], conv_w[l, 0], conv_w[l, 1], conv_w[l, 2], conv_w[l, 3], conv_b[l],
             b_a[l], b_x[l], lam[l], gn_a_g[l], gn_b_g[l]]
    p5 = jnp.stack(rows5 + [jnp.zeros((w_b,), _f32)] * (16 - len(rows5)))
    p10 = jnp.stack([norm1_g[l], norm2_g[l], normf_g] + [jnp.zeros((d_model,), _f32)] * 5)
    ws = w_s[l][:, :chunk, :chunk]
    wmix = ws.reshape(H_A // heads_per_tile, heads_per_tile, chunk, chunk).transpose(0, 2, 1, 3)
    wmix = wmix.reshape(H_A // heads_per_tile, chunk, heads_per_tile * chunk)
    bmix = jnp.repeat(b_s[l][:, :chunk].T, hd_a, axis=1)
    hh = H_B // 2
    wgate = jnp.stack([jnp.concatenate([_block_diag(w_a[l, j * hh:(j + 1) * hh]),
                                        _block_diag(w_x[l, j * hh:(j + 1) * hh])], axis=1)
                       for j in range(2)]).astype(_bf16)
    big = (w_in[l], w_out[l], w_up[l], w_down[l])
    if big_dtype is not None:
        big = tuple(w.astype(big_dtype) for w in big)
    return (p5, p10, big[0], wmix, bmix, wgate, big[1], big[2], big[3])


def _state_rows(conv_state, h_state):
    n, k, w = conv_state.shape
    hdr = jnp.concatenate([jnp.zeros((n, SUBLANES - k, w), conv_state.dtype), conv_state], axis=1)
    h0 = jnp.broadcast_to(h_state[:, None, :], (n, SUBLANES, w))
    return hdr, h0


def kernel(x_prompt, x_sample, state_conv_b, state_h_b, norm1_g, w_in, ln_v_g, ln_v_b, w_s, b_s, conv_w, conv_b,
           w_a, b_a, w_x, b_x, lam, gn_a_g, gn_b_g, w_out, norm2_g, w_up, w_down, normf_g):
    depth = w_in.shape[0]
    nb = x_prompt.shape[0]
    ns, ts, _ = x_sample.shape
    w_b = conv_b.shape[-1]
    params = (norm1_g, w_in, ln_v_g, ln_v_b, w_s, b_s, conv_w, conv_b, w_a, b_a, w_x, b_x, lam, gn_a_g, gn_b_g,
              w_out, norm2_g, w_up, w_down, normf_g)
    chunk_s = GMLP_CHUNK if ts % GMLP_CHUNK == 0 else ts
    tail = slice(SUBLANES - (CONV_W - 1), SUBLANES)
    hp, hs = x_prompt, x_sample
    conv_p, hlast_p, conv_s, hlast_s, v_s = [], [], [], [], []
    for l in range(depth):
        last = l == depth - 1
        hdr, h0 = _state_rows(jnp.zeros((nb, CONV_W - 1, w_b), _f32), jnp.zeros((nb, w_b), _f32))
        hp, stc, sth = _run_prompt_layer(hp, hdr[:, None], h0[:, None],
                                         _prep_weights(l, GMLP_CHUNK, None, *params),
                                         tm=PROMPT_TM, final_norm=last)
        conv_p.append(stc[:, 0, tail])
        hlast_p.append(sth[:, 0, SUBLANES - 1])
        hdr, h0 = _state_rows(state_conv_b[l], state_h_b[l])
        hs, stc, sth, vv = _run_sample_layer(hs, hdr[None], h0[None], _prep_weights(l, chunk_s, _bf16, *params),
                                             chunk=chunk_s, final_norm=last)
        conv_s.append(stc[0, :, tail])
        hlast_s.append(sth[0, :, SUBLANES - 1])
        v_s.append(vv)
    return (hp, hs, jnp.stack(conv_p), jnp.stack(hlast_p), jnp.stack(conv_s), jnp.stack(hlast_s), jnp.stack(v_s))
```

```python
import functools

import jax
import jax.numpy as jnp
from jax import lax
from jax.experimental import pallas as pl
from jax.experimental.pallas import tpu as pltpu

H_A = 8
H_B = 8
GMLP_CHUNK = 128
CONV_W = 4
LRU_C = 8.0
EPS = 1e-6

SUBLANES = 8
PACK_ROWS = 16
MXU_DIM = 256
FF_CHUNK = 1024
PROMPT_TM = 512
STAGE_ROWS, STAGE_COLS = 128, 1024
VMEM_LIMIT_BYTES = 60 * 1024 * 1024

P_LNV_G, P_LNV_B, P_CONV_W, P_CONV_B, P_BA, P_BX, P_LAM, P_GNA, P_GNB = 0, 1, 2, 6, 7, 8, 9, 10, 11
P_NORM1, P_NORM2, P_NORMF = 0, 1, 2

_f32 = jnp.float32
_bf16 = jnp.bfloat16


def _rms(x, g):
    return x * lax.rsqrt(jnp.mean(x * x, axis=-1, keepdims=True) + EPS) * g


def _layernorm(x, g, b):
    mu = jnp.mean(x, axis=-1, keepdims=True)
    xc = x - mu
    var = jnp.mean(xc * xc, axis=-1, keepdims=True)
    return xc * lax.rsqrt(var + EPS) * g + b


def _sigmoid(x):
    return 0.5 * (1.0 + jnp.tanh(0.5 * x))


def _row(ref, r):
    return ref[r:r + 1, :]


def _load_weight_bf16(src_hbm, dst_ref, stage_ref, sem):
    n_rows, n_cols = src_hbm.shape
    _, stage_rows, stage_cols = stage_ref.shape
    n_chunks = n_rows // stage_rows

    for c0 in range(0, n_cols, stage_cols):
        def chunk_copy(k, slot, c0=c0):
            r0 = pl.multiple_of(k * stage_rows, stage_rows)
            return pltpu.make_async_copy(src_hbm.at[pl.ds(r0, stage_rows), pl.ds(c0, stage_cols)],
                                         stage_ref.at[slot], sem.at[slot])

        chunk_copy(0, 0).start()

        def body(k, carry, c0=c0, chunk_copy=chunk_copy):
            slot = k % 2

            @pl.when(k + 1 < n_chunks)
            def _():
                chunk_copy(k + 1, 1 - slot).start()

            chunk_copy(k, slot).wait()
            r0 = pl.multiple_of(k * stage_rows, stage_rows)
            dst_ref[pl.ds(r0, stage_rows), c0:c0 + stage_cols] = stage_ref[slot].astype(_bf16)
            return carry

        lax.fori_loop(0, n_chunks, body, 0)


def _init_proj(hdr_ref, wmix_ref, xpad_ref, wmixm_ref, *, chunk):
    n_tiles_a, _, kcat = wmix_ref.shape
    row = lax.broadcasted_iota(jnp.int32, (chunk, kcat), 0)
    col = lax.broadcasted_iota(jnp.int32, (chunk, kcat), 1) % chunk
    for q in range(n_tiles_a):
        wmixm_ref[q] = jnp.where(col <= row, wmix_ref[q], 0.0).astype(_bf16)
    xpad_ref[:, 0:SUBLANES, :] = hdr_ref[0]


def _proj_phases(x_ref, p5_ref, p10_ref, win_ref, wgate_ref, v_ref, wmixm_ref,
                 xn_ref, z_ref, xpad_ref, gu_ref, gg_ref, mixed_ref, xc_ref, pre_ref, *, tm, chunk, seg):
    nseg = tm // seg
    nchunk = tm // chunk
    w_a = gu_ref.shape[-1]
    w_b = p5_ref.shape[-1]
    hd_a = w_a // H_A
    heads_per_tile = MXU_DIM // hd_a
    n_tiles_a = w_a // MXU_DIM

    x = x_ref[...]
    xn_ref[...] = (x * _row(p10_ref, P_NORM1)).astype(_bf16)
    r1 = lax.rsqrt(jnp.mean(x * x, axis=-1, keepdims=True) + EPS)
    d_mix = w_a + w_b
    z_ref[:, 0:d_mix] = jnp.dot(xn_ref[...], win_ref[:, 0:d_mix], preferred_element_type=_f32) * r1
    yield
    z_ref[:, d_mix:2 * d_mix] = jnp.dot(xn_ref[...], win_ref[:, d_mix:2 * d_mix], preferred_element_type=_f32) * r1
    yield

    gu_ref[...] = z_ref[:, 0:w_a]
    gg_ref[...] = z_ref[:, 2 * w_a + w_b:2 * w_a + 2 * w_b]

    v = _layernorm(jax.nn.gelu(z_ref[:, w_a:2 * w_a]), _row(p5_ref, P_LNV_G), _row(p5_ref, P_LNV_B))
    if v_ref is not None:
        v_ref[...] = v
    lane = lax.broadcasted_iota(jnp.int32, (chunk, MXU_DIM), 1)
    for c in range(nchunk):
        rows = slice(c * chunk, (c + 1) * chunk)
        for q in range(n_tiles_a):
            cols = slice(q * MXU_DIM, (q + 1) * MXU_DIM)
            vq = v[rows, cols]
            rhs = jnp.concatenate(
                [jnp.where((lane >= hd_a * j) & (lane < hd_a * (j + 1)), vq, 0.0)
                 for j in range(heads_per_tile)], axis=0).astype(_bf16)
            mixed_ref[rows, cols] = jnp.dot(wmixm_ref[q], rhs, preferred_element_type=_f32)

    for s in range(nseg):
        xpad_ref[s, SUBLANES:SUBLANES + seg, :] = z_ref[s * seg:(s + 1) * seg, 2 * w_a:2 * w_a + w_b]
    xcs = []
    for s in range(nseg):
        acc = _row(p5_ref, P_CONV_B)
        for k in range(CONV_W):
            off = SUBLANES - (CONV_W - 1) + k
            acc = acc + xpad_ref[s, off:off + seg, :] * _row(p5_ref, P_CONV_W + k)
        xcs.append(acc)
        xpad_ref[s, 0:SUBLANES, :] = xpad_ref[s, seg:seg + SUBLANES, :]
    xc = jnp.concatenate(xcs, axis=0) if nseg > 1 else xcs[0]
    xc_ref[...] = xc
    xcb = xc.astype(_bf16)
    half = w_b // 2
    for j in range(2):
        res = jnp.dot(xcb[:, j * half:(j + 1) * half], wgate_ref[j], preferred_element_type=_f32)
        pre_ref[:, j * half:(j + 1) * half] = res[:, :half]
        pre_ref[:, w_b + j * half:w_b + (j + 1) * half] = res[:, half:]


def _gate_stage(p5_ref, bmix_ref, gu_ref, gg_ref, mixed_ref, xc_ref, pre_ref, hcar_ref, cat_ref,
                *, tm, chunk, seg, first_tile, never=None):
    nseg = tm // seg
    w_a = gu_ref.shape[-1]
    w_b = p5_ref.shape[-1]

    rowid = lax.broadcasted_iota(jnp.int32, (SUBLANES, w_b), 0)
    for s in range(nseg):
        hp = hcar_ref[s]
        link = None
        for r0 in range(s * seg, (s + 1) * seg, PACK_ROWS):
            ybs = []
            for g0 in range(r0, r0 + PACK_ROWS, SUBLANES):
                rows = slice(g0, g0 + SUBLANES)
                pre_r = pre_ref[rows, 0:w_b]
                pre_i = pre_ref[rows, w_b:2 * w_b]
                if never is not None and link is not None:
                    pre_r = jnp.where(never, link[0], pre_r)
                    pre_i = jnp.where(never, link[1], pre_i)
                r = _sigmoid(pre_r + _row(p5_ref, P_BA))
                i = _sigmoid(pre_i + _row(p5_ref, P_BX))
                log_a = -LRU_C * r * jax.nn.softplus(-_row(p5_ref, P_LAM))
                ag = jnp.exp(log_a)
                th = jnp.tanh(log_a)
                n = -2.0 * th
                mult = jnp.where(n > 0.0, n * lax.rsqrt(n * (1.0 - th)), 0.0)
                if first_tile is not None and g0 == s * seg:
                    mult = jnp.where((rowid == 0) & first_tile, 1.0, mult)
                bg = mult * (i * xc_ref[rows, :])
                for d in (1, 2, 4):
                    keep = rowid >= d
                    a_sh = jnp.where(keep, pltpu.roll(ag, d, 0), 1.0)
                    b_sh = jnp.where(keep, pltpu.roll(bg, d, 0), 0.0)
                    bg = bg + ag * b_sh
                    ag = ag * a_sh
                hg = bg + ag * hp
                hp = jnp.broadcast_to(hg[SUBLANES - 1:SUBLANES, :], (SUBLANES, w_b))
                link = (hp, hp)
                ybs.append(hg * jax.nn.gelu(gg_ref[rows, :]))
            rows = slice(r0, r0 + PACK_ROWS)
            yb = jnp.concatenate(ybs, axis=0)
            cat_ref[rows, w_a:w_a + w_b] = _rms(yb, _row(p5_ref, P_GNB)).astype(_bf16)
            u = gu_ref[rows, :]
            if never is not None:
                u = jnp.where(never, jnp.concatenate([hp, hp], axis=0), u)
            ya = jax.nn.gelu(u) * (mixed_ref[rows, :] + bmix_ref[r0 % chunk:r0 % chunk + PACK_ROWS, :])
            link = (ya[0:SUBLANES], ya[SUBLANES:PACK_ROWS])
            cat_ref[rows, 0:w_a] = _rms(ya, _row(p5_ref, P_GNA)).astype(_bf16)
        hcar_ref[s] = hp


def _down_stage(h1_ref, hid_ref, r2_ref, p10_ref, wdn_ref, y_ref, *, final_norm):
    out = h1_ref[...] + jnp.dot(hid_ref[...], wdn_ref[...], preferred_element_type=_f32) * r2_ref[:, 0:1]
    y_ref[...] = _rms(out, _row(p10_ref, P_NORMF)) if final_norm else out


def _up_phases(x_ref, cat_ref, p10_ref, wout_ref, wup_ref, h1_ref, hn_ref, hid_ref, r2_ref, anchor=None):
    d_ff = wup_ref.shape[-1]
    x = x_ref[...]
    if anchor is not None:
        never, anchored_ref = anchor
        x = jnp.where(never, anchored_ref[...], x)
    h1_ref[...] = x + jnp.dot(cat_ref[...], wout_ref[...], preferred_element_type=_f32)
    yield
    h1 = h1_ref[...]
    hn_ref[...] = (h1 * _row(p10_ref, P_NORM2)).astype(_bf16)
    r2_ref[...] = jnp.broadcast_to(1.0 / (jnp.mean(h1 * h1, axis=-1, keepdims=True) + EPS), r2_ref.shape)
    for c in range(d_ff // FF_CHUNK):
        if c:
            yield
        cols = slice(c * FF_CHUNK, (c + 1) * FF_CHUNK)
        up = jnp.dot(hn_ref[...], wup_ref[:, cols], preferred_element_type=_f32)
        hid_ref[:, cols] = jnp.square(jnp.maximum(up.astype(_bf16), 0.0))


_PIPELINE_ORDER = "PUPUUUUP"
_PIPELINE_DEPTH = 2


def _pipelined_body(x_ref, xres_ref, hdr_ref, h0_ref, p5_ref, p10_ref, win_hbm, wmix_ref, bmix_ref, wgate_ref,
                    wout_hbm, wup_hbm, wdn_hbm, y_ref, stc_ref, sth_ref,
                    xn_ref, z_ref, xpad_ref, gu_ref, gg_ref, mixed_ref, xc_ref, pre_ref, hcar_ref, cat_ref,
                    h1_ref, hn_ref, hid_ref, r2_ref, wmixm_ref, win_ref, wout_ref, wup_ref, wdn_ref,
                    stage_ref, dma_sem, *, tm, chunk, n_tiles, tiles_per_seq, final_norm):
    s = pl.program_id(0)
    t_proj = jnp.minimum(s, n_tiles - 1) % tiles_per_seq
    t_gate = jnp.clip(s - 1, 0, n_tiles - 1) % tiles_per_seq

    @pl.when(s == 0)
    def _():
        for ref in (gu_ref, gg_ref, mixed_ref, xc_ref, pre_ref, h1_ref, hid_ref, r2_ref):
            ref[...] = jnp.zeros_like(ref)
        for src_hbm, dst_ref in ((win_hbm, win_ref), (wout_hbm, wout_ref), (wup_hbm, wup_ref), (wdn_hbm, wdn_ref)):
            _load_weight_bf16(src_hbm, dst_ref, stage_ref, dma_sem)

    @pl.when(t_proj == 0)
    def _():
        _init_proj(hdr_ref, wmix_ref, xpad_ref, wmixm_ref, chunk=chunk)

    @pl.when(t_gate == 0)
    def _():
        hcar_ref[...] = h0_ref[0]

    _gate_stage(p5_ref, bmix_ref, gu_ref, gg_ref, mixed_ref, xc_ref, pre_ref, hcar_ref, cat_ref,
                tm=tm, chunk=chunk, seg=tm, first_tile=(t_gate == 0), never=(s < 0))
    _down_stage(h1_ref, hid_ref, r2_ref, p10_ref, wdn_ref, y_ref, final_norm=final_norm)
    up = _up_phases(xres_ref, cat_ref, p10_ref, wout_ref, wup_ref, h1_ref, hn_ref, hid_ref, r2_ref,
                    anchor=(s < 0, y_ref))
    proj = _proj_phases(x_ref, p5_ref, p10_ref, win_ref, wgate_ref, None, wmixm_ref,
                        xn_ref, z_ref, xpad_ref, gu_ref, gg_ref, mixed_ref, xc_ref, pre_ref,
                        tm=tm, chunk=chunk, seg=tm)
    for who in _PIPELINE_ORDER:
        next(proj if who == "P" else up, None)
    assert next(proj, "done") == "done" and next(up, "done") == "done"

    @pl.when((t_proj == tiles_per_seq - 1) & (s < n_tiles))
    def _():
        stc_ref[0] = xpad_ref[:, 0:SUBLANES, :]

    @pl.when((t_gate == tiles_per_seq - 1) & (s >= 1) & (s <= n_tiles))
    def _():
        sth_ref[0] = hcar_ref[...]


def _single_body(x_ref, hdr_ref, h0_ref, p5_ref, p10_ref, win_ref, wmix_ref, bmix_ref, wgate_ref,
                 wout_ref, wup_ref, wdn_ref, y_ref, stc_ref, sth_ref, v_ref,
                 xn_ref, z_ref, xpad_ref, gu_ref, gg_ref, mixed_ref, xc_ref, pre_ref, hcar_ref, cat_ref,
                 h1_ref, hn_ref, hid_ref, r2_ref, wmixm_ref, *, tm, chunk, seg, final_norm):
    _init_proj(hdr_ref, wmix_ref, xpad_ref, wmixm_ref, chunk=chunk)
    hcar_ref[...] = h0_ref[0]
    for _ in _proj_phases(x_ref, p5_ref, p10_ref, win_ref, wgate_ref, v_ref, wmixm_ref,
                          xn_ref, z_ref, xpad_ref, gu_ref, gg_ref, mixed_ref, xc_ref, pre_ref,
                          tm=tm, chunk=chunk, seg=seg):
        pass
    _gate_stage(p5_ref, bmix_ref, gu_ref, gg_ref, mixed_ref, xc_ref, pre_ref, hcar_ref, cat_ref,
                tm=tm, chunk=chunk, seg=seg, first_tile=None)
    for _ in _up_phases(x_ref, cat_ref, p10_ref, wout_ref, wup_ref, h1_ref, hn_ref, hid_ref, r2_ref):
        pass
    _down_stage(h1_ref, hid_ref, r2_ref, p10_ref, wdn_ref, y_ref, final_norm=final_norm)
    stc_ref[0] = xpad_ref[:, 0:SUBLANES, :]
    sth_ref[0] = hcar_ref[...]


def _resident(shape):
    nd = len(shape)
    return pl.BlockSpec(shape, lambda i: (0,) * nd, pipeline_mode=pl.Buffered(1))


def _scratch(tm, nseg, seg, d_model, d_ff, w_a, w_b, wmix_shape):
    return [
        pltpu.VMEM((tm, d_model), _bf16),
        pltpu.VMEM((tm, 2 * w_a + 2 * w_b), _f32),
        pltpu.VMEM((nseg, seg + SUBLANES, w_b), _f32),
        pltpu.VMEM((tm, w_a), _f32),
        pltpu.VMEM((tm, w_b), _f32),
        pltpu.VMEM((tm, w_a), _f32),
        pltpu.VMEM((tm, w_b), _f32),
        pltpu.VMEM((tm, 2 * w_b), _f32),
        pltpu.VMEM((nseg, SUBLANES, w_b), _f32),
        pltpu.VMEM((tm, w_a + w_b), _bf16),
        pltpu.VMEM((tm, d_model), _f32),
        pltpu.VMEM((tm, d_model), _bf16),
        pltpu.VMEM((tm, d_ff), _bf16),
        pltpu.VMEM((tm, 128), _f32),
        pltpu.VMEM(wmix_shape, _bf16),
    ]


def _run_prompt_layer(x, hdr, h0, weights, *, tm, final_norm):
    p5, p10, win, wmix, bmix, wgate, wout, wup, wdn = weights
    nb, T, d_model = x.shape
    w_b = p5.shape[-1]
    w_a = bmix.shape[-1]
    tiles_per_seq = T // tm
    n_tiles = nb * tiles_per_seq
    x2 = x.reshape(nb * T, d_model)
    body = functools.partial(_pipelined_body, tm=tm, chunk=GMLP_CHUNK, n_tiles=n_tiles,
                             tiles_per_seq=tiles_per_seq, final_norm=final_norm)
    tile = lambda lag: (lambda s: (jnp.clip(s - lag, 0, n_tiles - 1), 0))
    seq = lambda lag: (lambda s: (jnp.clip(s - lag, 0, n_tiles - 1) // tiles_per_seq, 0, 0, 0))
    in_hbm = pl.BlockSpec(memory_space=pl.ANY)
    state_block = (1, 1, SUBLANES, w_b)
    in_specs = [
        pl.BlockSpec((tm, d_model), tile(0)),
        pl.BlockSpec((tm, d_model), tile(1)),
        pl.BlockSpec(state_block, seq(0)),
        pl.BlockSpec(state_block, seq(1)),
        _resident(p5.shape), _resident(p10.shape), in_hbm, _resident(wmix.shape), _resident(bmix.shape),
        _resident(wgate.shape), in_hbm, in_hbm, in_hbm,
    ]
    y, stc, sth = pl.pallas_call(
        body,
        grid=(n_tiles + _PIPELINE_DEPTH,),
        in_specs=in_specs,
        out_specs=[pl.BlockSpec((tm, d_model), tile(_PIPELINE_DEPTH)),
                   pl.BlockSpec(state_block, seq(0)), pl.BlockSpec(state_block, seq(1))],
        out_shape=[jax.ShapeDtypeStruct((nb * T, d_model), _f32),
                   jax.ShapeDtypeStruct((nb, 1, SUBLANES, w_b), _f32),
                   jax.ShapeDtypeStruct((nb, 1, SUBLANES, w_b), _f32)],
        scratch_shapes=_scratch(tm, 1, tm, d_model, wup.shape[-1], w_a, w_b, wmix.shape)
        + [pltpu.VMEM(w.shape, _bf16) for w in (win, wout, wup, wdn)]
        + [pltpu.VMEM((2, STAGE_ROWS, STAGE_COLS), _f32), pltpu.SemaphoreType.DMA((2,))],
        compiler_params=pltpu.CompilerParams(dimension_semantics=("arbitrary",),
                                             vmem_limit_bytes=VMEM_LIMIT_BYTES),
        name="prompt_layer",
    )(x2, x2, hdr, h0, *weights)
    return y.reshape(nb, T, d_model), stc, sth


def _run_sample_layer(x, hdr, h0, weights, *, chunk, final_norm):
    p5, p10, win, wmix, bmix, wgate, wout, wup, wdn = weights
    ns, ts, d_model = x.shape
    w_b = p5.shape[-1]
    w_a = bmix.shape[-1]
    tm = ns * ts
    body = functools.partial(_single_body, tm=tm, chunk=chunk, seg=ts, final_norm=final_norm)
    whole = lambda shape: pl.BlockSpec(shape, lambda i: (0,) * len(shape))
    state_shape = (1, ns, SUBLANES, w_b)
    y, stc, sth, v = pl.pallas_call(
        body,
        grid=(1,),
        in_specs=[whole((tm, d_model)), whole(hdr.shape), whole(h0.shape)] + [_resident(w.shape) for w in weights],
        out_specs=[whole((tm, d_model)), whole(state_shape), whole(state_shape), whole((tm, w_a))],
        out_shape=[jax.ShapeDtypeStruct((tm, d_model), _f32),
                   jax.ShapeDtypeStruct(state_shape, _f32),
                   jax.ShapeDtypeStruct(state_shape, _f32),
                   jax.ShapeDtypeStruct((tm, w_a), _f32)],
        scratch_shapes=_scratch(tm, ns, ts, d_model, wup.shape[-1], w_a, w_b, wmix.shape),
        compiler_params=pltpu.CompilerParams(dimension_semantics=("arbitrary",),
                                             vmem_limit_bytes=VMEM_LIMIT_BYTES),
        name="sample_layer",
    )(x.reshape(tm, d_model), hdr, h0, *weights)
    return y.reshape(ns, ts, d_model), stc, sth, v.reshape(ns, ts, w_a)


def _block_diag(w):
    n, k, _ = w.shape
    eye = jnp.eye(n, dtype=w.dtype)
    return (eye[:, None, :, None] * w[:, :, None, :]).reshape(n * k, n * k)


def _prep_weights(l, chunk, big_dtype, norm1_g, w_in, ln_v_g, ln_v_b, w_s, b_s, conv_w, conv_b, w_a, b_a, w_x, b_x,
                  lam, gn_a_g, gn_b_g, w_out, norm2_g, w_up, w_down, normf_g):
    w_b = conv_b.shape[-1]
    d_model = norm1_g.shape[-1]
    hd_a = gn_a_g.shape[-1] // H_A
    heads_per_tile = MXU_DIM // hd_a
    rows5 = [ln_v_g[l], ln_v_b[l], conv_w[l, 0], conv_w[l, 1], conv_w[l, 2], conv_w[l, 3], conv_b[l],
             b_a[l], b_x[l], lam[l], gn_a_g[l], gn_b_g[l]]
    p5 = jnp.stack(rows5 + [jnp.zeros((w_b,), _f32)] * (16 - len(rows5)))
    p10 = jnp.stack([norm1_g[l], norm2_g[l], normf_g] + [jnp.zeros((d_model,), _f32)] * 5)
    ws = w_s[l][:, :chunk, :chunk]
    wmix = ws.reshape(H_A // heads_per_tile, heads_per_tile, chunk, chunk).transpose(0, 2, 1, 3)
    wmix = wmix.reshape(H_A // heads_per_tile, chunk, heads_per_tile * chunk)
    bmix = jnp.repeat(b_s[l][:, :chunk].T, hd_a, axis=1)
    hh = H_B // 2
    wgate = jnp.stack([jnp.concatenate([_block_diag(w_a[l, j * hh:(j + 1) * hh]),
                                        _block_diag(w_x[l, j * hh:(j + 1) * hh])], axis=1)
                       for j in range(2)]).astype(_bf16)
    big = (w_in[l], w_out[l], w_up[l], w_down[l])
    if big_dtype is not None:
        big = tuple(w.astype(big_dtype) for w in big)
    return (p5, p10, big[0], wmix, bmix, wgate, big[1], big[2], big[3])


def _state_rows(conv_state, h_state):
    n, k, w = conv_state.shape
    hdr = jnp.concatenate([jnp.zeros((n, SUBLANES - k, w), conv_state.dtype), conv_state], axis=1)
    h0 = jnp.broadcast_to(h_state[:, None, :], (n, SUBLANES, w))
    return hdr, h0


def kernel(x_prompt, x_sample, state_conv_b, state_h_b, norm1_g, w_in, ln_v_g, ln_v_b, w_s, b_s, conv_w, conv_b,
           w_a, b_a, w_x, b_x, lam, gn_a_g, gn_b_g, w_out, norm2_g, w_up, w_down, normf_g):
    depth = w_in.shape[0]
    nb = x_prompt.shape[0]
    ns, ts, _ = x_sample.shape
    w_b = conv_b.shape[-1]
    params = (norm1_g, w_in, ln_v_g, ln_v_b, w_s, b_s, conv_w, conv_b, w_a, b_a, w_x, b_x, lam, gn_a_g, gn_b_g,
              w_out, norm2_g, w_up, w_down, normf_g)
    chunk_s = GMLP_CHUNK if ts % GMLP_CHUNK == 0 else ts
    tail = slice(SUBLANES - (CONV_W - 1), SUBLANES)
    hp, hs = x_prompt, x_sample
    conv_p, hlast_p, conv_s, hlast_s, v_s = [], [], [], [], []
    for l in range(depth):
        last = l == depth - 1
        hdr, h0 = _state_rows(jnp.zeros((nb, CONV_W - 1, w_b), _f32), jnp.zeros((nb, w_b), _f32))
        hp, stc, sth = _run_prompt_layer(hp, hdr[:, None], h0[:, None],
                                         _prep_weights(l, GMLP_CHUNK, None, *params),
                                         tm=PROMPT_TM, final_norm=last)
        conv_p.append(stc[:, 0, tail])
        hlast_p.append(sth[:, 0, SUBLANES - 1])
        hdr, h0 = _state_rows(state_conv_b[l], state_h_b[l])
        hs, stc, sth, vv = _run_sample_layer(hs, hdr[None], h0[None], _prep_weights(l, chunk_s, _bf16, *params),
                                             chunk=chunk_s, final_norm=last)
        conv_s.append(stc[0, :, tail])
        hlast_s.append(sth[0, :, SUBLANES - 1])
        v_s.append(vv)
    return (hp, hs, jnp.stack(conv_p), jnp.stack(hlast_p), jnp.stack(conv_s), jnp.stack(hlast_s), jnp.stack(v_s))
```

```python
import functools

import jax
import jax.numpy as jnp
from jax import lax
from jax.experimental import pallas as pl
from jax.experimental.pallas import tpu as pltpu

H_A = 8
H_B = 8
GMLP_CHUNK = 128
CONV_W = 4
LRU_C = 8.0
EPS = 1e-6

SUBLANES = 8
PACK_ROWS = 16
MXU_DIM = 256
FF_CHUNK = 1024
PROMPT_TM = 256
STAGE_ROWS, STAGE_COLS = 256, 1024
VMEM_LIMIT_BYTES = 56 * 1024 * 1024

P_LNV_G, P_LNV_B, P_CONV_W, P_CONV_B, P_BA, P_BX, P_LAM, P_GNA, P_GNB = 0, 1, 2, 6, 7, 8, 9, 10, 11
P_NORM1, P_NORM2, P_NORMF = 0, 1, 2

_f32 = jnp.float32
_bf16 = jnp.bfloat16


def _rms(x, g):
    return x * lax.rsqrt(jnp.mean(x * x, axis=-1, keepdims=True) + EPS) * g


def _layernorm(x, g, b):
    mu = jnp.mean(x, axis=-1, keepdims=True)
    xc = x - mu
    var = jnp.mean(xc * xc, axis=-1, keepdims=True)
    return xc * lax.rsqrt(var + EPS) * g + b


def _sigmoid(x):
    return 0.5 * (1.0 + jnp.tanh(0.5 * x))


def _row(ref, r):
    return ref[r:r + 1, :]


def _load_weight_bf16(src_hbm, dst_ref, stage_ref, sem):
    n_rows, n_cols = src_hbm.shape
    _, stage_rows, stage_cols = stage_ref.shape
    n_chunks = n_rows // stage_rows

    for c0 in range(0, n_cols, stage_cols):
        def chunk_copy(k, slot, c0=c0):
            r0 = pl.multiple_of(k * stage_rows, stage_rows)
            return pltpu.make_async_copy(src_hbm.at[pl.ds(r0, stage_rows), pl.ds(c0, stage_cols)],
                                         stage_ref.at[slot], sem.at[slot])

        chunk_copy(0, 0).start()

        def body(k, carry, c0=c0, chunk_copy=chunk_copy):
            slot = k % 2

            @pl.when(k + 1 < n_chunks)
            def _():
                chunk_copy(k + 1, 1 - slot).start()

            chunk_copy(k, slot).wait()
            r0 = pl.multiple_of(k * stage_rows, stage_rows)
            dst_ref[pl.ds(r0, stage_rows), c0:c0 + stage_cols] = stage_ref[slot].astype(_bf16)
            return carry

        lax.fori_loop(0, n_chunks, body, 0)


def _init_proj(hdr_ref, wmix_ref, xpad_ref, wmixm_ref, *, chunk):
    n_tiles_a, _, kcat = wmix_ref.shape
    row = lax.broadcasted_iota(jnp.int32, (chunk, kcat), 0)
    col = lax.broadcasted_iota(jnp.int32, (chunk, kcat), 1) % chunk
    for q in range(n_tiles_a):
        wmixm_ref[q] = jnp.where(col <= row, wmix_ref[q], 0.0).astype(_bf16)
    xpad_ref[:, 0:SUBLANES, :] = hdr_ref[0]


def _proj_phases(x_ref, p5_ref, p10_ref, win_ref, wgate_ref, v_ref, wmixm_ref,
                 xn_ref, z_ref, xpad_ref, gu_ref, gg_ref, mixed_ref, xc_ref, pre_ref, *, tm, chunk, seg):
    nseg = tm // seg
    nchunk = tm // chunk
    w_a = gu_ref.shape[-1]
    w_b = p5_ref.shape[-1]
    hd_a = w_a // H_A
    heads_per_tile = MXU_DIM // hd_a
    n_tiles_a = w_a // MXU_DIM

    x = x_ref[...]
    xn_ref[...] = (x * _row(p10_ref, P_NORM1)).astype(_bf16)
    r1 = lax.rsqrt(jnp.mean(x * x, axis=-1, keepdims=True) + EPS)
    d_mix = w_a + w_b
    z_ref[:, 0:d_mix] = jnp.dot(xn_ref[...], win_ref[:, 0:d_mix], preferred_element_type=_f32) * r1
    yield
    z_ref[:, d_mix:2 * d_mix] = jnp.dot(xn_ref[...], win_ref[:, d_mix:2 * d_mix], preferred_element_type=_f32) * r1
    yield

    gu_ref[...] = z_ref[:, 0:w_a]
    gg_ref[...] = z_ref[:, 2 * w_a + w_b:2 * w_a + 2 * w_b]

    v = _layernorm(jax.nn.gelu(z_ref[:, w_a:2 * w_a]), _row(p5_ref, P_LNV_G), _row(p5_ref, P_LNV_B))
    if v_ref is not None:
        v_ref[...] = v
    lane = lax.broadcasted_iota(jnp.int32, (chunk, MXU_DIM), 1)
    for c in range(nchunk):
        rows = slice(c * chunk, (c + 1) * chunk)
        for q in range(n_tiles_a):
            cols = slice(q * MXU_DIM, (q + 1) * MXU_DIM)
            vq = v[rows, cols]
            rhs = jnp.concatenate(
                [jnp.where((lane >= hd_a * j) & (lane < hd_a * (j + 1)), vq, 0.0)
                 for j in range(heads_per_tile)], axis=0).astype(_bf16)
            mixed_ref[rows, cols] = jnp.dot(wmixm_ref[q], rhs, preferred_element_type=_f32)

    for s in range(nseg):
        xpad_ref[s, SUBLANES:SUBLANES + seg, :] = z_ref[s * seg:(s + 1) * seg, 2 * w_a:2 * w_a + w_b]
    xcs = []
    for s in range(nseg):
        acc = _row(p5_ref, P_CONV_B)
        for k in range(CONV_W):
            off = SUBLANES - (CONV_W - 1) + k
            acc = acc + xpad_ref[s, off:off + seg, :] * _row(p5_ref, P_CONV_W + k)
        xcs.append(acc)
        xpad_ref[s, 0:SUBLANES, :] = xpad_ref[s, seg:seg + SUBLANES, :]
    xc = jnp.concatenate(xcs, axis=0) if nseg > 1 else xcs[0]
    xc_ref[...] = xc
    xcb = xc.astype(_bf16)
    half = w_b // 2
    for j in range(2):
        res = jnp.dot(xcb[:, j * half:(j + 1) * half], wgate_ref[j], preferred_element_type=_f32)
        pre_ref[:, j * half:(j + 1) * half] = res[:, :half]
        pre_ref[:, w_b + j * half:w_b + (j + 1) * half] = res[:, half:]


def _gate_stage(p5_ref, bmix_ref, gu_ref, gg_ref, mixed_ref, xc_ref, pre_ref, hcar_ref, cat_ref,
                *, tm, chunk, seg, first_tile, never=None):
    nseg = tm // seg
    w_a = gu_ref.shape[-1]
    w_b = p5_ref.shape[-1]

    rowid = lax.broadcasted_iota(jnp.int32, (SUBLANES, w_b), 0)
    for s in range(nseg):
        hp = hcar_ref[s]
        link = None
        for r0 in range(s * seg, (s + 1) * seg, PACK_ROWS):
            ybs = []
            for g0 in range(r0, r0 + PACK_ROWS, SUBLANES):
                rows = slice(g0, g0 + SUBLANES)
                pre_r = pre_ref[rows, 0:w_b]
                pre_i = pre_ref[rows, w_b:2 * w_b]
                if never is not None and link is not None:
                    pre_r = jnp.where(never, link[0], pre_r)
                    pre_i = jnp.where(never, link[1], pre_i)
                r = _sigmoid(pre_r + _row(p5_ref, P_BA))
                i = _sigmoid(pre_i + _row(p5_ref, P_BX))
                log_a = -LRU_C * r * jax.nn.softplus(-_row(p5_ref, P_LAM))
                ag = jnp.exp(log_a)
                th = jnp.tanh(log_a)
                n = -2.0 * th
                mult = jnp.where(n > 0.0, n * lax.rsqrt(n * (1.0 - th)), 0.0)
                if first_tile is not None and g0 == s * seg:
                    mult = jnp.where((rowid == 0) & first_tile, 1.0, mult)
                bg = mult * (i * xc_ref[rows, :])
                for d in (1, 2, 4):
                    keep = rowid >= d
                    a_sh = jnp.where(keep, pltpu.roll(ag, d, 0), 1.0)
                    b_sh = jnp.where(keep, pltpu.roll(bg, d, 0), 0.0)
                    bg = bg + ag * b_sh
                    ag = ag * a_sh
                hg = bg + ag * hp
                hp = jnp.broadcast_to(hg[SUBLANES - 1:SUBLANES, :], (SUBLANES, w_b))
                link = (hp, hp)
                ybs.append(hg * jax.nn.gelu(gg_ref[rows, :]))
            rows = slice(r0, r0 + PACK_ROWS)
            yb = jnp.concatenate(ybs, axis=0)
            cat_ref[rows, w_a:w_a + w_b] = _rms(yb, _row(p5_ref, P_GNB)).astype(_bf16)
            u = gu_ref[rows, :]
            if never is not None:
                u = jnp.where(never, jnp.concatenate([hp, hp], axis=0), u)
            ya = jax.nn.gelu(u) * (mixed_ref[rows, :] + bmix_ref[r0 % chunk:r0 % chunk + PACK_ROWS, :])
            link = (ya[0:SUBLANES], ya[SUBLANES:PACK_ROWS])
            cat_ref[rows, 0:w_a] = _rms(ya, _row(p5_ref, P_GNA)).astype(_bf16)
        hcar_ref[s] = hp


def _down_stage(h1_ref, hid_ref, r2_ref, p10_ref, wdn_ref, y_ref, *, final_norm):
    out = h1_ref[...] + jnp.dot(hid_ref[...], wdn_ref[...], preferred_element_type=_f32) * r2_ref[:, 0:1]
    y_ref[...] = _rms(out, _row(p10_ref, P_NORMF)) if final_norm else out


def _up_phases(x_ref, cat_ref, p10_ref, wout_ref, wup_ref, h1_ref, hn_ref, hid_ref, r2_ref, anchor=None):
    d_ff = wup_ref.shape[-1]
    x = x_ref[...]
    if anchor is not None:
        never, anchored_ref = anchor
        x = jnp.where(never, anchored_ref[...], x)
    h1_ref[...] = x + jnp.dot(cat_ref[...], wout_ref[...], preferred_element_type=_f32)
    yield
    h1 = h1_ref[...]
    hn_ref[...] = (h1 * _row(p10_ref, P_NORM2)).astype(_bf16)
    r2_ref[...] = jnp.broadcast_to(1.0 / (jnp.mean(h1 * h1, axis=-1, keepdims=True) + EPS), r2_ref.shape)
    for c in range(d_ff // FF_CHUNK):
        if c:
            yield
        cols = slice(c * FF_CHUNK, (c + 1) * FF_CHUNK)
        up = jnp.dot(hn_ref[...], wup_ref[:, cols], preferred_element_type=_f32)
        hid_ref[:, cols] = jnp.square(jnp.maximum(up.astype(_bf16), 0.0))


_PIPELINE_ORDER = "PUPUUUUP"
_PIPELINE_DEPTH = 2


def _pipelined_body(x_ref, xres_ref, hdr_ref, h0_ref, p5_ref, p10_ref, win_hbm, wmix_ref, bmix_ref, wgate_ref,
                    wout_hbm, wup_hbm, wdn_hbm, y_ref, stc_ref, sth_ref,
                    xn_ref, z_ref, xpad_ref, gu_ref, gg_ref, mixed_ref, xc_ref, pre_ref, hcar_ref, cat_ref,
                    h1_ref, hn_ref, hid_ref, r2_ref, wmixm_ref, win_ref, wout_ref, wup_ref, wdn_ref,
                    stage_ref, dma_sem, *, tm, chunk, n_tiles, tiles_per_seq, final_norm):
    s = pl.program_id(0)
    t_proj = jnp.minimum(s, n_tiles - 1) % tiles_per_seq
    t_gate = jnp.clip(s - 1, 0, n_tiles - 1) % tiles_per_seq

    @pl.when(s == 0)
    def _():
        for ref in (gu_ref, gg_ref, mixed_ref, xc_ref, pre_ref, h1_ref, hid_ref, r2_ref):
            ref[...] = jnp.zeros_like(ref)
        for src_hbm, dst_ref in ((win_hbm, win_ref), (wout_hbm, wout_ref), (wup_hbm, wup_ref), (wdn_hbm, wdn_ref)):
            _load_weight_bf16(src_hbm, dst_ref, stage_ref, dma_sem)

    @pl.when(t_proj == 0)
    def _():
        _init_proj(hdr_ref, wmix_ref, xpad_ref, wmixm_ref, chunk=chunk)

    @pl.when(t_gate == 0)
    def _():
        hcar_ref[...] = h0_ref[0]

    _gate_stage(p5_ref, bmix_ref, gu_ref, gg_ref, mixed_ref, xc_ref, pre_ref, hcar_ref, cat_ref,
                tm=tm, chunk=chunk, seg=tm, first_tile=(t_gate == 0), never=(s < 0))
    _down_stage(h1_ref, hid_ref, r2_ref, p10_ref, wdn_ref, y_ref, final_norm=final_norm)
    up = _up_phases(xres_ref, cat_ref, p10_ref, wout_ref, wup_ref, h1_ref, hn_ref, hid_ref, r2_ref,
                    anchor=(s < 0, y_ref))
    proj = _proj_phases(x_ref, p5_ref, p10_ref, win_ref, wgate_ref, None, wmixm_ref,
                        xn_ref, z_ref, xpad_ref, gu_ref, gg_ref, mixed_ref, xc_ref, pre_ref,
                        tm=tm, chunk=chunk, seg=tm)
    for who in _PIPELINE_ORDER:
        next(proj if who == "P" else up, None)
    assert next(proj, "done") == "done" and next(up, "done") == "done"

    @pl.when((t_proj == tiles_per_seq - 1) & (s < n_tiles))
    def _():
        stc_ref[0] = xpad_ref[:, 0:SUBLANES, :]

    @pl.when((t_gate == tiles_per_seq - 1) & (s >= 1) & (s <= n_tiles))
    def _():
        sth_ref[0] = hcar_ref[...]


def _single_body(x_ref, hdr_ref, h0_ref, p5_ref, p10_ref, win_hbm, wmix_ref, bmix_ref, wgate_ref,
                 wout_hbm, wup_hbm, wdn_hbm, y_ref, stc_ref, sth_ref, v_ref,
                 xn_ref, z_ref, xpad_ref, gu_ref, gg_ref, mixed_ref, xc_ref, pre_ref, hcar_ref, cat_ref,
                 h1_ref, hn_ref, hid_ref, r2_ref, wmixm_ref, win_ref, wout_ref, wup_ref, wdn_ref,
                 stage_ref, dma_sem, *, tm, chunk, seg, final_norm):
    for src_hbm, dst_ref in ((win_hbm, win_ref), (wout_hbm, wout_ref), (wup_hbm, wup_ref), (wdn_hbm, wdn_ref)):
        _load_weight_bf16(src_hbm, dst_ref, stage_ref, dma_sem)
    _init_proj(hdr_ref, wmix_ref, xpad_ref, wmixm_ref, chunk=chunk)
    hcar_ref[...] = h0_ref[0]
    for _ in _proj_phases(x_ref, p5_ref, p10_ref, win_ref, wgate_ref, v_ref, wmixm_ref,
                          xn_ref, z_ref, xpad_ref, gu_ref, gg_ref, mixed_ref, xc_ref, pre_ref,
                          tm=tm, chunk=chunk, seg=seg):
        pass
    _gate_stage(p5_ref, bmix_ref, gu_ref, gg_ref, mixed_ref, xc_ref, pre_ref, hcar_ref, cat_ref,
                tm=tm, chunk=chunk, seg=seg, first_tile=None)
    for _ in _up_phases(x_ref, cat_ref, p10_ref, wout_ref, wup_ref, h1_ref, hn_ref, hid_ref, r2_ref):
        pass
    _down_stage(h1_ref, hid_ref, r2_ref, p10_ref, wdn_ref, y_ref, final_norm=final_norm)
    stc_ref[0] = xpad_ref[:, 0:SUBLANES, :]
    sth_ref[0] = hcar_ref[...]


def _resident(shape):
    nd = len(shape)
    return pl.BlockSpec(shape, lambda i: (0,) * nd, pipeline_mode=pl.Buffered(1))


def _scratch(tm, nseg, seg, d_model, d_ff, w_a, w_b, wmix_shape):
    return [
        pltpu.VMEM((tm, d_model), _bf16),
        pltpu.VMEM((tm, 2 * w_a + 2 * w_b), _f32),
        pltpu.VMEM((nseg, seg + SUBLANES, w_b), _f32),
        pltpu.VMEM((tm, w_a), _f32),
        pltpu.VMEM((tm, w_b), _f32),
        pltpu.VMEM((tm, w_a), _f32),
        pltpu.VMEM((tm, w_b), _f32),
        pltpu.VMEM((tm, 2 * w_b), _f32),
        pltpu.VMEM((nseg, SUBLANES, w_b), _f32),
        pltpu.VMEM((tm, w_a + w_b), _bf16),
        pltpu.VMEM((tm, d_model), _f32),
        pltpu.VMEM((tm, d_model), _bf16),
        pltpu.VMEM((tm, d_ff), _bf16),
        pltpu.VMEM((tm, 128), _f32),
        pltpu.VMEM(wmix_shape, _bf16),
    ]


def _weight_scratch(*big_weights):
    return ([pltpu.VMEM(w.shape, _bf16) for w in big_weights]
            + [pltpu.VMEM((2, STAGE_ROWS, STAGE_COLS), _f32), pltpu.SemaphoreType.DMA((2,))])


def _run_prompt_layer(x, hdr, h0, weights, *, tm, final_norm):
    p5, p10, win, wmix, bmix, wgate, wout, wup, wdn = weights
    nb, T, d_model = x.shape
    w_b = p5.shape[-1]
    w_a = bmix.shape[-1]
    tiles_per_seq = T // tm
    n_tiles = nb * tiles_per_seq
    x2 = x.reshape(nb * T, d_model)
    body = functools.partial(_pipelined_body, tm=tm, chunk=GMLP_CHUNK, n_tiles=n_tiles,
                             tiles_per_seq=tiles_per_seq, final_norm=final_norm)
    tile = lambda lag: (lambda s: (jnp.clip(s - lag, 0, n_tiles - 1), 0))
    seq = lambda lag: (lambda s: (jnp.clip(s - lag, 0, n_tiles - 1) // tiles_per_seq, 0, 0, 0))
    in_hbm = pl.BlockSpec(memory_space=pl.ANY)
    state_block = (1, 1, SUBLANES, w_b)
    in_specs = [
        pl.BlockSpec((tm, d_model), tile(0)),
        pl.BlockSpec((tm, d_model), tile(1)),
        pl.BlockSpec(state_block, seq(0)),
        pl.BlockSpec(state_block, seq(1)),
        _resident(p5.shape), _resident(p10.shape), in_hbm, _resident(wmix.shape), _resident(bmix.shape),
        _resident(wgate.shape), in_hbm, in_hbm, in_hbm,
    ]
    y, stc, sth = pl.pallas_call(
        body,
        grid=(n_tiles + _PIPELINE_DEPTH,),
        in_specs=in_specs,
        out_specs=[pl.BlockSpec((tm, d_model), tile(_PIPELINE_DEPTH)),
                   pl.BlockSpec(state_block, seq(0)), pl.BlockSpec(state_block, seq(1))],
        out_shape=[jax.ShapeDtypeStruct((nb * T, d_model), _f32),
                   jax.ShapeDtypeStruct((nb, 1, SUBLANES, w_b), _f32),
                   jax.ShapeDtypeStruct((nb, 1, SUBLANES, w_b), _f32)],
        scratch_shapes=_scratch(tm, 1, tm, d_model, wup.shape[-1], w_a, w_b, wmix.shape)
        + _weight_scratch(win, wout, wup, wdn),
        compiler_params=pltpu.CompilerParams(dimension_semantics=("arbitrary",),
                                             vmem_limit_bytes=VMEM_LIMIT_BYTES),
        name="prompt_layer",
    )(x2, x2, hdr, h0, *weights)
    return y.reshape(nb, T, d_model), stc, sth


def _run_sample_layer(x, hdr, h0, weights, *, chunk, final_norm):
    p5, p10, win, wmix, bmix, wgate, wout, wup, wdn = weights
    ns, ts, d_model = x.shape
    w_b = p5.shape[-1]
    w_a = bmix.shape[-1]
    tm = ns * ts
    body = functools.partial(_single_body, tm=tm, chunk=chunk, seg=ts, final_norm=final_norm)
    whole = lambda shape: pl.BlockSpec(shape, lambda i: (0,) * len(shape))
    in_hbm = pl.BlockSpec(memory_space=pl.ANY)
    state_shape = (1, ns, SUBLANES, w_b)
    y, stc, sth, v = pl.pallas_call(
        body,
        grid=(1,),
        in_specs=[whole((tm, d_model)), whole(hdr.shape), whole(h0.shape),
                  _resident(p5.shape), _resident(p10.shape), in_hbm, _resident(wmix.shape), _resident(bmix.shape),
                  _resident(wgate.shape), in_hbm, in_hbm, in_hbm],
        out_specs=[whole((tm, d_model)), whole(state_shape), whole(state_shape), whole((tm, w_a))],
        out_shape=[jax.ShapeDtypeStruct((tm, d_model), _f32),
                   jax.ShapeDtypeStruct(state_shape, _f32),
                   jax.ShapeDtypeStruct(state_shape, _f32),
                   jax.ShapeDtypeStruct((tm, w_a), _f32)],
        scratch_shapes=_scratch(tm, ns, ts, d_model, wup.shape[-1], w_a, w_b, wmix.shape)
        + _weight_scratch(win, wout, wup, wdn),
        compiler_params=pltpu.CompilerParams(dimension_semantics=("arbitrary",),
                                             vmem_limit_bytes=VMEM_LIMIT_BYTES),
        name="sample_layer",
    )(x.reshape(tm, d_model), hdr, h0, *weights)
    return y.reshape(ns, ts, d_model), stc, sth, v.reshape(ns, ts, w_a)


def _block_diag(w):
    n, k, _ = w.shape
    eye = jnp.eye(n, dtype=w.dtype)
    return (eye[:, None, :, None] * w[:, :, None, :]).reshape(n * k, n * k)


def _prep_weights(l, chunk, norm1_g, w_in, ln_v_g, ln_v_b, w_s, b_s, conv_w, conv_b, w_a, b_a, w_x, b_x,
                  lam, gn_a_g, gn_b_g, w_out, norm2_g, w_up, w_down, normf_g):
    w_b = conv_b.shape[-1]
    d_model = norm1_g.shape[-1]
    hd_a = gn_a_g.shape[-1] // H_A
    heads_per_tile = MXU_DIM // hd_a
    rows5 = [ln_v_g[l], ln_v_b[l], conv_w[l, 0], conv_w[l, 1], conv_w[l, 2], conv_w[l, 3], conv_b[l],
             b_a[l], b_x[l], lam[l], gn_a_g[l], gn_b_g[l]]
    p5 = jnp.stack(rows5 + [jnp.zeros((w_b,), _f32)] * (16 - len(rows5)))
    p10 = jnp.stack([norm1_g[l], norm2_g[l], normf_g] + [jnp.zeros((d_model,), _f32)] * 5)
    ws = w_s[l][:, :chunk, :chunk]
    wmix = ws.reshape(H_A // heads_per_tile, heads_per_tile, chunk, chunk).transpose(0, 2, 1, 3)
    wmix = wmix.reshape(H_A // heads_per_tile, chunk, heads_per_tile * chunk)
    bmix = jnp.repeat(b_s[l][:, :chunk].T, hd_a, axis=1)
    hh = H_B // 2
    wgate = jnp.stack([jnp.concatenate([_block_diag(w_a[l, j * hh:(j + 1) * hh]),
                                        _block_diag(w_x[l, j * hh:(j + 1) * hh])], axis=1)
                       for j in range(2)]).astype(_bf16)
    return (p5, p10, w_in[l], wmix, bmix, wgate, w_out[l], w_up[l], w_down[l])


def _state_rows(conv_state, h_state):
    n, k, w = conv_state.shape
    hdr = jnp.concatenate([jnp.zeros((n, SUBLANES - k, w), conv_state.dtype), conv_state], axis=1)
    h0 = jnp.broadcast_to(h_state[:, None, :], (n, SUBLANES, w))
    return hdr, h0


def kernel(x_prompt, x_sample, state_conv_b, state_h_b, norm1_g, w_in, ln_v_g, ln_v_b, w_s, b_s, conv_w, conv_b,
           w_a, b_a, w_x, b_x, lam, gn_a_g, gn_b_g, w_out, norm2_g, w_up, w_down, normf_g):
    depth = w_in.shape[0]
    nb = x_prompt.shape[0]
    ns, ts, _ = x_sample.shape
    w_b = conv_b.shape[-1]
    params = (norm1_g, w_in, ln_v_g, ln_v_b, w_s, b_s, conv_w, conv_b, w_a, b_a, w_x, b_x, lam, gn_a_g, gn_b_g,
              w_out, norm2_g, w_up, w_down, normf_g)
    chunk_s = GMLP_CHUNK if ts % GMLP_CHUNK == 0 else ts
    tail = slice(SUBLANES - (CONV_W - 1), SUBLANES)
    hp, hs = x_prompt, x_sample
    conv_p, hlast_p, conv_s, hlast_s, v_s = [], [], [], [], []
    for l in range(depth):
        last = l == depth - 1
        hdr, h0 = _state_rows(jnp.zeros((nb, CONV_W - 1, w_b), _f32), jnp.zeros((nb, w_b), _f32))
        hp, stc, sth = _run_prompt_layer(hp, hdr[:, None], h0[:, None],
                                         _prep_weights(l, GMLP_CHUNK, *params),
                                         tm=PROMPT_TM, final_norm=last)
        conv_p.append(stc[:, 0, tail])
        hlast_p.append(sth[:, 0, SUBLANES - 1])
        hdr, h0 = _state_rows(state_conv_b[l], state_h_b[l])
        hs, stc, sth, vv = _run_sample_layer(hs, hdr[None], h0[None], _prep_weights(l, chunk_s, *params),
                                             chunk=chunk_s, final_norm=last)
        conv_s.append(stc[0, :, tail])
        hlast_s.append(sth[0, :, SUBLANES - 1])
        v_s.append(vv)
    return (hp, hs, jnp.stack(conv_p), jnp.stack(hlast_p), jnp.stack(conv_s), jnp.stack(hlast_s), jnp.stack(v_s))
```

```python
import functools

import jax
import jax.numpy as jnp
from jax import lax
from jax.experimental import pallas as pl
from jax.experimental.pallas import tpu as pltpu

H_A = 8
H_B = 8
GMLP_CHUNK = 128
CONV_W = 4
LRU_C = 8.0
EPS = 1e-6

SUBLANES = 8
PACK_ROWS = 16
MXU_DIM = 256
FF_CHUNK = 1024
PROMPT_TM = 256
STAGE_SLOTS, STAGE_ROWS, STAGE_COLS = 4, 256, 1024
VMEM_LIMIT_BYTES = 56 * 1024 * 1024

P_LNV_G, P_LNV_B, P_CONV_W, P_CONV_B, P_BA, P_BX, P_LAM, P_GNA, P_GNB = 0, 1, 2, 6, 7, 8, 9, 10, 11
P_NORM1, P_NORM2, P_NORMF = 0, 1, 2

_f32 = jnp.float32
_bf16 = jnp.bfloat16


def _rms(x, g):
    return x * lax.rsqrt(jnp.mean(x * x, axis=-1, keepdims=True) + EPS) * g


def _layernorm(x, g, b):
    mu = jnp.mean(x, axis=-1, keepdims=True)
    xc = x - mu
    var = jnp.mean(xc * xc, axis=-1, keepdims=True)
    return xc * lax.rsqrt(var + EPS) * g + b


def _sigmoid(x):
    return 0.5 * (1.0 + jnp.tanh(0.5 * x))


def _row(ref, r):
    return ref[r:r + 1, :]


def _load_weights_bf16(pairs, stage_ref, sem):
    n_slots, stage_rows, stage_cols = stage_ref.shape
    chunks = [(src, dst, r0, c0) for src, dst in pairs
              for c0 in range(0, src.shape[1], stage_cols) for r0 in range(0, src.shape[0], stage_rows)]

    def chunk_copy(k):
        src, _, r0, c0 = chunks[k]
        return pltpu.make_async_copy(src.at[r0:r0 + stage_rows, c0:c0 + stage_cols],
                                     stage_ref.at[k % n_slots], sem.at[k % n_slots])

    lookahead = n_slots - 1
    for k in range(min(lookahead, len(chunks))):
        chunk_copy(k).start()
    for k, (_, dst, r0, c0) in enumerate(chunks):
        if k + lookahead < len(chunks):
            chunk_copy(k + lookahead).start()
        chunk_copy(k).wait()
        dst[r0:r0 + stage_rows, c0:c0 + stage_cols] = stage_ref[k % n_slots].astype(_bf16)


def _init_proj(hdr_ref, wmix_ref, xpad_ref, wmixm_ref, *, chunk):
    n_tiles_a, _, kcat = wmix_ref.shape
    row = lax.broadcasted_iota(jnp.int32, (chunk, kcat), 0)
    col = lax.broadcasted_iota(jnp.int32, (chunk, kcat), 1) % chunk
    for q in range(n_tiles_a):
        wmixm_ref[q] = jnp.where(col <= row, wmix_ref[q], 0.0).astype(_bf16)
    xpad_ref[:, 0:SUBLANES, :] = hdr_ref[0]


def _proj_phases(x_ref, p5_ref, p10_ref, win_ref, wgate_ref, v_ref, wmixm_ref,
                 xn_ref, z_ref, xpad_ref, gu_ref, gg_ref, mixed_ref, xc_ref, pre_ref, *, tm, chunk, seg):
    nseg = tm // seg
    nchunk = tm // chunk
    w_a = gu_ref.shape[-1]
    w_b = p5_ref.shape[-1]
    hd_a = w_a // H_A
    heads_per_tile = MXU_DIM // hd_a
    n_tiles_a = w_a // MXU_DIM

    x = x_ref[...]
    xn_ref[...] = (x * _row(p10_ref, P_NORM1)).astype(_bf16)
    r1 = lax.rsqrt(jnp.mean(x * x, axis=-1, keepdims=True) + EPS)
    d_mix = w_a + w_b
    z_ref[:, 0:d_mix] = jnp.dot(xn_ref[...], win_ref[:, 0:d_mix], preferred_element_type=_f32) * r1
    yield
    z_ref[:, d_mix:2 * d_mix] = jnp.dot(xn_ref[...], win_ref[:, d_mix:2 * d_mix], preferred_element_type=_f32) * r1
    yield

    gu_ref[...] = z_ref[:, 0:w_a]
    gg_ref[...] = z_ref[:, 2 * w_a + w_b:2 * w_a + 2 * w_b]

    v = _layernorm(jax.nn.gelu(z_ref[:, w_a:2 * w_a]), _row(p5_ref, P_LNV_G), _row(p5_ref, P_LNV_B))
    if v_ref is not None:
        v_ref[...] = v
    lane = lax.broadcasted_iota(jnp.int32, (chunk, MXU_DIM), 1)
    for c in range(nchunk):
        rows = slice(c * chunk, (c + 1) * chunk)
        for q in range(n_tiles_a):
            cols = slice(q * MXU_DIM, (q + 1) * MXU_DIM)
            vq = v[rows, cols]
            rhs = jnp.concatenate(
                [jnp.where((lane >= hd_a * j) & (lane < hd_a * (j + 1)), vq, 0.0)
                 for j in range(heads_per_tile)], axis=0).astype(_bf16)
            mixed_ref[rows, cols] = jnp.dot(wmixm_ref[q], rhs, preferred_element_type=_f32)

    for s in range(nseg):
        xpad_ref[s, SUBLANES:SUBLANES + seg, :] = z_ref[s * seg:(s + 1) * seg, 2 * w_a:2 * w_a + w_b]
    xcs = []
    for s in range(nseg):
        acc = _row(p5_ref, P_CONV_B)
        for k in range(CONV_W):
            off = SUBLANES - (CONV_W - 1) + k
            acc = acc + xpad_ref[s, off:off + seg, :] * _row(p5_ref, P_CONV_W + k)
        xcs.append(acc)
        xpad_ref[s, 0:SUBLANES, :] = xpad_ref[s, seg:seg + SUBLANES, :]
    xc = jnp.concatenate(xcs, axis=0) if nseg > 1 else xcs[0]
    xc_ref[...] = xc
    xcb = xc.astype(_bf16)
    half = w_b // 2
    for j in range(2):
        res = jnp.dot(xcb[:, j * half:(j + 1) * half], wgate_ref[j], preferred_element_type=_f32)
        pre_ref[:, j * half:(j + 1) * half] = res[:, :half]
        pre_ref[:, w_b + j * half:w_b + (j + 1) * half] = res[:, half:]


def _gate_stage(p5_ref, bmix_ref, gu_ref, gg_ref, mixed_ref, xc_ref, pre_ref, hcar_ref, cat_ref,
                *, tm, chunk, seg, first_tile, never=None):
    nseg = tm // seg
    w_a = gu_ref.shape[-1]
    w_b = p5_ref.shape[-1]

    rowid = lax.broadcasted_iota(jnp.int32, (SUBLANES, w_b), 0)
    for s in range(nseg):
        hp = hcar_ref[s]
        link = None
        for r0 in range(s * seg, (s + 1) * seg, PACK_ROWS):
            ybs = []
            for g0 in range(r0, r0 + PACK_ROWS, SUBLANES):
                rows = slice(g0, g0 + SUBLANES)
                pre_r = pre_ref[rows, 0:w_b]
                pre_i = pre_ref[rows, w_b:2 * w_b]
                if never is not None and link is not None:
                    pre_r = jnp.where(never, link[0], pre_r)
                    pre_i = jnp.where(never, link[1], pre_i)
                r = _sigmoid(pre_r + _row(p5_ref, P_BA))
                i = _sigmoid(pre_i + _row(p5_ref, P_BX))
                log_a = -LRU_C * r * jax.nn.softplus(-_row(p5_ref, P_LAM))
                ag = jnp.exp(log_a)
                th = jnp.tanh(log_a)
                n = -2.0 * th
                mult = jnp.where(n > 0.0, n * lax.rsqrt(n * (1.0 - th)), 0.0)
                if first_tile is not None and g0 == s * seg:
                    mult = jnp.where((rowid == 0) & first_tile, 1.0, mult)
                bg = mult * (i * xc_ref[rows, :])
                for d in (1, 2, 4):
                    keep = rowid >= d
                    a_sh = jnp.where(keep, pltpu.roll(ag, d, 0), 1.0)
                    b_sh = jnp.where(keep, pltpu.roll(bg, d, 0), 0.0)
                    bg = bg + ag * b_sh
                    ag = ag * a_sh
                hg = bg + ag * hp
                hp = jnp.broadcast_to(hg[SUBLANES - 1:SUBLANES, :], (SUBLANES, w_b))
                link = (hp, hp)
                ybs.append(hg * jax.nn.gelu(gg_ref[rows, :]))
            rows = slice(r0, r0 + PACK_ROWS)
            yb = jnp.concatenate(ybs, axis=0)
            cat_ref[rows, w_a:w_a + w_b] = _rms(yb, _row(p5_ref, P_GNB)).astype(_bf16)
            u = gu_ref[rows, :]
            if never is not None:
                u = jnp.where(never, jnp.concatenate([hp, hp], axis=0), u)
            ya = jax.nn.gelu(u) * (mixed_ref[rows, :] + bmix_ref[r0 % chunk:r0 % chunk + PACK_ROWS, :])
            link = (ya[0:SUBLANES], ya[SUBLANES:PACK_ROWS])
            cat_ref[rows, 0:w_a] = _rms(ya, _row(p5_ref, P_GNA)).astype(_bf16)
        hcar_ref[s] = hp


def _down_stage(h1_ref, hid_ref, r2_ref, p10_ref, wdn_ref, y_ref, *, final_norm):
    out = h1_ref[...] + jnp.dot(hid_ref[...], wdn_ref[...], preferred_element_type=_f32) * r2_ref[:, 0:1]
    y_ref[...] = _rms(out, _row(p10_ref, P_NORMF)) if final_norm else out


def _up_phases(x_ref, cat_ref, p10_ref, wout_ref, wup_ref, h1_ref, hn_ref, hid_ref, r2_ref, anchor=None):
    d_ff = wup_ref.shape[-1]
    x = x_ref[...]
    if anchor is not None:
        never, anchored_ref = anchor
        x = jnp.where(never, anchored_ref[...], x)
    h1_ref[...] = x + jnp.dot(cat_ref[...], wout_ref[...], preferred_element_type=_f32)
    yield
    h1 = h1_ref[...]
    hn_ref[...] = (h1 * _row(p10_ref, P_NORM2)).astype(_bf16)
    r2_ref[...] = jnp.broadcast_to(1.0 / (jnp.mean(h1 * h1, axis=-1, keepdims=True) + EPS), r2_ref.shape)
    for c in range(d_ff // FF_CHUNK):
        if c:
            yield
        cols = slice(c * FF_CHUNK, (c + 1) * FF_CHUNK)
        up = jnp.dot(hn_ref[...], wup_ref[:, cols], preferred_element_type=_f32)
        hid_ref[:, cols] = jnp.square(jnp.maximum(up.astype(_bf16), 0.0))


_PIPELINE_ORDER = "PUPUUUUP"
_PIPELINE_DEPTH = 2


def _pipelined_body(x_ref, xres_ref, hdr_ref, h0_ref, p5_ref, p10_ref, win_hbm, wmix_ref, bmix_ref, wgate_ref,
                    wout_hbm, wup_hbm, wdn_hbm, y_ref, stc_ref, sth_ref,
                    xn_ref, z_ref, xpad_ref, gu_ref, gg_ref, mixed_ref, xc_ref, pre_ref, hcar_ref, cat_ref,
                    h1_ref, hn_ref, hid_ref, r2_ref, wmixm_ref, win_ref, wout_ref, wup_ref, wdn_ref,
                    stage_ref, dma_sem, *, tm, chunk, n_tiles, tiles_per_seq, final_norm):
    s = pl.program_id(0)
    t_proj = jnp.minimum(s, n_tiles - 1) % tiles_per_seq
    t_gate = jnp.clip(s - 1, 0, n_tiles - 1) % tiles_per_seq

    @pl.when(s == 0)
    def _():
        for ref in (gu_ref, gg_ref, mixed_ref, xc_ref, pre_ref, h1_ref, hid_ref, r2_ref):
            ref[...] = jnp.zeros_like(ref)
        _load_weights_bf16(((win_hbm, win_ref), (wout_hbm, wout_ref), (wup_hbm, wup_ref), (wdn_hbm, wdn_ref)),
                           stage_ref, dma_sem)

    @pl.when(t_proj == 0)
    def _():
        _init_proj(hdr_ref, wmix_ref, xpad_ref, wmixm_ref, chunk=chunk)

    @pl.when(t_gate == 0)
    def _():
        hcar_ref[...] = h0_ref[0]

    _gate_stage(p5_ref, bmix_ref, gu_ref, gg_ref, mixed_ref, xc_ref, pre_ref, hcar_ref, cat_ref,
                tm=tm, chunk=chunk, seg=tm, first_tile=(t_gate == 0), never=(s < 0))
    _down_stage(h1_ref, hid_ref, r2_ref, p10_ref, wdn_ref, y_ref, final_norm=final_norm)
    up = _up_phases(xres_ref, cat_ref, p10_ref, wout_ref, wup_ref, h1_ref, hn_ref, hid_ref, r2_ref,
                    anchor=(s < 0, y_ref))
    proj = _proj_phases(x_ref, p5_ref, p10_ref, win_ref, wgate_ref, None, wmixm_ref,
                        xn_ref, z_ref, xpad_ref, gu_ref, gg_ref, mixed_ref, xc_ref, pre_ref,
                        tm=tm, chunk=chunk, seg=tm)
    for who in _PIPELINE_ORDER:
        next(proj if who == "P" else up, None)
    assert next(proj, "done") == "done" and next(up, "done") == "done"

    @pl.when((t_proj == tiles_per_seq - 1) & (s < n_tiles))
    def _():
        stc_ref[0] = xpad_ref[:, 0:SUBLANES, :]

    @pl.when((t_gate == tiles_per_seq - 1) & (s >= 1) & (s <= n_tiles))
    def _():
        sth_ref[0] = hcar_ref[...]


def _single_body(x_ref, hdr_ref, h0_ref, p5_ref, p10_ref, win_hbm, wmix_ref, bmix_ref, wgate_ref,
                 wout_hbm, wup_hbm, wdn_hbm, y_ref, stc_ref, sth_ref, v_ref,
                 xn_ref, z_ref, xpad_ref, gu_ref, gg_ref, mixed_ref, xc_ref, pre_ref, hcar_ref, cat_ref,
                 h1_ref, hn_ref, hid_ref, r2_ref, wmixm_ref, win_ref, wout_ref, wup_ref, wdn_ref,
                 stage_ref, dma_sem, *, tm, chunk, seg, final_norm):
    _load_weights_bf16(((win_hbm, win_ref), (wout_hbm, wout_ref), (wup_hbm, wup_ref), (wdn_hbm, wdn_ref)),
                       stage_ref, dma_sem)
    _init_proj(hdr_ref, wmix_ref, xpad_ref, wmixm_ref, chunk=chunk)
    hcar_ref[...] = h0_ref[0]
    for _ in _proj_phases(x_ref, p5_ref, p10_ref, win_ref, wgate_ref, v_ref, wmixm_ref,
                          xn_ref, z_ref, xpad_ref, gu_ref, gg_ref, mixed_ref, xc_ref, pre_ref,
                          tm=tm, chunk=chunk, seg=seg):
        pass
    _gate_stage(p5_ref, bmix_ref, gu_ref, gg_ref, mixed_ref, xc_ref, pre_ref, hcar_ref, cat_ref,
                tm=tm, chunk=chunk, seg=seg, first_tile=None)
    for _ in _up_phases(x_ref, cat_ref, p10_ref, wout_ref, wup_ref, h1_ref, hn_ref, hid_ref, r2_ref):
        pass
    _down_stage(h1_ref, hid_ref, r2_ref, p10_ref, wdn_ref, y_ref, final_norm=final_norm)
    stc_ref[0] = xpad_ref[:, 0:SUBLANES, :]
    sth_ref[0] = hcar_ref[...]


def _resident(shape):
    nd = len(shape)
    return pl.BlockSpec(shape, lambda i: (0,) * nd, pipeline_mode=pl.Buffered(1))


def _scratch(tm, nseg, seg, d_model, d_ff, w_a, w_b, wmix_shape):
    return [
        pltpu.VMEM((tm, d_model), _bf16),
        pltpu.VMEM((tm, 2 * w_a + 2 * w_b), _f32),
        pltpu.VMEM((nseg, seg + SUBLANES, w_b), _f32),
        pltpu.VMEM((tm, w_a), _f32),
        pltpu.VMEM((tm, w_b), _f32),
        pltpu.VMEM((tm, w_a), _f32),
        pltpu.VMEM((tm, w_b), _f32),
        pltpu.VMEM((tm, 2 * w_b), _f32),
        pltpu.VMEM((nseg, SUBLANES, w_b), _f32),
        pltpu.VMEM((tm, w_a + w_b), _bf16),
        pltpu.VMEM((tm, d_model), _f32),
        pltpu.VMEM((tm, d_model), _bf16),
        pltpu.VMEM((tm, d_ff), _bf16),
        pltpu.VMEM((tm, 128), _f32),
        pltpu.VMEM(wmix_shape, _bf16),
    ]


def _weight_scratch(*big_weights):
    return ([pltpu.VMEM(w.shape, _bf16) for w in big_weights]
            + [pltpu.VMEM((STAGE_SLOTS, STAGE_ROWS, STAGE_COLS), _f32), pltpu.SemaphoreType.DMA((STAGE_SLOTS,))])


def _run_prompt_layer(x, hdr, h0, weights, *, tm, final_norm):
    p5, p10, win, wmix, bmix, wgate, wout, wup, wdn = weights
    nb, T, d_model = x.shape
    w_b = p5.shape[-1]
    w_a = bmix.shape[-1]
    tiles_per_seq = T // tm
    n_tiles = nb * tiles_per_seq
    x2 = x.reshape(nb * T, d_model)
    body = functools.partial(_pipelined_body, tm=tm, chunk=GMLP_CHUNK, n_tiles=n_tiles,
                             tiles_per_seq=tiles_per_seq, final_norm=final_norm)
    tile = lambda lag: (lambda s: (jnp.clip(s - lag, 0, n_tiles - 1), 0))
    seq = lambda lag: (lambda s: (jnp.clip(s - lag, 0, n_tiles - 1) // tiles_per_seq, 0, 0, 0))
    in_hbm = pl.BlockSpec(memory_space=pl.ANY)
    state_block = (1, 1, SUBLANES, w_b)
    in_specs = [
        pl.BlockSpec((tm, d_model), tile(0)),
        pl.BlockSpec((tm, d_model), tile(1)),
        pl.BlockSpec(state_block, seq(0)),
        pl.BlockSpec(state_block, seq(1)),
        _resident(p5.shape), _resident(p10.shape), in_hbm, _resident(wmix.shape), _resident(bmix.shape),
        _resident(wgate.shape), in_hbm, in_hbm, in_hbm,
    ]
    y, stc, sth = pl.pallas_call(
        body,
        grid=(n_tiles + _PIPELINE_DEPTH,),
        in_specs=in_specs,
        out_specs=[pl.BlockSpec((tm, d_model), tile(_PIPELINE_DEPTH)),
                   pl.BlockSpec(state_block, seq(0)), pl.BlockSpec(state_block, seq(1))],
        out_shape=[jax.ShapeDtypeStruct((nb * T, d_model), _f32),
                   jax.ShapeDtypeStruct((nb, 1, SUBLANES, w_b), _f32),
                   jax.ShapeDtypeStruct((nb, 1, SUBLANES, w_b), _f32)],
        scratch_shapes=_scratch(tm, 1, tm, d_model, wup.shape[-1], w_a, w_b, wmix.shape)
        + _weight_scratch(win, wout, wup, wdn),
        compiler_params=pltpu.CompilerParams(dimension_semantics=("arbitrary",),
                                             vmem_limit_bytes=VMEM_LIMIT_BYTES),
        name="prompt_layer",
    )(x2, x2, hdr, h0, *weights)
    return y.reshape(nb, T, d_model), stc, sth


def _run_sample_layer(x, hdr, h0, weights, *, chunk, final_norm):
    p5, p10, win, wmix, bmix, wgate, wout, wup, wdn = weights
    ns, ts, d_model = x.shape
    w_b = p5.shape[-1]
    w_a = bmix.shape[-1]
    tm = ns * ts
    body = functools.partial(_single_body, tm=tm, chunk=chunk, seg=ts, final_norm=final_norm)
    whole = lambda shape: pl.BlockSpec(shape, lambda i: (0,) * len(shape))
    in_hbm = pl.BlockSpec(memory_space=pl.ANY)
    state_shape = (1, ns, SUBLANES, w_b)
    y, stc, sth, v = pl.pallas_call(
        body,
        grid=(1,),
        in_specs=[whole((tm, d_model)), whole(hdr.shape), whole(h0.shape),
                  _resident(p5.shape), _resident(p10.shape), in_hbm, _resident(wmix.shape), _resident(bmix.shape),
                  _resident(wgate.shape), in_hbm, in_hbm, in_hbm],
        out_specs=[whole((tm, d_model)), whole(state_shape), whole(state_shape), whole((tm, w_a))],
        out_shape=[jax.ShapeDtypeStruct((tm, d_model), _f32),
                   jax.ShapeDtypeStruct(state_shape, _f32),
                   jax.ShapeDtypeStruct(state_shape, _f32),
                   jax.ShapeDtypeStruct((tm, w_a), _f32)],
        scratch_shapes=_scratch(tm, ns, ts, d_model, wup.shape[-1], w_a, w_b, wmix.shape)
        + _weight_scratch(win, wout, wup, wdn),
        compiler_params=pltpu.CompilerParams(dimension_semantics=("arbitrary",),
                                             vmem_limit_bytes=VMEM_LIMIT_BYTES),
        name="sample_layer",
    )(x.reshape(tm, d_model), hdr, h0, *weights)
    return y.reshape(ns, ts, d_model), stc, sth, v.reshape(ns, ts, w_a)


def _block_diag(w):
    n, k, _ = w.shape
    eye = jnp.eye(n, dtype=w.dtype)
    return (eye[:, None, :, None] * w[:, :, None, :]).reshape(n * k, n * k)


def _prep_weights(l, chunk, norm1_g, w_in, ln_v_g, ln_v_b, w_s, b_s, conv_w, conv_b, w_a, b_a, w_x, b_x,
                  lam, gn_a_g, gn_b_g, w_out, norm2_g, w_up, w_down, normf_g):
    w_b = conv_b.shape[-1]
    d_model = norm1_g.shape[-1]
    hd_a = gn_a_g.shape[-1] // H_A
    heads_per_tile = MXU_DIM // hd_a
    rows5 = [ln_v_g[l], ln_v_b[l], conv_w[l, 0], conv_w[l, 1], conv_w[l, 2], conv_w[l, 3], conv_b[l],
             b_a[l], b_x[l], lam[l], gn_a_g[l], gn_b_g[l]]
    p5 = jnp.stack(rows5 + [jnp.zeros((w_b,), _f32)] * (16 - len(rows5)))
    p10 = jnp.stack([norm1_g[l], norm2_g[l], normf_g] + [jnp.zeros((d_model,), _f32)] * 5)
    ws = w_s[l][:, :chunk, :chunk]
    wmix = ws.reshape(H_A // heads_per_tile, heads_per_tile, chunk, chunk).transpose(0, 2, 1, 3)
    wmix = wmix.reshape(H_A // heads_per_tile, chunk, heads_per_tile * chunk)
    bmix = jnp.repeat(b_s[l][:, :chunk].T, hd_a, axis=1)
    hh = H_B // 2
    wgate = jnp.stack([jnp.concatenate([_block_diag(w_a[l, j * hh:(j + 1) * hh]),
                                        _block_diag(w_x[l, j * hh:(j + 1) * hh])], axis=1)
                       for j in range(2)]).astype(_bf16)
    return (p5, p10, w_in[l], wmix, bmix, wgate, w_out[l], w_up[l], w_down[l])


def _state_rows(conv_state, h_state):
    n, k, w = conv_state.shape
    hdr = jnp.concatenate([jnp.zeros((n, SUBLANES - k, w), conv_state.dtype), conv_state], axis=1)
    h0 = jnp.broadcast_to(h_state[:, None, :], (n, SUBLANES, w))
    return hdr, h0


def kernel(x_prompt, x_sample, state_conv_b, state_h_b, norm1_g, w_in, ln_v_g, ln_v_b, w_s, b_s, conv_w, conv_b,
           w_a, b_a, w_x, b_x, lam, gn_a_g, gn_b_g, w_out, norm2_g, w_up, w_down, normf_g):
    depth = w_in.shape[0]
    nb = x_prompt.shape[0]
    ns, ts, _ = x_sample.shape
    w_b = conv_b.shape[-1]
    params = (norm1_g, w_in, ln_v_g, ln_v_b, w_s, b_s, conv_w, conv_b, w_a, b_a, w_x, b_x, lam, gn_a_g, gn_b_g,
              w_out, norm2_g, w_up, w_down, normf_g)
    chunk_s = GMLP_CHUNK if ts % GMLP_CHUNK == 0 else ts
    tail = slice(SUBLANES - (CONV_W - 1), SUBLANES)
    hp, hs = x_prompt, x_sample
    conv_p, hlast_p, conv_s, hlast_s, v_s = [], [], [], [], []
    for l in range(depth):
        last = l == depth - 1
        hdr, h0 = _state_rows(jnp.zeros((nb, CONV_W - 1, w_b), _f32), jnp.zeros((nb, w_b), _f32))
        hp, stc, sth = _run_prompt_layer(hp, hdr[:, None], h0[:, None],
                                         _prep_weights(l, GMLP_CHUNK, *params),
                                         tm=PROMPT_TM, final_norm=last)
        conv_p.append(stc[:, 0, tail])
        hlast_p.append(sth[:, 0, SUBLANES - 1])
        hdr, h0 = _state_rows(state_conv_b[l], state_h_b[l])
        hs, stc, sth, vv = _run_sample_layer(hs, hdr[None], h0[None], _prep_weights(l, chunk_s, *params),
                                             chunk=chunk_s, final_norm=last)
        conv_s.append(stc[0, :, tail])
        hlast_s.append(sth[0, :, SUBLANES - 1])
        v_s.append(vv)
    return (hp, hs, jnp.stack(conv_p), jnp.stack(hlast_p), jnp.stack(conv_s), jnp.stack(hlast_s), jnp.stack(v_s))
```

```python
import functools

import jax
import jax.numpy as jnp
from jax import lax
from jax.experimental import pallas as pl
from jax.experimental.pallas import tpu as pltpu

H_A = 8
H_B = 8
GMLP_CHUNK = 128
CONV_W = 4
LRU_C = 8.0
EPS = 1e-6

SUBLANES = 8
PACK_ROWS = 16
MXU_DIM = 256
FF_CHUNK = 1024
PROMPT_TM = 256
STAGE_SLOTS, STAGE_ROWS, STAGE_COLS = 4, 256, 1024
VMEM_LIMIT_BYTES = 56 * 1024 * 1024

P_LNV_G, P_LNV_B, P_CONV_W, P_CONV_B, P_BA, P_BX, P_LAM, P_GNA, P_GNB = 0, 1, 2, 6, 7, 8, 9, 10, 11
P_NORM1, P_NORM2, P_NORMF = 0, 1, 2

_f32 = jnp.float32
_bf16 = jnp.bfloat16


def _rms(x, g):
    return x * lax.rsqrt(jnp.mean(x * x, axis=-1, keepdims=True) + EPS) * g


def _layernorm(x, g, b):
    mu = jnp.mean(x, axis=-1, keepdims=True)
    xc = x - mu
    var = jnp.mean(xc * xc, axis=-1, keepdims=True)
    return xc * lax.rsqrt(var + EPS) * g + b


def _sigmoid(x):
    return 0.5 * (1.0 + jnp.tanh(0.5 * x))


def _row(ref, r):
    return ref[r:r + 1, :]


def _load_weights_bf16(pairs, stage_ref, sem):
    n_slots, stage_rows, stage_cols = stage_ref.shape
    chunks = [(src, dst, r0, c0) for src, dst in pairs
              for c0 in range(0, src.shape[1], stage_cols) for r0 in range(0, src.shape[0], stage_rows)]

    def chunk_copy(k):
        src, _, r0, c0 = chunks[k]
        return pltpu.make_async_copy(src.at[r0:r0 + stage_rows, c0:c0 + stage_cols],
                                     stage_ref.at[k % n_slots], sem.at[k % n_slots])

    lookahead = n_slots - 1
    for k in range(min(lookahead, len(chunks))):
        chunk_copy(k).start()
    for k, (_, dst, r0, c0) in enumerate(chunks):
        if k + lookahead < len(chunks):
            chunk_copy(k + lookahead).start()
        chunk_copy(k).wait()
        dst[r0:r0 + stage_rows, c0:c0 + stage_cols] = stage_ref[k % n_slots].astype(_bf16)


def _init_proj(hdr_ref, wmix_ref, xpad_ref, wmixm_ref, *, chunk):
    n_tiles_a, _, kcat = wmix_ref.shape
    row = lax.broadcasted_iota(jnp.int32, (chunk, kcat), 0)
    col = lax.broadcasted_iota(jnp.int32, (chunk, kcat), 1) % chunk
    for q in range(n_tiles_a):
        wmixm_ref[q] = jnp.where(col <= row, wmix_ref[q], 0.0).astype(_bf16)
    xpad_ref[:, 0:SUBLANES, :] = hdr_ref[0]


def _proj_phases(x_ref, p5_ref, p10_ref, win_ref, wgate_ref, v_ref, wmixm_ref,
                 xn_ref, zv_ref, xpad_ref, gu_ref, gg_ref, mixed_ref, xc_ref, pre_ref, *, tm, chunk, seg):
    nseg = tm // seg
    nchunk = tm // chunk
    w_a = gu_ref.shape[-1]
    w_b = p5_ref.shape[-1]
    hd_a = w_a // H_A
    heads_per_tile = MXU_DIM // hd_a
    n_tiles_a = w_a // MXU_DIM

    x = x_ref[...]
    xn_ref[...] = (x * _row(p10_ref, P_NORM1)).astype(_bf16)
    r1 = lax.rsqrt(jnp.mean(x * x, axis=-1, keepdims=True) + EPS)
    d_mix = w_a + w_b
    ua_va = jnp.dot(xn_ref[...], win_ref[:, 0:d_mix], preferred_element_type=_f32) * r1
    gu_ref[...] = jax.nn.gelu(ua_va[:, 0:w_a])
    zv_ref[...] = ua_va[:, w_a:d_mix]
    yield
    xb_gb = jnp.dot(xn_ref[...], win_ref[:, d_mix:2 * d_mix], preferred_element_type=_f32) * r1
    for s in range(nseg):
        xpad_ref[s, SUBLANES:SUBLANES + seg, :] = xb_gb[s * seg:(s + 1) * seg, 0:w_b]
    gg_ref[...] = jax.nn.gelu(xb_gb[:, w_b:d_mix])
    yield

    v = _layernorm(jax.nn.gelu(zv_ref[...]), _row(p5_ref, P_LNV_G), _row(p5_ref, P_LNV_B))
    if v_ref is not None:
        v_ref[...] = v
    lane = lax.broadcasted_iota(jnp.int32, (chunk, MXU_DIM), 1)
    for c in range(nchunk):
        rows = slice(c * chunk, (c + 1) * chunk)
        for q in range(n_tiles_a):
            cols = slice(q * MXU_DIM, (q + 1) * MXU_DIM)
            vq = v[rows, cols]
            rhs = jnp.concatenate(
                [jnp.where((lane >= hd_a * j) & (lane < hd_a * (j + 1)), vq, 0.0)
                 for j in range(heads_per_tile)], axis=0).astype(_bf16)
            mixed_ref[rows, cols] = jnp.dot(wmixm_ref[q], rhs, preferred_element_type=_f32)

    xcs = []
    for s in range(nseg):
        acc = _row(p5_ref, P_CONV_B)
        for k in range(CONV_W):
            off = SUBLANES - (CONV_W - 1) + k
            acc = acc + xpad_ref[s, off:off + seg, :] * _row(p5_ref, P_CONV_W + k)
        xcs.append(acc)
        xpad_ref[s, 0:SUBLANES, :] = xpad_ref[s, seg:seg + SUBLANES, :]
    xc = jnp.concatenate(xcs, axis=0) if nseg > 1 else xcs[0]
    xc_ref[...] = xc
    xcb = xc.astype(_bf16)
    half = w_b // 2
    for j in range(2):
        res = jnp.dot(xcb[:, j * half:(j + 1) * half], wgate_ref[j], preferred_element_type=_f32)
        pre_ref[:, j * half:(j + 1) * half] = res[:, :half]
        pre_ref[:, w_b + j * half:w_b + (j + 1) * half] = res[:, half:]


def _gate_stage(p5_ref, bmix_ref, gu_ref, gg_ref, mixed_ref, xc_ref, pre_ref, hcar_ref, cat_ref,
                *, tm, chunk, seg, first_tile, never=None):
    nseg = tm // seg
    w_a = gu_ref.shape[-1]
    w_b = p5_ref.shape[-1]

    rowid = lax.broadcasted_iota(jnp.int32, (SUBLANES, w_b), 0)
    for s in range(nseg):
        hp = hcar_ref[s]
        link = None
        for r0 in range(s * seg, (s + 1) * seg, PACK_ROWS):
            ybs = []
            for g0 in range(r0, r0 + PACK_ROWS, SUBLANES):
                rows = slice(g0, g0 + SUBLANES)
                pre_r = pre_ref[rows, 0:w_b]
                pre_i = pre_ref[rows, w_b:2 * w_b]
                if never is not None and link is not None:
                    pre_r = jnp.where(never, link[0], pre_r)
                    pre_i = jnp.where(never, link[1], pre_i)
                r = _sigmoid(pre_r + _row(p5_ref, P_BA))
                i = _sigmoid(pre_i + _row(p5_ref, P_BX))
                log_a = -LRU_C * r * jax.nn.softplus(-_row(p5_ref, P_LAM))
                ag = jnp.exp(log_a)
                th = jnp.tanh(log_a)
                n = -2.0 * th
                mult = jnp.where(n > 0.0, n * lax.rsqrt(n * (1.0 - th)), 0.0)
                if first_tile is not None and g0 == s * seg:
                    mult = jnp.where((rowid == 0) & first_tile, 1.0, mult)
                bg = mult * (i * xc_ref[rows, :])
                for d in (1, 2, 4):
                    keep = rowid >= d
                    a_sh = jnp.where(keep, pltpu.roll(ag, d, 0), 1.0)
                    b_sh = jnp.where(keep, pltpu.roll(bg, d, 0), 0.0)
                    bg = bg + ag * b_sh
                    ag = ag * a_sh
                hg = bg + ag * hp
                hp = jnp.broadcast_to(hg[SUBLANES - 1:SUBLANES, :], (SUBLANES, w_b))
                link = (hp, hp)
                ybs.append(hg * gg_ref[rows, :])
            rows = slice(r0, r0 + PACK_ROWS)
            yb = jnp.concatenate(ybs, axis=0)
            cat_ref[rows, w_a:w_a + w_b] = _rms(yb, _row(p5_ref, P_GNB)).astype(_bf16)
            u = gu_ref[rows, :]
            if never is not None:
                u = jnp.where(never, jnp.concatenate([hp, hp], axis=0), u)
            ya = u * (mixed_ref[rows, :] + bmix_ref[r0 % chunk:r0 % chunk + PACK_ROWS, :])
            link = (ya[0:SUBLANES], ya[SUBLANES:PACK_ROWS])
            cat_ref[rows, 0:w_a] = _rms(ya, _row(p5_ref, P_GNA)).astype(_bf16)
        hcar_ref[s] = hp


def _down_stage(h1_ref, hid_ref, r2_ref, p10_ref, wdn_ref, y_ref, *, final_norm):
    out = h1_ref[...] + jnp.dot(hid_ref[...], wdn_ref[...], preferred_element_type=_f32) * r2_ref[:, 0:1]
    y_ref[...] = _rms(out, _row(p10_ref, P_NORMF)) if final_norm else out


def _up_phases(x_ref, cat_ref, p10_ref, wout_ref, wup_ref, h1_ref, hn_ref, hid_ref, r2_ref, anchor=None):
    d_ff = wup_ref.shape[-1]
    x = x_ref[...]
    if anchor is not None:
        never, anchored_ref = anchor
        x = jnp.where(never, anchored_ref[...], x)
    h1_ref[...] = x + jnp.dot(cat_ref[...], wout_ref[...], preferred_element_type=_f32)
    yield
    h1 = h1_ref[...]
    hn_ref[...] = (h1 * _row(p10_ref, P_NORM2)).astype(_bf16)
    r2_ref[...] = jnp.broadcast_to(1.0 / (jnp.mean(h1 * h1, axis=-1, keepdims=True) + EPS), r2_ref.shape)
    for c in range(d_ff // FF_CHUNK):
        if c:
            yield
        cols = slice(c * FF_CHUNK, (c + 1) * FF_CHUNK)
        up = jnp.dot(hn_ref[...], wup_ref[:, cols], preferred_element_type=_f32)
        hid_ref[:, cols] = jnp.square(jnp.maximum(up.astype(_bf16), 0.0))


_PIPELINE_ORDER = "PUPUUUUP"
_PIPELINE_DEPTH = 2


def _pipelined_body(x_ref, xres_ref, hdr_ref, h0_ref, p5_ref, p10_ref, win_hbm, wmix_ref, bmix_ref, wgate_ref,
                    wout_hbm, wup_hbm, wdn_hbm, y_ref, stc_ref, sth_ref,
                    xn_ref, zv_ref, xpad_ref, gu_ref, gg_ref, mixed_ref, xc_ref, pre_ref, hcar_ref, cat_ref,
                    h1_ref, hn_ref, hid_ref, r2_ref, wmixm_ref, win_ref, wout_ref, wup_ref, wdn_ref,
                    stage_ref, dma_sem, *, tm, chunk, n_tiles, tiles_per_seq, final_norm):
    s = pl.program_id(0)
    t_proj = jnp.minimum(s, n_tiles - 1) % tiles_per_seq
    t_gate = jnp.clip(s - 1, 0, n_tiles - 1) % tiles_per_seq

    @pl.when(s == 0)
    def _():
        for ref in (gu_ref, gg_ref, mixed_ref, xc_ref, pre_ref, h1_ref, hid_ref, r2_ref):
            ref[...] = jnp.zeros_like(ref)
        _load_weights_bf16(((win_hbm, win_ref), (wout_hbm, wout_ref), (wup_hbm, wup_ref), (wdn_hbm, wdn_ref)),
                           stage_ref, dma_sem)

    @pl.when(t_proj == 0)
    def _():
        _init_proj(hdr_ref, wmix_ref, xpad_ref, wmixm_ref, chunk=chunk)

    @pl.when(t_gate == 0)
    def _():
        hcar_ref[...] = h0_ref[0]

    _gate_stage(p5_ref, bmix_ref, gu_ref, gg_ref, mixed_ref, xc_ref, pre_ref, hcar_ref, cat_ref,
                tm=tm, chunk=chunk, seg=tm, first_tile=(t_gate == 0), never=(s < 0))
    _down_stage(h1_ref, hid_ref, r2_ref, p10_ref, wdn_ref, y_ref, final_norm=final_norm)
    up = _up_phases(xres_ref, cat_ref, p10_ref, wout_ref, wup_ref, h1_ref, hn_ref, hid_ref, r2_ref,
                    anchor=(s < 0, y_ref))
    proj = _proj_phases(x_ref, p5_ref, p10_ref, win_ref, wgate_ref, None, wmixm_ref,
                        xn_ref, zv_ref, xpad_ref, gu_ref, gg_ref, mixed_ref, xc_ref, pre_ref,
                        tm=tm, chunk=chunk, seg=tm)
    for who in _PIPELINE_ORDER:
        next(proj if who == "P" else up, None)
    assert next(proj, "done") == "done" and next(up, "done") == "done"

    @pl.when((t_proj == tiles_per_seq - 1) & (s < n_tiles))
    def _():
        stc_ref[0] = xpad_ref[:, 0:SUBLANES, :]

    @pl.when((t_gate == tiles_per_seq - 1) & (s >= 1) & (s <= n_tiles))
    def _():
        sth_ref[0] = hcar_ref[...]


def _single_body(x_ref, hdr_ref, h0_ref, p5_ref, p10_ref, win_hbm, wmix_ref, bmix_ref, wgate_ref,
                 wout_hbm, wup_hbm, wdn_hbm, y_ref, stc_ref, sth_ref, v_ref,
                 xn_ref, zv_ref, xpad_ref, gu_ref, gg_ref, mixed_ref, xc_ref, pre_ref, hcar_ref, cat_ref,
                 h1_ref, hn_ref, hid_ref, r2_ref, wmixm_ref, win_ref, wout_ref, wup_ref, wdn_ref,
                 stage_ref, dma_sem, *, tm, chunk, seg, final_norm):
    _load_weights_bf16(((win_hbm, win_ref), (wout_hbm, wout_ref), (wup_hbm, wup_ref), (wdn_hbm, wdn_ref)),
                       stage_ref, dma_sem)
    _init_proj(hdr_ref, wmix_ref, xpad_ref, wmixm_ref, chunk=chunk)
    hcar_ref[...] = h0_ref[0]
    for _ in _proj_phases(x_ref, p5_ref, p10_ref, win_ref, wgate_ref, v_ref, wmixm_ref,
                          xn_ref, zv_ref, xpad_ref, gu_ref, gg_ref, mixed_ref, xc_ref, pre_ref,
                          tm=tm, chunk=chunk, seg=seg):
        pass
    _gate_stage(p5_ref, bmix_ref, gu_ref, gg_ref, mixed_ref, xc_ref, pre_ref, hcar_ref, cat_ref,
                tm=tm, chunk=chunk, seg=seg, first_tile=None)
    for _ in _up_phases(x_ref, cat_ref, p10_ref, wout_ref, wup_ref, h1_ref, hn_ref, hid_ref, r2_ref):
        pass
    _down_stage(h1_ref, hid_ref, r2_ref, p10_ref, wdn_ref, y_ref, final_norm=final_norm)
    stc_ref[0] = xpad_ref[:, 0:SUBLANES, :]
    sth_ref[0] = hcar_ref[...]


def _resident(shape):
    nd = len(shape)
    return pl.BlockSpec(shape, lambda i: (0,) * nd, pipeline_mode=pl.Buffered(1))


def _scratch(tm, nseg, seg, d_model, d_ff, w_a, w_b, wmix_shape):
    return [
        pltpu.VMEM((tm, d_model), _bf16),
        pltpu.VMEM((tm, w_a), _f32),
        pltpu.VMEM((nseg, seg + SUBLANES, w_b), _f32),
        pltpu.VMEM((tm, w_a), _f32),
        pltpu.VMEM((tm, w_b), _f32),
        pltpu.VMEM((tm, w_a), _f32),
        pltpu.VMEM((tm, w_b), _f32),
        pltpu.VMEM((tm, 2 * w_b), _f32),
        pltpu.VMEM((nseg, SUBLANES, w_b), _f32),
        pltpu.VMEM((tm, w_a + w_b), _bf16),
        pltpu.VMEM((tm, d_model), _f32),
        pltpu.VMEM((tm, d_model), _bf16),
        pltpu.VMEM((tm, d_ff), _bf16),
        pltpu.VMEM((tm, 128), _f32),
        pltpu.VMEM(wmix_shape, _bf16),
    ]


def _weight_scratch(*big_weights):
    return ([pltpu.VMEM(w.shape, _bf16) for w in big_weights]
            + [pltpu.VMEM((STAGE_SLOTS, STAGE_ROWS, STAGE_COLS), _f32), pltpu.SemaphoreType.DMA((STAGE_SLOTS,))])


def _run_prompt_layer(x, hdr, h0, weights, *, tm, final_norm):
    p5, p10, win, wmix, bmix, wgate, wout, wup, wdn = weights
    nb, T, d_model = x.shape
    w_b = p5.shape[-1]
    w_a = bmix.shape[-1]
    tiles_per_seq = T // tm
    n_tiles = nb * tiles_per_seq
    x2 = x.reshape(nb * T, d_model)
    body = functools.partial(_pipelined_body, tm=tm, chunk=GMLP_CHUNK, n_tiles=n_tiles,
                             tiles_per_seq=tiles_per_seq, final_norm=final_norm)
    tile = lambda lag: (lambda s: (jnp.clip(s - lag, 0, n_tiles - 1), 0))
    seq = lambda lag: (lambda s: (jnp.clip(s - lag, 0, n_tiles - 1) // tiles_per_seq, 0, 0, 0))
    in_hbm = pl.BlockSpec(memory_space=pl.ANY)
    state_block = (1, 1, SUBLANES, w_b)
    in_specs = [
        pl.BlockSpec((tm, d_model), tile(0)),
        pl.BlockSpec((tm, d_model), tile(1)),
        pl.BlockSpec(state_block, seq(0)),
        pl.BlockSpec(state_block, seq(1)),
        _resident(p5.shape), _resident(p10.shape), in_hbm, _resident(wmix.shape), _resident(bmix.shape),
        _resident(wgate.shape), in_hbm, in_hbm, in_hbm,
    ]
    y, stc, sth = pl.pallas_call(
        body,
        grid=(n_tiles + _PIPELINE_DEPTH,),
        in_specs=in_specs,
        out_specs=[pl.BlockSpec((tm, d_model), tile(_PIPELINE_DEPTH)),
                   pl.BlockSpec(state_block, seq(0)), pl.BlockSpec(state_block, seq(1))],
        out_shape=[jax.ShapeDtypeStruct((nb * T, d_model), _f32),
                   jax.ShapeDtypeStruct((nb, 1, SUBLANES, w_b), _f32),
                   jax.ShapeDtypeStruct((nb, 1, SUBLANES, w_b), _f32)],
        scratch_shapes=_scratch(tm, 1, tm, d_model, wup.shape[-1], w_a, w_b, wmix.shape)
        + _weight_scratch(win, wout, wup, wdn),
        compiler_params=pltpu.CompilerParams(dimension_semantics=("arbitrary",),
                                             vmem_limit_bytes=VMEM_LIMIT_BYTES),
        name="prompt_layer",
    )(x2, x2, hdr, h0, *weights)
    return y.reshape(nb, T, d_model), stc, sth


def _run_sample_layer(x, hdr, h0, weights, *, chunk, final_norm):
    p5, p10, win, wmix, bmix, wgate, wout, wup, wdn = weights
    ns, ts, d_model = x.shape
    w_b = p5.shape[-1]
    w_a = bmix.shape[-1]
    tm = ns * ts
    body = functools.partial(_single_body, tm=tm, chunk=chunk, seg=ts, final_norm=final_norm)
    whole = lambda shape: pl.BlockSpec(shape, lambda i: (0,) * len(shape))
    in_hbm = pl.BlockSpec(memory_space=pl.ANY)
    state_shape = (1, ns, SUBLANES, w_b)
    y, stc, sth, v = pl.pallas_call(
        body,
        grid=(1,),
        in_specs=[whole((tm, d_model)), whole(hdr.shape), whole(h0.shape),
                  _resident(p5.shape), _resident(p10.shape), in_hbm, _resident(wmix.shape), _resident(bmix.shape),
                  _resident(wgate.shape), in_hbm, in_hbm, in_hbm],
        out_specs=[whole((tm, d_model)), whole(state_shape), whole(state_shape), whole((tm, w_a))],
        out_shape=[jax.ShapeDtypeStruct((tm, d_model), _f32),
                   jax.ShapeDtypeStruct(state_shape, _f32),
                   jax.ShapeDtypeStruct(state_shape, _f32),
                   jax.ShapeDtypeStruct((tm, w_a), _f32)],
        scratch_shapes=_scratch(tm, ns, ts, d_model, wup.shape[-1], w_a, w_b, wmix.shape)
        + _weight_scratch(win, wout, wup, wdn),
        compiler_params=pltpu.CompilerParams(dimension_semantics=("arbitrary",),
                                             vmem_limit_bytes=VMEM_LIMIT_BYTES),
        name="sample_layer",
    )(x.reshape(tm, d_model), hdr, h0, *weights)
    return y.reshape(ns, ts, d_model), stc, sth, v.reshape(ns, ts, w_a)


def _block_diag(w):
    n, k, _ = w.shape
    eye = jnp.eye(n, dtype=w.dtype)
    return (eye[:, None, :, None] * w[:, :, None, :]).reshape(n * k, n * k)


def _prep_weights(l, chunk, norm1_g, w_in, ln_v_g, ln_v_b, w_s, b_s, conv_w, conv_b, w_a, b_a, w_x, b_x,
                  lam, gn_a_g, gn_b_g, w_out, norm2_g, w_up, w_down, normf_g):
    w_b = conv_b.shape[-1]
    d_model = norm1_g.shape[-1]
    hd_a = gn_a_g.shape[-1] // H_A
    heads_per_tile = MXU_DIM // hd_a
    rows5 = [ln_v_g[l], ln_v_b[l], conv_w[l, 0], conv_w[l, 1], conv_w[l, 2], conv_w[l, 3], conv_b[l],
             b_a[l], b_x[l], lam[l], gn_a_g[l], gn_b_g[l]]
    p5 = jnp.stack(rows5 + [jnp.zeros((w_b,), _f32)] * (16 - len(rows5)))
    p10 = jnp.stack([norm1_g[l], norm2_g[l], normf_g] + [jnp.zeros((d_model,), _f32)] * 5)
    ws = w_s[l][:, :chunk, :chunk]
    wmix = ws.reshape(H_A // heads_per_tile, heads_per_tile, chunk, chunk).transpose(0, 2, 1, 3)
    wmix = wmix.reshape(H_A // heads_per_tile, chunk, heads_per_tile * chunk)
    bmix = jnp.repeat(b_s[l][:, :chunk].T, hd_a, axis=1)
    hh = H_B // 2
    wgate = jnp.stack([jnp.concatenate([_block_diag(w_a[l, j * hh:(j + 1) * hh]),
                                        _block_diag(w_x[l, j * hh:(j + 1) * hh])], axis=1)
                       for j in range(2)]).astype(_bf16)
    return (p5, p10, w_in[l], wmix, bmix, wgate, w_out[l], w_up[l], w_down[l])


def _state_rows(conv_state, h_state):
    n, k, w = conv_state.shape
    hdr = jnp.concatenate([jnp.zeros((n, SUBLANES - k, w), conv_state.dtype), conv_state], axis=1)
    h0 = jnp.broadcast_to(h_state[:, None, :], (n, SUBLANES, w))
    return hdr, h0


def kernel(x_prompt, x_sample, state_conv_b, state_h_b, norm1_g, w_in, ln_v_g, ln_v_b, w_s, b_s, conv_w, conv_b,
           w_a, b_a, w_x, b_x, lam, gn_a_g, gn_b_g, w_out, norm2_g, w_up, w_down, normf_g):
    depth = w_in.shape[0]
    nb = x_prompt.shape[0]
    ns, ts, _ = x_sample.shape
    w_b = conv_b.shape[-1]
    params = (norm1_g, w_in, ln_v_g, ln_v_b, w_s, b_s, conv_w, conv_b, w_a, b_a, w_x, b_x, lam, gn_a_g, gn_b_g,
              w_out, norm2_g, w_up, w_down, normf_g)
    chunk_s = GMLP_CHUNK if ts % GMLP_CHUNK == 0 else ts
    tail = slice(SUBLANES - (CONV_W - 1), SUBLANES)
    hp, hs = x_prompt, x_sample
    conv_p, hlast_p, conv_s, hlast_s, v_s = [], [], [], [], []
    for l in range(depth):
        last = l == depth - 1
        hdr, h0 = _state_rows(jnp.zeros((nb, CONV_W - 1, w_b), _f32), jnp.zeros((nb, w_b), _f32))
        hp, stc, sth = _run_prompt_layer(hp, hdr[:, None], h0[:, None],
                                         _prep_weights(l, GMLP_CHUNK, *params),
                                         tm=PROMPT_TM, final_norm=last)
        conv_p.append(stc[:, 0, tail])
        hlast_p.append(sth[:, 0, SUBLANES - 1])
        hdr, h0 = _state_rows(state_conv_b[l], state_h_b[l])
        hs, stc, sth, vv = _run_sample_layer(hs, hdr[None], h0[None], _prep_weights(l, chunk_s, *params),
                                             chunk=chunk_s, final_norm=last)
        conv_s.append(stc[0, :, tail])
        hlast_s.append(sth[0, :, SUBLANES - 1])
        v_s.append(vv)
    return (hp, hs, jnp.stack(conv_p), jnp.stack(hlast_p), jnp.stack(conv_s), jnp.stack(hlast_s), jnp.stack(v_s))
```

```python
import functools

import jax
import jax.numpy as jnp
from jax import lax
from jax.experimental import pallas as pl
from jax.experimental.pallas import tpu as pltpu

H_A = 8
H_B = 8
GMLP_CHUNK = 128
CONV_W = 4
LRU_C = 8.0
EPS = 1e-6

SUBLANES = 8
PACK_ROWS = 16
MXU_DIM = 256
FF_CHUNK = 1024
LN_CHAINS = 2
PROMPT_TM = 256
STAGE_SLOTS, STAGE_ROWS, STAGE_COLS = 4, 256, 1024
VMEM_LIMIT_BYTES = 56 * 1024 * 1024

P_LNV_G, P_LNV_B, P_CONV_W, P_CONV_B, P_BA, P_BX, P_LAM, P_GNA, P_GNB = 0, 1, 2, 6, 7, 8, 9, 10, 11
P_NORM1, P_NORM2, P_NORMF = 0, 1, 2

_f32 = jnp.float32
_bf16 = jnp.bfloat16


def _rms(x, g):
    return x * lax.rsqrt(jnp.mean(x * x, axis=-1, keepdims=True) + EPS) * g


def _layernorm(x, g, b):
    mu = jnp.mean(x, axis=-1, keepdims=True)
    xc = x - mu
    var = jnp.mean(xc * xc, axis=-1, keepdims=True)
    return xc * lax.rsqrt(var + EPS) * g + b


def _sigmoid(x):
    return 0.5 * (1.0 + jnp.tanh(0.5 * x))


def _row(ref, r):
    return ref[r:r + 1, :]


def _load_weights_bf16(pairs, stage_ref, sem):
    n_slots, stage_rows, stage_cols = stage_ref.shape
    chunks = [(src, dst, r0, c0) for src, dst in pairs
              for c0 in range(0, src.shape[1], stage_cols) for r0 in range(0, src.shape[0], stage_rows)]

    def chunk_copy(k):
        src, _, r0, c0 = chunks[k]
        return pltpu.make_async_copy(src.at[r0:r0 + stage_rows, c0:c0 + stage_cols],
                                     stage_ref.at[k % n_slots], sem.at[k % n_slots])

    lookahead = n_slots - 1
    for k in range(min(lookahead, len(chunks))):
        chunk_copy(k).start()
    for k, (_, dst, r0, c0) in enumerate(chunks):
        if k + lookahead < len(chunks):
            chunk_copy(k + lookahead).start()
        chunk_copy(k).wait()
        dst[r0:r0 + stage_rows, c0:c0 + stage_cols] = stage_ref[k % n_slots].astype(_bf16)


def _init_proj(hdr_ref, wmix_ref, xpad_ref, wmixm_ref, *, chunk):
    n_tiles_a, _, kcat = wmix_ref.shape
    row = lax.broadcasted_iota(jnp.int32, (chunk, kcat), 0)
    col = lax.broadcasted_iota(jnp.int32, (chunk, kcat), 1) % chunk
    for q in range(n_tiles_a):
        wmixm_ref[q] = jnp.where(col <= row, wmix_ref[q], 0.0).astype(_bf16)
    xpad_ref[:, 0:SUBLANES, :] = hdr_ref[0]


def _proj_phases(x_ref, p5_ref, p10_ref, win_ref, wgate_ref, v_ref, wmixm_ref,
                 xn_ref, z_ref, xpad_ref, gu_ref, gg_ref, mixed_ref, xc_ref, pre_ref,
                 *, tm, chunk, seg, never=None):
    nseg = tm // seg
    nchunk = tm // chunk
    w_a = gu_ref.shape[-1]
    w_b = p5_ref.shape[-1]
    hd_a = w_a // H_A
    heads_per_tile = MXU_DIM // hd_a
    n_tiles_a = w_a // MXU_DIM

    x = x_ref[...]
    xn_ref[...] = (x * _row(p10_ref, P_NORM1)).astype(_bf16)
    r1 = lax.rsqrt(jnp.mean(x * x, axis=-1, keepdims=True) + EPS)
    d_mix = w_a + w_b
    z_ref[:, 0:d_mix] = jnp.dot(xn_ref[...], win_ref[:, 0:d_mix], preferred_element_type=_f32) * r1
    yield
    z_ref[:, d_mix:2 * d_mix] = jnp.dot(xn_ref[...], win_ref[:, d_mix:2 * d_mix], preferred_element_type=_f32) * r1
    yield

    gu_ref[...] = z_ref[:, 0:w_a]
    gg_ref[...] = z_ref[:, 2 * w_a + w_b:2 * w_a + 2 * w_b]

    v_blocks = []
    for r0 in range(0, tm, PACK_ROWS):
        zv = z_ref[r0:r0 + PACK_ROWS, w_a:2 * w_a]
        if never is not None and len(v_blocks) >= LN_CHAINS:
            zv = jnp.where(never, v_blocks[-LN_CHAINS], zv)
        v_blocks.append(_layernorm(jax.nn.gelu(zv), _row(p5_ref, P_LNV_G), _row(p5_ref, P_LNV_B)))
    v = jnp.concatenate(v_blocks, axis=0)
    if v_ref is not None:
        v_ref[...] = v
    lane = lax.broadcasted_iota(jnp.int32, (chunk, MXU_DIM), 1)
    for c in range(nchunk):
        rows = slice(c * chunk, (c + 1) * chunk)
        for q in range(n_tiles_a):
            cols = slice(q * MXU_DIM, (q + 1) * MXU_DIM)
            vq = v[rows, cols]
            rhs = jnp.concatenate(
                [jnp.where((lane >= hd_a * j) & (lane < hd_a * (j + 1)), vq, 0.0)
                 for j in range(heads_per_tile)], axis=0).astype(_bf16)
            mixed_ref[rows, cols] = jnp.dot(wmixm_ref[q], rhs, preferred_element_type=_f32)

    for s in range(nseg):
        xpad_ref[s, SUBLANES:SUBLANES + seg, :] = z_ref[s * seg:(s + 1) * seg, 2 * w_a:2 * w_a + w_b]
    xcs = []
    for s in range(nseg):
        acc = _row(p5_ref, P_CONV_B)
        for k in range(CONV_W):
            off = SUBLANES - (CONV_W - 1) + k
            acc = acc + xpad_ref[s, off:off + seg, :] * _row(p5_ref, P_CONV_W + k)
        xcs.append(acc)
        xpad_ref[s, 0:SUBLANES, :] = xpad_ref[s, seg:seg + SUBLANES, :]
    xc = jnp.concatenate(xcs, axis=0) if nseg > 1 else xcs[0]
    xc_ref[...] = xc
    xcb = xc.astype(_bf16)
    half = w_b // 2
    for j in range(2):
        res = jnp.dot(xcb[:, j * half:(j + 1) * half], wgate_ref[j], preferred_element_type=_f32)
        pre_ref[:, j * half:(j + 1) * half] = res[:, :half]
        pre_ref[:, w_b + j * half:w_b + (j + 1) * half] = res[:, half:]


def _gate_stage(p5_ref, bmix_ref, gu_ref, gg_ref, mixed_ref, xc_ref, pre_ref, hcar_ref, cat_ref,
                *, tm, chunk, seg, first_tile, never=None):
    nseg = tm // seg
    w_a = gu_ref.shape[-1]
    w_b = p5_ref.shape[-1]

    rowid = lax.broadcasted_iota(jnp.int32, (SUBLANES, w_b), 0)
    for s in range(nseg):
        hp = hcar_ref[s]
        link = None
        for r0 in range(s * seg, (s + 1) * seg, PACK_ROWS):
            ybs = []
            for g0 in range(r0, r0 + PACK_ROWS, SUBLANES):
                rows = slice(g0, g0 + SUBLANES)
                pre_r = pre_ref[rows, 0:w_b]
                pre_i = pre_ref[rows, w_b:2 * w_b]
                g = gg_ref[rows, :]
                if never is not None and link is not None:
                    pre_r = jnp.where(never, link[0], pre_r)
                    pre_i = jnp.where(never, link[1], pre_i)
                    g = jnp.where(never, link[0], g)
                r = _sigmoid(pre_r + _row(p5_ref, P_BA))
                i = _sigmoid(pre_i + _row(p5_ref, P_BX))
                log_a = -LRU_C * r * jax.nn.softplus(-_row(p5_ref, P_LAM))
                ag = jnp.exp(log_a)
                th = jnp.tanh(log_a)
                n = -2.0 * th
                mult = jnp.where(n > 0.0, n * lax.rsqrt(n * (1.0 - th)), 0.0)
                if first_tile is not None and g0 == s * seg:
                    mult = jnp.where((rowid == 0) & first_tile, 1.0, mult)
                bg = mult * (i * xc_ref[rows, :])
                for d in (1, 2, 4):
                    keep = rowid >= d
                    a_sh = jnp.where(keep, pltpu.roll(ag, d, 0), 1.0)
                    b_sh = jnp.where(keep, pltpu.roll(bg, d, 0), 0.0)
                    bg = bg + ag * b_sh
                    ag = ag * a_sh
                hg = bg + ag * hp
                hp = jnp.broadcast_to(hg[SUBLANES - 1:SUBLANES, :], (SUBLANES, w_b))
                link = (hp, hp)
                ybs.append(hg * jax.nn.gelu(g))
            rows = slice(r0, r0 + PACK_ROWS)
            yb = _rms(jnp.concatenate(ybs, axis=0), _row(p5_ref, P_GNB))
            cat_ref[rows, w_a:w_a + w_b] = yb.astype(_bf16)
            u = gu_ref[rows, :]
            if never is not None:
                u = jnp.where(never, yb, u)
            ya = jax.nn.gelu(u) * (mixed_ref[rows, :] + bmix_ref[r0 % chunk:r0 % chunk + PACK_ROWS, :])
            link = (ya[0:SUBLANES], ya[SUBLANES:PACK_ROWS])
            cat_ref[rows, 0:w_a] = _rms(ya, _row(p5_ref, P_GNA)).astype(_bf16)
        hcar_ref[s] = hp


def _down_stage(h1_ref, hid_ref, r2_ref, p10_ref, wdn_ref, y_ref, *, final_norm, never=None):
    down = jnp.dot(hid_ref[...], wdn_ref[...], preferred_element_type=_f32)
    tm = down.shape[0]
    prev = None
    for r0 in range(0, tm, PACK_ROWS):
        rows = slice(r0, r0 + PACK_ROWS)
        out = h1_ref[rows, :] + down[rows, :] * r2_ref[rows, 0:1]
        if never is not None and prev is not None:
            out = jnp.where(never, prev, out)
        prev = _rms(out, _row(p10_ref, P_NORMF)) if final_norm else out
        y_ref[rows, :] = prev


def _up_phases(x_ref, cat_ref, p10_ref, wout_ref, wup_ref, h1_ref, hn_ref, hid_ref, r2_ref, anchor=None):
    d_ff = wup_ref.shape[-1]
    x = x_ref[...]
    if anchor is not None:
        never, anchored_ref = anchor
        x = jnp.where(never, anchored_ref[...], x)
    h1_ref[...] = x + jnp.dot(cat_ref[...], wout_ref[...], preferred_element_type=_f32)
    yield
    h1 = h1_ref[...]
    hn_ref[...] = (h1 * _row(p10_ref, P_NORM2)).astype(_bf16)
    r2_ref[...] = jnp.broadcast_to(1.0 / (jnp.mean(h1 * h1, axis=-1, keepdims=True) + EPS), r2_ref.shape)
    for c in range(d_ff // FF_CHUNK):
        if c:
            yield
        cols = slice(c * FF_CHUNK, (c + 1) * FF_CHUNK)
        up = jnp.dot(hn_ref[...], wup_ref[:, cols], preferred_element_type=_f32)
        hid_ref[:, cols] = jnp.square(jnp.maximum(up.astype(_bf16), 0.0))


_PIPELINE_ORDER = "PPUUUUUP"
_PIPELINE_DEPTH = 2


def _pipelined_body(x_ref, xres_ref, hdr_ref, h0_ref, p5_ref, p10_ref, win_hbm, wmix_ref, bmix_ref, wgate_ref,
                    wout_hbm, wup_hbm, wdn_hbm, y_ref, stc_ref, sth_ref,
                    xn_ref, z_ref, xpad_ref, gu_ref, gg_ref, mixed_ref, xc_ref, pre_ref, hcar_ref, cat_ref,
                    h1_ref, hn_ref, hid_ref, r2_ref, wmixm_ref, win_ref, wout_ref, wup_ref, wdn_ref,
                    stage_ref, dma_sem, *, tm, chunk, n_tiles, tiles_per_seq, final_norm):
    s = pl.program_id(0)
    t_proj = jnp.minimum(s, n_tiles - 1) % tiles_per_seq
    t_gate = jnp.clip(s - 1, 0, n_tiles - 1) % tiles_per_seq

    @pl.when(s == 0)
    def _():
        for ref in (gu_ref, gg_ref, mixed_ref, xc_ref, pre_ref, h1_ref, hid_ref, r2_ref):
            ref[...] = jnp.zeros_like(ref)
        _load_weights_bf16(((win_hbm, win_ref), (wout_hbm, wout_ref), (wup_hbm, wup_ref), (wdn_hbm, wdn_ref)),
                           stage_ref, dma_sem)

    @pl.when(t_proj == 0)
    def _():
        _init_proj(hdr_ref, wmix_ref, xpad_ref, wmixm_ref, chunk=chunk)

    @pl.when(t_gate == 0)
    def _():
        hcar_ref[...] = h0_ref[0]

    _gate_stage(p5_ref, bmix_ref, gu_ref, gg_ref, mixed_ref, xc_ref, pre_ref, hcar_ref, cat_ref,
                tm=tm, chunk=chunk, seg=tm, first_tile=(t_gate == 0), never=(s < 0))
    _down_stage(h1_ref, hid_ref, r2_ref, p10_ref, wdn_ref, y_ref, final_norm=final_norm, never=(s < 0))
    up = _up_phases(xres_ref, cat_ref, p10_ref, wout_ref, wup_ref, h1_ref, hn_ref, hid_ref, r2_ref,
                    anchor=(s < 0, y_ref))
    proj = _proj_phases(x_ref, p5_ref, p10_ref, win_ref, wgate_ref, None, wmixm_ref,
                        xn_ref, z_ref, xpad_ref, gu_ref, gg_ref, mixed_ref, xc_ref, pre_ref,
                        tm=tm, chunk=chunk, seg=tm, never=(s < 0))
    for who in _PIPELINE_ORDER:
        next(proj if who == "P" else up, None)
    assert next(proj, "done") == "done" and next(up, "done") == "done"

    @pl.when((t_proj == tiles_per_seq - 1) & (s < n_tiles))
    def _():
        stc_ref[0] = xpad_ref[:, 0:SUBLANES, :]

    @pl.when((t_gate == tiles_per_seq - 1) & (s >= 1) & (s <= n_tiles))
    def _():
        sth_ref[0] = hcar_ref[...]


def _single_body(x_ref, hdr_ref, h0_ref, p5_ref, p10_ref, win_hbm, wmix_ref, bmix_ref, wgate_ref,
                 wout_hbm, wup_hbm, wdn_hbm, y_ref, stc_ref, sth_ref, v_ref,
                 xn_ref, z_ref, xpad_ref, gu_ref, gg_ref, mixed_ref, xc_ref, pre_ref, hcar_ref, cat_ref,
                 h1_ref, hn_ref, hid_ref, r2_ref, wmixm_ref, win_ref, wout_ref, wup_ref, wdn_ref,
                 stage_ref, dma_sem, *, tm, chunk, seg, final_norm):
    _load_weights_bf16(((win_hbm, win_ref), (wout_hbm, wout_ref), (wup_hbm, wup_ref), (wdn_hbm, wdn_ref)),
                       stage_ref, dma_sem)
    _init_proj(hdr_ref, wmix_ref, xpad_ref, wmixm_ref, chunk=chunk)
    hcar_ref[...] = h0_ref[0]
    for _ in _proj_phases(x_ref, p5_ref, p10_ref, win_ref, wgate_ref, v_ref, wmixm_ref,
                          xn_ref, z_ref, xpad_ref, gu_ref, gg_ref, mixed_ref, xc_ref, pre_ref,
                          tm=tm, chunk=chunk, seg=seg):
        pass
    _gate_stage(p5_ref, bmix_ref, gu_ref, gg_ref, mixed_ref, xc_ref, pre_ref, hcar_ref, cat_ref,
                tm=tm, chunk=chunk, seg=seg, first_tile=None)
    for _ in _up_phases(x_ref, cat_ref, p10_ref, wout_ref, wup_ref, h1_ref, hn_ref, hid_ref, r2_ref):
        pass
    _down_stage(h1_ref, hid_ref, r2_ref, p10_ref, wdn_ref, y_ref, final_norm=final_norm)
    stc_ref[0] = xpad_ref[:, 0:SUBLANES, :]
    sth_ref[0] = hcar_ref[...]


def _resident(shape):
    nd = len(shape)
    return pl.BlockSpec(shape, lambda i: (0,) * nd, pipeline_mode=pl.Buffered(1))


def _scratch(tm, nseg, seg, d_model, d_ff, w_a, w_b, wmix_shape):
    return [
        pltpu.VMEM((tm, d_model), _bf16),
        pltpu.VMEM((tm, 2 * w_a + 2 * w_b), _f32),
        pltpu.VMEM((nseg, seg + SUBLANES, w_b), _f32),
        pltpu.VMEM((tm, w_a), _f32),
        pltpu.VMEM((tm, w_b), _f32),
        pltpu.VMEM((tm, w_a), _f32),
        pltpu.VMEM((tm, w_b), _f32),
        pltpu.VMEM((tm, 2 * w_b), _f32),
        pltpu.VMEM((nseg, SUBLANES, w_b), _f32),
        pltpu.VMEM((tm, w_a + w_b), _bf16),
        pltpu.VMEM((tm, d_model), _f32),
        pltpu.VMEM((tm, d_model), _bf16),
        pltpu.VMEM((tm, d_ff), _bf16),
        pltpu.VMEM((tm, 128), _f32),
        pltpu.VMEM(wmix_shape, _bf16),
    ]


def _weight_scratch(*big_weights):
    return ([pltpu.VMEM(w.shape, _bf16) for w in big_weights]
            + [pltpu.VMEM((STAGE_SLOTS, STAGE_ROWS, STAGE_COLS), _f32), pltpu.SemaphoreType.DMA((STAGE_SLOTS,))])


def _run_prompt_layer(x, hdr, h0, weights, *, tm, final_norm):
    p5, p10, win, wmix, bmix, wgate, wout, wup, wdn = weights
    nb, T, d_model = x.shape
    w_b = p5.shape[-1]
    w_a = bmix.shape[-1]
    tiles_per_seq = T // tm
    n_tiles = nb * tiles_per_seq
    x2 = x.reshape(nb * T, d_model)
    body = functools.partial(_pipelined_body, tm=tm, chunk=GMLP_CHUNK, n_tiles=n_tiles,
                             tiles_per_seq=tiles_per_seq, final_norm=final_norm)
    tile = lambda lag: (lambda s: (jnp.clip(s - lag, 0, n_tiles - 1), 0))
    seq = lambda lag: (lambda s: (jnp.clip(s - lag, 0, n_tiles - 1) // tiles_per_seq, 0, 0, 0))
    in_hbm = pl.BlockSpec(memory_space=pl.ANY)
    state_block = (1, 1, SUBLANES, w_b)
    in_specs = [
        pl.BlockSpec((tm, d_model), tile(0)),
        pl.BlockSpec((tm, d_model), tile(1)),
        pl.BlockSpec(state_block, seq(0)),
        pl.BlockSpec(state_block, seq(1)),
        _resident(p5.shape), _resident(p10.shape), in_hbm, _resident(wmix.shape), _resident(bmix.shape),
        _resident(wgate.shape), in_hbm, in_hbm, in_hbm,
    ]
    y, stc, sth = pl.pallas_call(
        body,
        grid=(n_tiles + _PIPELINE_DEPTH,),
        in_specs=in_specs,
        out_specs=[pl.BlockSpec((tm, d_model), tile(_PIPELINE_DEPTH)),
                   pl.BlockSpec(state_block, seq(0)), pl.BlockSpec(state_block, seq(1))],
        out_shape=[jax.ShapeDtypeStruct((nb * T, d_model), _f32),
                   jax.ShapeDtypeStruct((nb, 1, SUBLANES, w_b), _f32),
                   jax.ShapeDtypeStruct((nb, 1, SUBLANES, w_b), _f32)],
        scratch_shapes=_scratch(tm, 1, tm, d_model, wup.shape[-1], w_a, w_b, wmix.shape)
        + _weight_scratch(win, wout, wup, wdn),
        compiler_params=pltpu.CompilerParams(dimension_semantics=("arbitrary",),
                                             vmem_limit_bytes=VMEM_LIMIT_BYTES),
        name="prompt_layer",
    )(x2, x2, hdr, h0, *weights)
    return y.reshape(nb, T, d_model), stc, sth


def _run_sample_layer(x, hdr, h0, weights, *, chunk, final_norm):
    p5, p10, win, wmix, bmix, wgate, wout, wup, wdn = weights
    ns, ts, d_model = x.shape
    w_b = p5.shape[-1]
    w_a = bmix.shape[-1]
    tm = ns * ts
    body = functools.partial(_single_body, tm=tm, chunk=chunk, seg=ts, final_norm=final_norm)
    whole = lambda shape: pl.BlockSpec(shape, lambda i: (0,) * len(shape))
    in_hbm = pl.BlockSpec(memory_space=pl.ANY)
    state_shape = (1, ns, SUBLANES, w_b)
    y, stc, sth, v = pl.pallas_call(
        body,
        grid=(1,),
        in_specs=[whole((tm, d_model)), whole(hdr.shape), whole(h0.shape),
                  _resident(p5.shape), _resident(p10.shape), in_hbm, _resident(wmix.shape), _resident(bmix.shape),
                  _resident(wgate.shape), in_hbm, in_hbm, in_hbm],
        out_specs=[whole((tm, d_model)), whole(state_shape), whole(state_shape), whole((tm, w_a))],
        out_shape=[jax.ShapeDtypeStruct((tm, d_model), _f32),
                   jax.ShapeDtypeStruct(state_shape, _f32),
                   jax.ShapeDtypeStruct(state_shape, _f32),
                   jax.ShapeDtypeStruct((tm, w_a), _f32)],
        scratch_shapes=_scratch(tm, ns, ts, d_model, wup.shape[-1], w_a, w_b, wmix.shape)
        + _weight_scratch(win, wout, wup, wdn),
        compiler_params=pltpu.CompilerParams(dimension_semantics=("arbitrary",),
                                             vmem_limit_bytes=VMEM_LIMIT_BYTES),
        name="sample_layer",
    )(x.reshape(tm, d_model), hdr, h0, *weights)
    return y.reshape(ns, ts, d_model), stc, sth, v.reshape(ns, ts, w_a)


def _block_diag(w):
    n, k, _ = w.shape
    eye = jnp.eye(n, dtype=w.dtype)
    return (eye[:, None, :, None] * w[:, :, None, :]).reshape(n * k, n * k)


def _prep_weights(l, chunk, norm1_g, w_in, ln_v_g, ln_v_b, w_s, b_s, conv_w, conv_b, w_a, b_a, w_x, b_x,
                  lam, gn_a_g, gn_b_g, w_out, norm2_g, w_up, w_down, normf_g):
    w_b = conv_b.shape[-1]
    d_model = norm1_g.shape[-1]
    hd_a = gn_a_g.shape[-1] // H_A
    heads_per_tile = MXU_DIM // hd_a
    rows5 = [ln_v_g[l], ln_v_b[l], conv_w[l, 0], conv_w[l, 1], conv_w[l, 2], conv_w[l, 3], conv_b[l],
             b_a[l], b_x[l], lam[l], gn_a_g[l], gn_b_g[l]]
    p5 = jnp.stack(rows5 + [jnp.zeros((w_b,), _f32)] * (16 - len(rows5)))
    p10 = jnp.stack([norm1_g[l], norm2_g[l], normf_g] + [jnp.zeros((d_model,), _f32)] * 5)
    ws = w_s[l][:, :chunk, :chunk]
    wmix = ws.reshape(H_A // heads_per_tile, heads_per_tile, chunk, chunk).transpose(0, 2, 1, 3)
    wmix = wmix.reshape(H_A // heads_per_tile, chunk, heads_per_tile * chunk)
    bmix = jnp.repeat(b_s[l][:, :chunk].T, hd_a, axis=1)
    hh = H_B // 2
    wgate = jnp.stack([jnp.concatenate([_block_diag(w_a[l, j * hh:(j + 1) * hh]),
                                        _block_diag(w_x[l, j * hh:(j + 1) * hh])], axis=1)
                       for j in range(2)]).astype(_bf16)
    return (p5, p10, w_in[l], wmix, bmix, wgate, w_out[l], w_up[l], w_down[l])


def _state_rows(conv_state, h_state):
    n, k, w = conv_state.shape
    hdr = jnp.concatenate([jnp.zeros((n, SUBLANES - k, w), conv_state.dtype), conv_state], axis=1)
    h0 = jnp.broadcast_to(h_state[:, None, :], (n, SUBLANES, w))
    return hdr, h0


def kernel(x_prompt, x_sample, state_conv_b, state_h_b, norm1_g, w_in, ln_v_g, ln_v_b, w_s, b_s, conv_w, conv_b,
           w_a, b_a, w_x, b_x, lam, gn_a_g, gn_b_g, w_out, norm2_g, w_up, w_down, normf_g):
    depth = w_in.shape[0]
    nb = x_prompt.shape[0]
    ns, ts, _ = x_sample.shape
    w_b = conv_b.shape[-1]
    params = (norm1_g, w_in, ln_v_g, ln_v_b, w_s, b_s, conv_w, conv_b, w_a, b_a, w_x, b_x, lam, gn_a_g, gn_b_g,
              w_out, norm2_g, w_up, w_down, normf_g)
    chunk_s = GMLP_CHUNK if ts % GMLP_CHUNK == 0 else ts
    tail = slice(SUBLANES - (CONV_W - 1), SUBLANES)
    hp, hs = x_prompt, x_sample
    conv_p, hlast_p, conv_s, hlast_s, v_s = [], [], [], [], []
    for l in range(depth):
        last = l == depth - 1
        hdr, h0 = _state_rows(jnp.zeros((nb, CONV_W - 1, w_b), _f32), jnp.zeros((nb, w_b), _f32))
        hp, stc, sth = _run_prompt_layer(hp, hdr[:, None], h0[:, None],
                                         _prep_weights(l, GMLP_CHUNK, *params),
                                         tm=PROMPT_TM, final_norm=last)
        conv_p.append(stc[:, 0, tail])
        hlast_p.append(sth[:, 0, SUBLANES - 1])
        hdr, h0 = _state_rows(state_conv_b[l], state_h_b[l])
        hs, stc, sth, vv = _run_sample_layer(hs, hdr[None], h0[None], _prep_weights(l, chunk_s, *params),
                                             chunk=chunk_s, final_norm=last)
        conv_s.append(stc[0, :, tail])
        hlast_s.append(sth[0, :, SUBLANES - 1])
        v_s.append(vv)
    return (hp, hs, jnp.stack(conv_p), jnp.stack(hlast_p), jnp.stack(conv_s), jnp.stack(hlast_s), jnp.stack(v_s))
```

```python
import functools

import jax
import jax.numpy as jnp
from jax import lax
from jax.experimental import pallas as pl
from jax.experimental.pallas import tpu as pltpu

H_A = 8
H_B = 8
GMLP_CHUNK = 128
CONV_W = 4
LRU_C = 8.0
EPS = 1e-6

SUBLANES = 8
PACK_ROWS = 16
MXU_DIM = 256
FF_CHUNK = 1024
LN_CHAINS = 2
PROMPT_TM = 512
STAGE_SLOTS, STAGE_ROWS, STAGE_COLS = 4, 128, 1024
VMEM_LIMIT_BYTES = 60 * 1024 * 1024

P_LNV_G, P_LNV_B, P_CONV_W, P_CONV_B, P_BA, P_BX, P_LAM, P_GNA, P_GNB = 0, 1, 2, 6, 7, 8, 9, 10, 11
P_NORM1, P_NORM2, P_NORMF = 0, 1, 2

_f32 = jnp.float32
_bf16 = jnp.bfloat16


def _rms(x, g):
    return x * lax.rsqrt(jnp.mean(x * x, axis=-1, keepdims=True) + EPS) * g


def _layernorm(x, g, b):
    mu = jnp.mean(x, axis=-1, keepdims=True)
    xc = x - mu
    var = jnp.mean(xc * xc, axis=-1, keepdims=True)
    return xc * lax.rsqrt(var + EPS) * g + b


def _sigmoid(x):
    return 0.5 * (1.0 + jnp.tanh(0.5 * x))


def _row(ref, r):
    return ref[r:r + 1, :]


def _load_weights_bf16(pairs, stage_ref, sem):
    n_slots, stage_rows, stage_cols = stage_ref.shape
    chunks = [(src, dst, r0, c0) for src, dst in pairs
              for c0 in range(0, src.shape[1], stage_cols) for r0 in range(0, src.shape[0], stage_rows)]

    def chunk_copy(k):
        src, _, r0, c0 = chunks[k]
        return pltpu.make_async_copy(src.at[r0:r0 + stage_rows, c0:c0 + stage_cols],
                                     stage_ref.at[k % n_slots], sem.at[k % n_slots])

    lookahead = n_slots - 1
    for k in range(min(lookahead, len(chunks))):
        chunk_copy(k).start()
    for k, (_, dst, r0, c0) in enumerate(chunks):
        if k + lookahead < len(chunks):
            chunk_copy(k + lookahead).start()
        chunk_copy(k).wait()
        dst[r0:r0 + stage_rows, c0:c0 + stage_cols] = stage_ref[k % n_slots].astype(_bf16)


def _init_proj(hdr_ref, wmix_ref, xpad_ref, wmixm_ref, *, chunk):
    n_tiles_a, _, kcat = wmix_ref.shape
    row = lax.broadcasted_iota(jnp.int32, (chunk, kcat), 0)
    col = lax.broadcasted_iota(jnp.int32, (chunk, kcat), 1) % chunk
    for q in range(n_tiles_a):
        wmixm_ref[q] = jnp.where(col <= row, wmix_ref[q], 0.0).astype(_bf16)
    xpad_ref[:, 0:SUBLANES, :] = hdr_ref[0]


def _proj_phases(x_ref, p5_ref, p10_ref, win_ref, wgate_ref, v_ref, wmixm_ref,
                 xn_ref, z_ref, xpad_ref, gu_ref, gg_ref, mixed_ref, xc_ref, pre_ref,
                 *, tm, chunk, seg, never=None):
    nseg = tm // seg
    nchunk = tm // chunk
    w_a = gu_ref.shape[-1]
    w_b = p5_ref.shape[-1]
    hd_a = w_a // H_A
    heads_per_tile = MXU_DIM // hd_a
    n_tiles_a = w_a // MXU_DIM

    x = x_ref[...]
    xn_ref[...] = (x * _row(p10_ref, P_NORM1)).astype(_bf16)
    r1 = lax.rsqrt(jnp.mean(x * x, axis=-1, keepdims=True) + EPS)
    d_mix = w_a + w_b
    z_ref[:, 0:d_mix] = jnp.dot(xn_ref[...], win_ref[:, 0:d_mix], preferred_element_type=_f32) * r1
    yield
    z_ref[:, d_mix:2 * d_mix] = jnp.dot(xn_ref[...], win_ref[:, d_mix:2 * d_mix], preferred_element_type=_f32) * r1
    yield

    gu_ref[...] = z_ref[:, 0:w_a]
    gg_ref[...] = z_ref[:, 2 * w_a + w_b:2 * w_a + 2 * w_b]

    v_blocks = []
    for r0 in range(0, tm, PACK_ROWS):
        zv = z_ref[r0:r0 + PACK_ROWS, w_a:2 * w_a]
        if never is not None and len(v_blocks) >= LN_CHAINS:
            zv = jnp.where(never, v_blocks[-LN_CHAINS], zv)
        v_blocks.append(_layernorm(jax.nn.gelu(zv), _row(p5_ref, P_LNV_G), _row(p5_ref, P_LNV_B)))
    v = jnp.concatenate(v_blocks, axis=0)
    if v_ref is not None:
        v_ref[...] = v
    lane = lax.broadcasted_iota(jnp.int32, (chunk, MXU_DIM), 1)
    for c in range(nchunk):
        rows = slice(c * chunk, (c + 1) * chunk)
        for q in range(n_tiles_a):
            cols = slice(q * MXU_DIM, (q + 1) * MXU_DIM)
            vq = v[rows, cols]
            rhs = jnp.concatenate(
                [jnp.where((lane >= hd_a * j) & (lane < hd_a * (j + 1)), vq, 0.0)
                 for j in range(heads_per_tile)], axis=0).astype(_bf16)
            mixed_ref[rows, cols] = jnp.dot(wmixm_ref[q], rhs, preferred_element_type=_f32)

    for s in range(nseg):
        xpad_ref[s, SUBLANES:SUBLANES + seg, :] = z_ref[s * seg:(s + 1) * seg, 2 * w_a:2 * w_a + w_b]
    xcs = []
    for s in range(nseg):
        acc = _row(p5_ref, P_CONV_B)
        for k in range(CONV_W):
            off = SUBLANES - (CONV_W - 1) + k
            acc = acc + xpad_ref[s, off:off + seg, :] * _row(p5_ref, P_CONV_W + k)
        xcs.append(acc)
        xpad_ref[s, 0:SUBLANES, :] = xpad_ref[s, seg:seg + SUBLANES, :]
    xc = jnp.concatenate(xcs, axis=0) if nseg > 1 else xcs[0]
    xc_ref[...] = xc
    xcb = xc.astype(_bf16)
    half = w_b // 2
    for j in range(2):
        res = jnp.dot(xcb[:, j * half:(j + 1) * half], wgate_ref[j], preferred_element_type=_f32)
        pre_ref[:, j * half:(j + 1) * half] = res[:, :half]
        pre_ref[:, w_b + j * half:w_b + (j + 1) * half] = res[:, half:]


def _gate_stage(p5_ref, bmix_ref, gu_ref, gg_ref, mixed_ref, xc_ref, pre_ref, hcar_ref, cat_ref,
                *, tm, chunk, seg, first_tile, never=None):
    nseg = tm // seg
    w_a = gu_ref.shape[-1]
    w_b = p5_ref.shape[-1]

    rowid = lax.broadcasted_iota(jnp.int32, (SUBLANES, w_b), 0)
    for s in range(nseg):
        hp = hcar_ref[s]
        link = None
        for r0 in range(s * seg, (s + 1) * seg, PACK_ROWS):
            ybs = []
            for g0 in range(r0, r0 + PACK_ROWS, SUBLANES):
                rows = slice(g0, g0 + SUBLANES)
                pre_r = pre_ref[rows, 0:w_b]
                pre_i = pre_ref[rows, w_b:2 * w_b]
                g = gg_ref[rows, :]
                if never is not None and link is not None:
                    pre_r = jnp.where(never, link[0], pre_r)
                    pre_i = jnp.where(never, link[1], pre_i)
                    g = jnp.where(never, link[0], g)
                r = _sigmoid(pre_r + _row(p5_ref, P_BA))
                i = _sigmoid(pre_i + _row(p5_ref, P_BX))
                log_a = -LRU_C * r * jax.nn.softplus(-_row(p5_ref, P_LAM))
                ag = jnp.exp(log_a)
                th = jnp.tanh(log_a)
                n = -2.0 * th
                mult = jnp.where(n > 0.0, n * lax.rsqrt(n * (1.0 - th)), 0.0)
                if first_tile is not None and g0 == s * seg:
                    mult = jnp.where((rowid == 0) & first_tile, 1.0, mult)
                bg = mult * (i * xc_ref[rows, :])
                for d in (1, 2, 4):
                    keep = rowid >= d
                    a_sh = jnp.where(keep, pltpu.roll(ag, d, 0), 1.0)
                    b_sh = jnp.where(keep, pltpu.roll(bg, d, 0), 0.0)
                    bg = bg + ag * b_sh
                    ag = ag * a_sh
                hg = bg + ag * hp
                hp = jnp.broadcast_to(hg[SUBLANES - 1:SUBLANES, :], (SUBLANES, w_b))
                link = (hp, hp)
                ybs.append(hg * jax.nn.gelu(g))
            rows = slice(r0, r0 + PACK_ROWS)
            yb = _rms(jnp.concatenate(ybs, axis=0), _row(p5_ref, P_GNB))
            cat_ref[rows, w_a:w_a + w_b] = yb.astype(_bf16)
            u = gu_ref[rows, :]
            if never is not None:
                u = jnp.where(never, yb, u)
            ya = jax.nn.gelu(u) * (mixed_ref[rows, :] + bmix_ref[r0 % chunk:r0 % chunk + PACK_ROWS, :])
            link = (ya[0:SUBLANES], ya[SUBLANES:PACK_ROWS])
            cat_ref[rows, 0:w_a] = _rms(ya, _row(p5_ref, P_GNA)).astype(_bf16)
        hcar_ref[s] = hp


def _down_stage(h1_ref, hid_ref, r2_ref, p10_ref, wdn_ref, y_ref, *, final_norm, never=None):
    down = jnp.dot(hid_ref[...], wdn_ref[...], preferred_element_type=_f32)
    tm = down.shape[0]
    prev = None
    for r0 in range(0, tm, PACK_ROWS):
        rows = slice(r0, r0 + PACK_ROWS)
        out = h1_ref[rows, :] + down[rows, :] * r2_ref[rows, 0:1]
        if never is not None and prev is not None:
            out = jnp.where(never, prev, out)
        prev = _rms(out, _row(p10_ref, P_NORMF)) if final_norm else out
        y_ref[rows, :] = prev


def _up_phases(x_ref, cat_ref, p10_ref, wout_ref, wup_ref, h1_ref, hn_ref, hid_ref, r2_ref, anchor=None):
    d_ff = wup_ref.shape[-1]
    x = x_ref[...]
    if anchor is not None:
        never, anchored_ref = anchor
        x = jnp.where(never, anchored_ref[...], x)
    h1_ref[...] = x + jnp.dot(cat_ref[...], wout_ref[...], preferred_element_type=_f32)
    yield
    h1 = h1_ref[...]
    hn_ref[...] = (h1 * _row(p10_ref, P_NORM2)).astype(_bf16)
    r2_ref[...] = jnp.broadcast_to(1.0 / (jnp.mean(h1 * h1, axis=-1, keepdims=True) + EPS), r2_ref.shape)
    for c in range(d_ff // FF_CHUNK):
        if c:
            yield
        cols = slice(c * FF_CHUNK, (c + 1) * FF_CHUNK)
        up = jnp.dot(hn_ref[...], wup_ref[:, cols], preferred_element_type=_f32)
        hid_ref[:, cols] = jnp.square(jnp.maximum(up.astype(_bf16), 0.0))


_PIPELINE_ORDER = "PPUUUUUP"
_PIPELINE_DEPTH = 2


def _pipelined_body(x_ref, xres_ref, hdr_ref, h0_ref, p5_ref, p10_ref, win_hbm, wmix_ref, bmix_ref, wgate_ref,
                    wout_hbm, wup_hbm, wdn_hbm, y_ref, stc_ref, sth_ref,
                    xn_ref, z_ref, xpad_ref, gu_ref, gg_ref, mixed_ref, xc_ref, pre_ref, hcar_ref, cat_ref,
                    h1_ref, hn_ref, hid_ref, r2_ref, wmixm_ref, win_ref, wout_ref, wup_ref, wdn_ref,
                    stage_ref, dma_sem, *, tm, chunk, n_tiles, tiles_per_seq, final_norm):
    s = pl.program_id(0)
    t_proj = jnp.minimum(s, n_tiles - 1) % tiles_per_seq
    t_gate = jnp.clip(s - 1, 0, n_tiles - 1) % tiles_per_seq

    @pl.when(s == 0)
    def _():
        for ref in (gu_ref, gg_ref, mixed_ref, xc_ref, pre_ref, h1_ref, hid_ref, r2_ref):
            ref[...] = jnp.zeros_like(ref)
        _load_weights_bf16(((win_hbm, win_ref), (wout_hbm, wout_ref), (wup_hbm, wup_ref), (wdn_hbm, wdn_ref)),
                           stage_ref, dma_sem)

    @pl.when(t_proj == 0)
    def _():
        _init_proj(hdr_ref, wmix_ref, xpad_ref, wmixm_ref, chunk=chunk)

    @pl.when(t_gate == 0)
    def _():
        hcar_ref[...] = h0_ref[0]

    _gate_stage(p5_ref, bmix_ref, gu_ref, gg_ref, mixed_ref, xc_ref, pre_ref, hcar_ref, cat_ref,
                tm=tm, chunk=chunk, seg=tm, first_tile=(t_gate == 0), never=(s < 0))
    _down_stage(h1_ref, hid_ref, r2_ref, p10_ref, wdn_ref, y_ref, final_norm=final_norm, never=(s < 0))
    up = _up_phases(xres_ref, cat_ref, p10_ref, wout_ref, wup_ref, h1_ref, hn_ref, hid_ref, r2_ref,
                    anchor=(s < 0, y_ref))
    proj = _proj_phases(x_ref, p5_ref, p10_ref, win_ref, wgate_ref, None, wmixm_ref,
                        xn_ref, z_ref, xpad_ref, gu_ref, gg_ref, mixed_ref, xc_ref, pre_ref,
                        tm=tm, chunk=chunk, seg=tm, never=(s < 0))
    for who in _PIPELINE_ORDER:
        next(proj if who == "P" else up, None)
    assert next(proj, "done") == "done" and next(up, "done") == "done"

    @pl.when((t_proj == tiles_per_seq - 1) & (s < n_tiles))
    def _():
        stc_ref[0] = xpad_ref[:, 0:SUBLANES, :]

    @pl.when((t_gate == tiles_per_seq - 1) & (s >= 1) & (s <= n_tiles))
    def _():
        sth_ref[0] = hcar_ref[...]


def _single_body(x_ref, hdr_ref, h0_ref, p5_ref, p10_ref, win_hbm, wmix_ref, bmix_ref, wgate_ref,
                 wout_hbm, wup_hbm, wdn_hbm, y_ref, stc_ref, sth_ref, v_ref,
                 xn_ref, z_ref, xpad_ref, gu_ref, gg_ref, mixed_ref, xc_ref, pre_ref, hcar_ref, cat_ref,
                 h1_ref, hn_ref, hid_ref, r2_ref, wmixm_ref, win_ref, wout_ref, wup_ref, wdn_ref,
                 stage_ref, dma_sem, *, tm, chunk, seg, final_norm):
    _load_weights_bf16(((win_hbm, win_ref), (wout_hbm, wout_ref), (wup_hbm, wup_ref), (wdn_hbm, wdn_ref)),
                       stage_ref, dma_sem)
    _init_proj(hdr_ref, wmix_ref, xpad_ref, wmixm_ref, chunk=chunk)
    hcar_ref[...] = h0_ref[0]
    for _ in _proj_phases(x_ref, p5_ref, p10_ref, win_ref, wgate_ref, v_ref, wmixm_ref,
                          xn_ref, z_ref, xpad_ref, gu_ref, gg_ref, mixed_ref, xc_ref, pre_ref,
                          tm=tm, chunk=chunk, seg=seg):
        pass
    _gate_stage(p5_ref, bmix_ref, gu_ref, gg_ref, mixed_ref, xc_ref, pre_ref, hcar_ref, cat_ref,
                tm=tm, chunk=chunk, seg=seg, first_tile=None)
    for _ in _up_phases(x_ref, cat_ref, p10_ref, wout_ref, wup_ref, h1_ref, hn_ref, hid_ref, r2_ref):
        pass
    _down_stage(h1_ref, hid_ref, r2_ref, p10_ref, wdn_ref, y_ref, final_norm=final_norm)
    stc_ref[0] = xpad_ref[:, 0:SUBLANES, :]
    sth_ref[0] = hcar_ref[...]


def _resident(shape):
    nd = len(shape)
    return pl.BlockSpec(shape, lambda i: (0,) * nd, pipeline_mode=pl.Buffered(1))


def _scratch(tm, nseg, seg, d_model, d_ff, w_a, w_b, wmix_shape):
    return [
        pltpu.VMEM((tm, d_model), _bf16),
        pltpu.VMEM((tm, 2 * w_a + 2 * w_b), _f32),
        pltpu.VMEM((nseg, seg + SUBLANES, w_b), _f32),
        pltpu.VMEM((tm, w_a), _f32),
        pltpu.VMEM((tm, w_b), _f32),
        pltpu.VMEM((tm, w_a), _f32),
        pltpu.VMEM((tm, w_b), _f32),
        pltpu.VMEM((tm, 2 * w_b), _f32),
        pltpu.VMEM((nseg, SUBLANES, w_b), _f32),
        pltpu.VMEM((tm, w_a + w_b), _bf16),
        pltpu.VMEM((tm, d_model), _f32),
        pltpu.VMEM((tm, d_model), _bf16),
        pltpu.VMEM((tm, d_ff), _bf16),
        pltpu.VMEM((tm, 128), _f32),
        pltpu.VMEM(wmix_shape, _bf16),
    ]


def _weight_scratch(*big_weights):
    return ([pltpu.VMEM(w.shape, _bf16) for w in big_weights]
            + [pltpu.VMEM((STAGE_SLOTS, STAGE_ROWS, STAGE_COLS), _f32), pltpu.SemaphoreType.DMA((STAGE_SLOTS,))])


def _run_prompt_layer(x, hdr, h0, weights, *, tm, final_norm):
    p5, p10, win, wmix, bmix, wgate, wout, wup, wdn = weights
    nb, T, d_model = x.shape
    w_b = p5.shape[-1]
    w_a = bmix.shape[-1]
    tiles_per_seq = T // tm
    n_tiles = nb * tiles_per_seq
    x2 = x.reshape(nb * T, d_model)
    body = functools.partial(_pipelined_body, tm=tm, chunk=GMLP_CHUNK, n_tiles=n_tiles,
                             tiles_per_seq=tiles_per_seq, final_norm=final_norm)
    tile = lambda lag: (lambda s: (jnp.clip(s - lag, 0, n_tiles - 1), 0))
    seq = lambda lag: (lambda s: (jnp.clip(s - lag, 0, n_tiles - 1) // tiles_per_seq, 0, 0, 0))
    in_hbm = pl.BlockSpec(memory_space=pl.ANY)
    state_block = (1, 1, SUBLANES, w_b)
    in_specs = [
        pl.BlockSpec((tm, d_model), tile(0)),
        pl.BlockSpec((tm, d_model), tile(1)),
        pl.BlockSpec(state_block, seq(0)),
        pl.BlockSpec(state_block, seq(1)),
        _resident(p5.shape), _resident(p10.shape), in_hbm, _resident(wmix.shape), _resident(bmix.shape),
        _resident(wgate.shape), in_hbm, in_hbm, in_hbm,
    ]
    y, stc, sth = pl.pallas_call(
        body,
        grid=(n_tiles + _PIPELINE_DEPTH,),
        in_specs=in_specs,
        out_specs=[pl.BlockSpec((tm, d_model), tile(_PIPELINE_DEPTH)),
                   pl.BlockSpec(state_block, seq(0)), pl.BlockSpec(state_block, seq(1))],
        out_shape=[jax.ShapeDtypeStruct((nb * T, d_model), _f32),
                   jax.ShapeDtypeStruct((nb, 1, SUBLANES, w_b), _f32),
                   jax.ShapeDtypeStruct((nb, 1, SUBLANES, w_b), _f32)],
        scratch_shapes=_scratch(tm, 1, tm, d_model, wup.shape[-1], w_a, w_b, wmix.shape)
        + _weight_scratch(win, wout, wup, wdn),
        compiler_params=pltpu.CompilerParams(dimension_semantics=("arbitrary",),
                                             vmem_limit_bytes=VMEM_LIMIT_BYTES),
        name="prompt_layer",
    )(x2, x2, hdr, h0, *weights)
    return y.reshape(nb, T, d_model), stc, sth


def _run_sample_layer(x, hdr, h0, weights, *, chunk, final_norm):
    p5, p10, win, wmix, bmix, wgate, wout, wup, wdn = weights
    ns, ts, d_model = x.shape
    w_b = p5.shape[-1]
    w_a = bmix.shape[-1]
    tm = ns * ts
    body = functools.partial(_single_body, tm=tm, chunk=chunk, seg=ts, final_norm=final_norm)
    whole = lambda shape: pl.BlockSpec(shape, lambda i: (0,) * len(shape))
    in_hbm = pl.BlockSpec(memory_space=pl.ANY)
    state_shape = (1, ns, SUBLANES, w_b)
    y, stc, sth, v = pl.pallas_call(
        body,
        grid=(1,),
        in_specs=[whole((tm, d_model)), whole(hdr.shape), whole(h0.shape),
                  _resident(p5.shape), _resident(p10.shape), in_hbm, _resident(wmix.shape), _resident(bmix.shape),
                  _resident(wgate.shape), in_hbm, in_hbm, in_hbm],
        out_specs=[whole((tm, d_model)), whole(state_shape), whole(state_shape), whole((tm, w_a))],
        out_shape=[jax.ShapeDtypeStruct((tm, d_model), _f32),
                   jax.ShapeDtypeStruct(state_shape, _f32),
                   jax.ShapeDtypeStruct(state_shape, _f32),
                   jax.ShapeDtypeStruct((tm, w_a), _f32)],
        scratch_shapes=_scratch(tm, ns, ts, d_model, wup.shape[-1], w_a, w_b, wmix.shape)
        + _weight_scratch(win, wout, wup, wdn),
        compiler_params=pltpu.CompilerParams(dimension_semantics=("arbitrary",),
                                             vmem_limit_bytes=VMEM_LIMIT_BYTES),
        name="sample_layer",
    )(x.reshape(tm, d_model), hdr, h0, *weights)
    return y.reshape(ns, ts, d_model), stc, sth, v.reshape(ns, ts, w_a)


def _block_diag(w):
    n, k, _ = w.shape
    eye = jnp.eye(n, dtype=w.dtype)
    return (eye[:, None, :, None] * w[:, :, None, :]).reshape(n * k, n * k)


def _prep_weights(l, chunk, norm1_g, w_in, ln_v_g, ln_v_b, w_s, b_s, conv_w, conv_b, w_a, b_a, w_x, b_x,
                  lam, gn_a_g, gn_b_g, w_out, norm2_g, w_up, w_down, normf_g):
    w_b = conv_b.shape[-1]
    d_model = norm1_g.shape[-1]
    hd_a = gn_a_g.shape[-1] // H_A
    heads_per_tile = MXU_DIM // hd_a
    rows5 = [ln_v_g[l], ln_v_b[l], conv_w[l, 0], conv_w[l, 1], conv_w[l, 2], conv_w[l, 3], conv_b[l],
             b_a[l], b_x[l], lam[l], gn_a_g[l], gn_b_g[l]]
    p5 = jnp.stack(rows5 + [jnp.zeros((w_b,), _f32)] * (16 - len(rows5)))
    p10 = jnp.stack([norm1_g[l], norm2_g[l], normf_g] + [jnp.zeros((d_model,), _f32)] * 5)
    ws = w_s[l][:, :chunk, :chunk]
    wmix = ws.reshape(H_A // heads_per_tile, heads_per_tile, chunk, chunk).transpose(0, 2, 1, 3)
    wmix = wmix.reshape(H_A // heads_per_tile, chunk, heads_per_tile * chunk)
    bmix = jnp.repeat(b_s[l][:, :chunk].T, hd_a, axis=1)
    hh = H_B // 2
    wgate = jnp.stack([jnp.concatenate([_block_diag(w_a[l, j * hh:(j + 1) * hh]),
                                        _block_diag(w_x[l, j * hh:(j + 1) * hh])], axis=1)
                       for j in range(2)]).astype(_bf16)
    return (p5, p10, w_in[l], wmix, bmix, wgate, w_out[l], w_up[l], w_down[l])


def _state_rows(conv_state, h_state):
    n, k, w = conv_state.shape
    hdr = jnp.concatenate([jnp.zeros((n, SUBLANES - k, w), conv_state.dtype), conv_state], axis=1)
    h0 = jnp.broadcast_to(h_state[:, None, :], (n, SUBLANES, w))
    return hdr, h0


def kernel(x_prompt, x_sample, state_conv_b, state_h_b, norm1_g, w_in, ln_v_g, ln_v_b, w_s, b_s, conv_w, conv_b,
           w_a, b_a, w_x, b_x, lam, gn_a_g, gn_b_g, w_out, norm2_g, w_up, w_down, normf_g):
    depth = w_in.shape[0]
    nb = x_prompt.shape[0]
    ns, ts, _ = x_sample.shape
    w_b = conv_b.shape[-1]
    params = (norm1_g, w_in, ln_v_g, ln_v_b, w_s, b_s, conv_w, conv_b, w_a, b_a, w_x, b_x, lam, gn_a_g, gn_b_g,
              w_out, norm2_g, w_up, w_down, normf_g)
    chunk_s = GMLP_CHUNK if ts % GMLP_CHUNK == 0 else ts
    tail = slice(SUBLANES - (CONV_W - 1), SUBLANES)
    hp, hs = x_prompt, x_sample
    conv_p, hlast_p, conv_s, hlast_s, v_s = [], [], [], [], []
    for l in range(depth):
        last = l == depth - 1
        hdr, h0 = _state_rows(jnp.zeros((nb, CONV_W - 1, w_b), _f32), jnp.zeros((nb, w_b), _f32))
        hp, stc, sth = _run_prompt_layer(hp, hdr[:, None], h0[:, None],
                                         _prep_weights(l, GMLP_CHUNK, *params),
                                         tm=PROMPT_TM, final_norm=last)
        conv_p.append(stc[:, 0, tail])
        hlast_p.append(sth[:, 0, SUBLANES - 1])
        hdr, h0 = _state_rows(state_conv_b[l], state_h_b[l])
        hs, stc, sth, vv = _run_sample_layer(hs, hdr[None], h0[None], _prep_weights(l, chunk_s, *params),
                                             chunk=chunk_s, final_norm=last)
        conv_s.append(stc[0, :, tail])
        hlast_s.append(sth[0, :, SUBLANES - 1])
        v_s.append(vv)
    return (hp, hs, jnp.stack(conv_p), jnp.stack(hlast_p), jnp.stack(conv_s), jnp.stack(hlast_s), jnp.stack(v_s))
```

```python
import functools

import jax
import jax.numpy as jnp
from jax import lax
from jax.experimental import pallas as pl
from jax.experimental.pallas import tpu as pltpu

H_A = 8
H_B = 8
GMLP_CHUNK = 128
CONV_W = 4
LRU_C = 8.0
EPS = 1e-6

SUBLANES = 8
PACK_ROWS = 16
MXU_DIM = 256
FF_CHUNK = 1024
LN_CHAINS = 2
PROMPT_TM = 512
STAGE_SLOTS, STAGE_ROWS, STAGE_COLS = 4, 256, 1024
VMEM_LIMIT_BYTES = 62 * 1024 * 1024

P_LNV_G, P_LNV_B, P_CONV_W, P_CONV_B, P_BA, P_BX, P_LAM, P_GNA, P_GNB = 0, 1, 2, 6, 7, 8, 9, 10, 11
P_NORM1, P_NORM2, P_NORMF = 0, 1, 2

_f32 = jnp.float32
_bf16 = jnp.bfloat16


def _rms(x, g):
    return x * lax.rsqrt(jnp.mean(x * x, axis=-1, keepdims=True) + EPS) * g


def _layernorm(x, g, b):
    mu = jnp.mean(x, axis=-1, keepdims=True)
    xc = x - mu
    var = jnp.mean(xc * xc, axis=-1, keepdims=True)
    return xc * lax.rsqrt(var + EPS) * g + b


def _sigmoid(x):
    return 0.5 * (1.0 + jnp.tanh(0.5 * x))


def _row(ref, r):
    return ref[r:r + 1, :]


def _load_weights_bf16(pairs, stage_ref, sem):
    n_slots, stage_rows, stage_cols = stage_ref.shape
    chunks = [(src, dst, r0, c0) for src, dst in pairs
              for c0 in range(0, src.shape[1], stage_cols) for r0 in range(0, src.shape[0], stage_rows)]

    def chunk_copy(k):
        src, _, r0, c0 = chunks[k]
        return pltpu.make_async_copy(src.at[r0:r0 + stage_rows, c0:c0 + stage_cols],
                                     stage_ref.at[k % n_slots], sem.at[k % n_slots])

    lookahead = n_slots - 1
    for k in range(min(lookahead, len(chunks))):
        chunk_copy(k).start()
    for k, (_, dst, r0, c0) in enumerate(chunks):
        if k + lookahead < len(chunks):
            chunk_copy(k + lookahead).start()
        chunk_copy(k).wait()
        dst[r0:r0 + stage_rows, c0:c0 + stage_cols] = stage_ref[k % n_slots].astype(_bf16)


def _init_proj(hdr_ref, wmix_ref, xpad_ref, wmixm_ref, *, chunk):
    n_tiles_a, _, kcat = wmix_ref.shape
    row = lax.broadcasted_iota(jnp.int32, (chunk, kcat), 0)
    col = lax.broadcasted_iota(jnp.int32, (chunk, kcat), 1) % chunk
    for q in range(n_tiles_a):
        wmixm_ref[q] = jnp.where(col <= row, wmix_ref[q], 0.0).astype(_bf16)
    xpad_ref[:, 0:SUBLANES, :] = hdr_ref[0]


def _proj_phases(x_ref, p5_ref, p10_ref, win_ref, wgate_ref, v_ref, wmixm_ref,
                 xn_ref, z_ref, xpad_ref, gu_ref, gg_ref, mixed_ref, xc_ref, pre_ref,
                 *, tm, chunk, seg, never=None):
    nseg = tm // seg
    nchunk = tm // chunk
    w_a = gu_ref.shape[-1]
    w_b = p5_ref.shape[-1]
    hd_a = w_a // H_A
    heads_per_tile = MXU_DIM // hd_a
    n_tiles_a = w_a // MXU_DIM

    x = x_ref[...]
    xn_ref[...] = (x * _row(p10_ref, P_NORM1)).astype(_bf16)
    r1 = lax.rsqrt(jnp.mean(x * x, axis=-1, keepdims=True) + EPS)
    d_mix = w_a + w_b
    z_ref[:, 0:d_mix] = jnp.dot(xn_ref[...], win_ref[:, 0:d_mix], preferred_element_type=_f32) * r1
    yield
    z_ref[:, d_mix:2 * d_mix] = jnp.dot(xn_ref[...], win_ref[:, d_mix:2 * d_mix], preferred_element_type=_f32) * r1
    yield

    gu_ref[...] = z_ref[:, 0:w_a]
    gg_ref[...] = z_ref[:, 2 * w_a + w_b:2 * w_a + 2 * w_b]

    v_blocks = []
    for r0 in range(0, tm, PACK_ROWS):
        zv = z_ref[r0:r0 + PACK_ROWS, w_a:2 * w_a]
        if never is not None and len(v_blocks) >= LN_CHAINS:
            zv = jnp.where(never, v_blocks[-LN_CHAINS], zv)
        v_blocks.append(_layernorm(jax.nn.gelu(zv), _row(p5_ref, P_LNV_G), _row(p5_ref, P_LNV_B)))
    v = jnp.concatenate(v_blocks, axis=0)
    if v_ref is not None:
        v_ref[...] = v
    lane = lax.broadcasted_iota(jnp.int32, (chunk, MXU_DIM), 1)
    for c in range(nchunk):
        rows = slice(c * chunk, (c + 1) * chunk)
        for q in range(n_tiles_a):
            cols = slice(q * MXU_DIM, (q + 1) * MXU_DIM)
            vq = v[rows, cols]
            rhs = jnp.concatenate(
                [jnp.where((lane >= hd_a * j) & (lane < hd_a * (j + 1)), vq, 0.0)
                 for j in range(heads_per_tile)], axis=0).astype(_bf16)
            mixed_ref[rows, cols] = jnp.dot(wmixm_ref[q], rhs, preferred_element_type=_f32)

    for s in range(nseg):
        xpad_ref[s, SUBLANES:SUBLANES + seg, :] = z_ref[s * seg:(s + 1) * seg, 2 * w_a:2 * w_a + w_b]
    xcs = []
    for s in range(nseg):
        acc = _row(p5_ref, P_CONV_B)
        for k in range(CONV_W):
            off = SUBLANES - (CONV_W - 1) + k
            acc = acc + xpad_ref[s, off:off + seg, :] * _row(p5_ref, P_CONV_W + k)
        xcs.append(acc)
        xpad_ref[s, 0:SUBLANES, :] = xpad_ref[s, seg:seg + SUBLANES, :]
    xc = jnp.concatenate(xcs, axis=0) if nseg > 1 else xcs[0]
    xc_ref[...] = xc
    xcb = xc.astype(_bf16)
    half = w_b // 2
    for j in range(2):
        res = jnp.dot(xcb[:, j * half:(j + 1) * half], wgate_ref[j], preferred_element_type=_f32)
        pre_ref[:, j * half:(j + 1) * half] = res[:, :half]
        pre_ref[:, w_b + j * half:w_b + (j + 1) * half] = res[:, half:]


def _gate_stage(p5_ref, bmix_ref, gu_ref, gg_ref, mixed_ref, xc_ref, pre_ref, hcar_ref, cat_ref,
                *, tm, chunk, seg, first_tile, never=None):
    nseg = tm // seg
    w_a = gu_ref.shape[-1]
    w_b = p5_ref.shape[-1]

    rowid = lax.broadcasted_iota(jnp.int32, (SUBLANES, w_b), 0)
    for s in range(nseg):
        hp = hcar_ref[s]
        link = None
        for r0 in range(s * seg, (s + 1) * seg, PACK_ROWS):
            ybs = []
            for g0 in range(r0, r0 + PACK_ROWS, SUBLANES):
                rows = slice(g0, g0 + SUBLANES)
                pre_r = pre_ref[rows, 0:w_b]
                pre_i = pre_ref[rows, w_b:2 * w_b]
                g = gg_ref[rows, :]
                if never is not None and link is not None:
                    pre_r = jnp.where(never, link[0], pre_r)
                    pre_i = jnp.where(never, link[1], pre_i)
                    g = jnp.where(never, link[0], g)
                r = _sigmoid(pre_r + _row(p5_ref, P_BA))
                i = _sigmoid(pre_i + _row(p5_ref, P_BX))
                log_a = -LRU_C * r * jax.nn.softplus(-_row(p5_ref, P_LAM))
                ag = jnp.exp(log_a)
                th = jnp.tanh(log_a)
                n = -2.0 * th
                mult = jnp.where(n > 0.0, n * lax.rsqrt(n * (1.0 - th)), 0.0)
                if first_tile is not None and g0 == s * seg:
                    mult = jnp.where((rowid == 0) & first_tile, 1.0, mult)
                bg = mult * (i * xc_ref[rows, :])
                for d in (1, 2, 4):
                    keep = rowid >= d
                    a_sh = jnp.where(keep, pltpu.roll(ag, d, 0), 1.0)
                    b_sh = jnp.where(keep, pltpu.roll(bg, d, 0), 0.0)
                    bg = bg + ag * b_sh
                    ag = ag * a_sh
                hg = bg + ag * hp
                hp = jnp.broadcast_to(hg[SUBLANES - 1:SUBLANES, :], (SUBLANES, w_b))
                link = (hp, hp)
                ybs.append(hg * jax.nn.gelu(g))
            rows = slice(r0, r0 + PACK_ROWS)
            yb = _rms(jnp.concatenate(ybs, axis=0), _row(p5_ref, P_GNB))
            cat_ref[rows, w_a:w_a + w_b] = yb.astype(_bf16)
            u = gu_ref[rows, :]
            if never is not None:
                u = jnp.where(never, yb, u)
            ya = jax.nn.gelu(u) * (mixed_ref[rows, :] + bmix_ref[r0 % chunk:r0 % chunk + PACK_ROWS, :])
            link = (ya[0:SUBLANES], ya[SUBLANES:PACK_ROWS])
            cat_ref[rows, 0:w_a] = _rms(ya, _row(p5_ref, P_GNA)).astype(_bf16)
        hcar_ref[s] = hp


def _down_stage(h1_ref, hid_ref, r2_ref, p10_ref, wdn_ref, y_ref, *, final_norm, never=None):
    down = jnp.dot(hid_ref[...], wdn_ref[...], preferred_element_type=_f32)
    tm = down.shape[0]
    prev = None
    for r0 in range(0, tm, PACK_ROWS):
        rows = slice(r0, r0 + PACK_ROWS)
        out = h1_ref[rows, :] + down[rows, :] * r2_ref[rows, 0:1]
        if never is not None and prev is not None:
            out = jnp.where(never, prev, out)
        prev = _rms(out, _row(p10_ref, P_NORMF)) if final_norm else out
        y_ref[rows, :] = prev


def _up_phases(x_ref, cat_ref, p10_ref, wout_ref, wup_ref, h1_ref, hn_ref, hid_ref, r2_ref, anchor=None):
    d_ff = wup_ref.shape[-1]
    x = x_ref[...]
    if anchor is not None:
        never, anchored_ref = anchor
        x = jnp.where(never, anchored_ref[...], x)
    h1_ref[...] = x + jnp.dot(cat_ref[...], wout_ref[...], preferred_element_type=_f32)
    yield
    h1 = h1_ref[...]
    hn_ref[...] = (h1 * _row(p10_ref, P_NORM2)).astype(_bf16)
    r2_ref[...] = jnp.broadcast_to(1.0 / (jnp.mean(h1 * h1, axis=-1, keepdims=True) + EPS), r2_ref.shape)
    for c in range(d_ff // FF_CHUNK):
        if c:
            yield
        cols = slice(c * FF_CHUNK, (c + 1) * FF_CHUNK)
        up = jnp.dot(hn_ref[...], wup_ref[:, cols], preferred_element_type=_f32)
        hid_ref[:, cols] = jnp.square(jnp.maximum(up.astype(_bf16), 0.0))


_PIPELINE_ORDER = "PPUUUUUP"
_PIPELINE_DEPTH = 2


def _pipelined_body(x_ref, xres_ref, hdr_ref, h0_ref, p5_ref, p10_ref, win_hbm, wmix_ref, bmix_ref, wgate_ref,
                    wout_hbm, wup_hbm, wdn_hbm, y_ref, stc_ref, sth_ref,
                    xn_ref, z_ref, xpad_ref, gu_ref, gg_ref, mixed_ref, xc_ref, pre_ref, hcar_ref, cat_ref,
                    h1_ref, hn_ref, hid_ref, r2_ref, wmixm_ref, win_ref, wout_ref, wup_ref, wdn_ref,
                    stage_ref, dma_sem, *, tm, chunk, n_tiles, tiles_per_seq, final_norm):
    s = pl.program_id(0)
    t_proj = jnp.minimum(s, n_tiles - 1) % tiles_per_seq
    t_gate = jnp.clip(s - 1, 0, n_tiles - 1) % tiles_per_seq

    @pl.when(s == 0)
    def _():
        for ref in (gu_ref, gg_ref, mixed_ref, xc_ref, pre_ref, h1_ref, hid_ref, r2_ref):
            ref[...] = jnp.zeros_like(ref)
        _load_weights_bf16(((win_hbm, win_ref), (wout_hbm, wout_ref), (wup_hbm, wup_ref), (wdn_hbm, wdn_ref)),
                           stage_ref, dma_sem)

    @pl.when(t_proj == 0)
    def _():
        _init_proj(hdr_ref, wmix_ref, xpad_ref, wmixm_ref, chunk=chunk)

    @pl.when(t_gate == 0)
    def _():
        hcar_ref[...] = h0_ref[0]

    _gate_stage(p5_ref, bmix_ref, gu_ref, gg_ref, mixed_ref, xc_ref, pre_ref, hcar_ref, cat_ref,
                tm=tm, chunk=chunk, seg=tm, first_tile=(t_gate == 0), never=(s < 0))
    _down_stage(h1_ref, hid_ref, r2_ref, p10_ref, wdn_ref, y_ref, final_norm=final_norm, never=(s < 0))
    up = _up_phases(xres_ref, cat_ref, p10_ref, wout_ref, wup_ref, h1_ref, hn_ref, hid_ref, r2_ref,
                    anchor=(s < 0, y_ref))
    proj = _proj_phases(x_ref, p5_ref, p10_ref, win_ref, wgate_ref, None, wmixm_ref,
                        xn_ref, z_ref, xpad_ref, gu_ref, gg_ref, mixed_ref, xc_ref, pre_ref,
                        tm=tm, chunk=chunk, seg=tm, never=(s < 0))
    for who in _PIPELINE_ORDER:
        next(proj if who == "P" else up, None)
    assert next(proj, "done") == "done" and next(up, "done") == "done"

    @pl.when((t_proj == tiles_per_seq - 1) & (s < n_tiles))
    def _():
        stc_ref[0] = xpad_ref[:, 0:SUBLANES, :]

    @pl.when((t_gate == tiles_per_seq - 1) & (s >= 1) & (s <= n_tiles))
    def _():
        sth_ref[0] = hcar_ref[...]


def _single_body(x_ref, hdr_ref, h0_ref, p5_ref, p10_ref, win_hbm, wmix_ref, bmix_ref, wgate_ref,
                 wout_hbm, wup_hbm, wdn_hbm, y_ref, stc_ref, sth_ref, v_ref,
                 xn_ref, z_ref, xpad_ref, gu_ref, gg_ref, mixed_ref, xc_ref, pre_ref, hcar_ref, cat_ref,
                 h1_ref, hn_ref, hid_ref, r2_ref, wmixm_ref, win_ref, wout_ref, wup_ref, wdn_ref,
                 stage_ref, dma_sem, *, tm, chunk, seg, final_norm):
    _load_weights_bf16(((win_hbm, win_ref), (wout_hbm, wout_ref), (wup_hbm, wup_ref), (wdn_hbm, wdn_ref)),
                       stage_ref, dma_sem)
    _init_proj(hdr_ref, wmix_ref, xpad_ref, wmixm_ref, chunk=chunk)
    hcar_ref[...] = h0_ref[0]
    for _ in _proj_phases(x_ref, p5_ref, p10_ref, win_ref, wgate_ref, v_ref, wmixm_ref,
                          xn_ref, z_ref, xpad_ref, gu_ref, gg_ref, mixed_ref, xc_ref, pre_ref,
                          tm=tm, chunk=chunk, seg=seg):
        pass
    _gate_stage(p5_ref, bmix_ref, gu_ref, gg_ref, mixed_ref, xc_ref, pre_ref, hcar_ref, cat_ref,
                tm=tm, chunk=chunk, seg=seg, first_tile=None)
    for _ in _up_phases(x_ref, cat_ref, p10_ref, wout_ref, wup_ref, h1_ref, hn_ref, hid_ref, r2_ref):
        pass
    _down_stage(h1_ref, hid_ref, r2_ref, p10_ref, wdn_ref, y_ref, final_norm=final_norm)
    stc_ref[0] = xpad_ref[:, 0:SUBLANES, :]
    sth_ref[0] = hcar_ref[...]


def _resident(shape):
    nd = len(shape)
    return pl.BlockSpec(shape, lambda i: (0,) * nd, pipeline_mode=pl.Buffered(1))


def _scratch(tm, nseg, seg, d_model, d_ff, w_a, w_b, wmix_shape):
    return [
        pltpu.VMEM((tm, d_model), _bf16),
        pltpu.VMEM((tm, 2 * w_a + 2 * w_b), _f32),
        pltpu.VMEM((nseg, seg + SUBLANES, w_b), _f32),
        pltpu.VMEM((tm, w_a), _f32),
        pltpu.VMEM((tm, w_b), _f32),
        pltpu.VMEM((tm, w_a), _f32),
        pltpu.VMEM((tm, w_b), _f32),
        pltpu.VMEM((tm, 2 * w_b), _f32),
        pltpu.VMEM((nseg, SUBLANES, w_b), _f32),
        pltpu.VMEM((tm, w_a + w_b), _bf16),
        pltpu.VMEM((tm, d_model), _f32),
        pltpu.VMEM((tm, d_model), _bf16),
        pltpu.VMEM((tm, d_ff), _bf16),
        pltpu.VMEM((tm, 128), _f32),
        pltpu.VMEM(wmix_shape, _bf16),
    ]


def _weight_scratch(*big_weights):
    return ([pltpu.VMEM(w.shape, _bf16) for w in big_weights]
            + [pltpu.VMEM((STAGE_SLOTS, STAGE_ROWS, STAGE_COLS), _f32), pltpu.SemaphoreType.DMA((STAGE_SLOTS,))])


def _run_prompt_layer(x, hdr, h0, weights, *, tm, final_norm):
    p5, p10, win, wmix, bmix, wgate, wout, wup, wdn = weights
    nb, T, d_model = x.shape
    w_b = p5.shape[-1]
    w_a = bmix.shape[-1]
    tiles_per_seq = T // tm
    n_tiles = nb * tiles_per_seq
    x2 = x.reshape(nb * T, d_model)
    body = functools.partial(_pipelined_body, tm=tm, chunk=GMLP_CHUNK, n_tiles=n_tiles,
                             tiles_per_seq=tiles_per_seq, final_norm=final_norm)
    tile = lambda lag: (lambda s: (jnp.clip(s - lag, 0, n_tiles - 1), 0))
    seq = lambda lag: (lambda s: (jnp.clip(s - lag, 0, n_tiles - 1) // tiles_per_seq, 0, 0, 0))
    in_hbm = pl.BlockSpec(memory_space=pl.ANY)
    state_block = (1, 1, SUBLANES, w_b)
    in_specs = [
        pl.BlockSpec((tm, d_model), tile(0)),
        pl.BlockSpec((tm, d_model), tile(1)),
        pl.BlockSpec(state_block, seq(0)),
        pl.BlockSpec(state_block, seq(1)),
        _resident(p5.shape), _resident(p10.shape), in_hbm, _resident(wmix.shape), _resident(bmix.shape),
        _resident(wgate.shape), in_hbm, in_hbm, in_hbm,
    ]
    y, stc, sth = pl.pallas_call(
        body,
        grid=(n_tiles + _PIPELINE_DEPTH,),
        in_specs=in_specs,
        out_specs=[pl.BlockSpec((tm, d_model), tile(_PIPELINE_DEPTH)),
                   pl.BlockSpec(state_block, seq(0)), pl.BlockSpec(state_block, seq(1))],
        out_shape=[jax.ShapeDtypeStruct((nb * T, d_model), _f32),
                   jax.ShapeDtypeStruct((nb, 1, SUBLANES, w_b), _f32),
                   jax.ShapeDtypeStruct((nb, 1, SUBLANES, w_b), _f32)],
        scratch_shapes=_scratch(tm, 1, tm, d_model, wup.shape[-1], w_a, w_b, wmix.shape)
        + _weight_scratch(win, wout, wup, wdn),
        compiler_params=pltpu.CompilerParams(dimension_semantics=("arbitrary",),
                                             vmem_limit_bytes=VMEM_LIMIT_BYTES),
        name="prompt_layer",
    )(x2, x2, hdr, h0, *weights)
    return y.reshape(nb, T, d_model), stc, sth


def _run_sample_layer(x, hdr, h0, weights, *, chunk, final_norm):
    p5, p10, win, wmix, bmix, wgate, wout, wup, wdn = weights
    ns, ts, d_model = x.shape
    w_b = p5.shape[-1]
    w_a = bmix.shape[-1]
    tm = ns * ts
    body = functools.partial(_single_body, tm=tm, chunk=chunk, seg=ts, final_norm=final_norm)
    whole = lambda shape: pl.BlockSpec(shape, lambda i: (0,) * len(shape))
    in_hbm = pl.BlockSpec(memory_space=pl.ANY)
    state_shape = (1, ns, SUBLANES, w_b)
    y, stc, sth, v = pl.pallas_call(
        body,
        grid=(1,),
        in_specs=[whole((tm, d_model)), whole(hdr.shape), whole(h0.shape),
                  _resident(p5.shape), _resident(p10.shape), in_hbm, _resident(wmix.shape), _resident(bmix.shape),
                  _resident(wgate.shape), in_hbm, in_hbm, in_hbm],
        out_specs=[whole((tm, d_model)), whole(state_shape), whole(state_shape), whole((tm, w_a))],
        out_shape=[jax.ShapeDtypeStruct((tm, d_model), _f32),
                   jax.ShapeDtypeStruct(state_shape, _f32),
                   jax.ShapeDtypeStruct(state_shape, _f32),
                   jax.ShapeDtypeStruct((tm, w_a), _f32)],
        scratch_shapes=_scratch(tm, ns, ts, d_model, wup.shape[-1], w_a, w_b, wmix.shape)
        + _weight_scratch(win, wout, wup, wdn),
        compiler_params=pltpu.CompilerParams(dimension_semantics=("arbitrary",),
                                             vmem_limit_bytes=VMEM_LIMIT_BYTES),
        name="sample_layer",
    )(x.reshape(tm, d_model), hdr, h0, *weights)
    return y.reshape(ns, ts, d_model), stc, sth, v.reshape(ns, ts, w_a)


def _block_diag(w):
    n, k, _ = w.shape
    eye = jnp.eye(n, dtype=w.dtype)
    return (eye[:, None, :, None] * w[:, :, None, :]).reshape(n * k, n * k)


def _prep_weights(l, chunk, norm1_g, w_in, ln_v_g, ln_v_b, w_s, b_s, conv_w, conv_b, w_a, b_a, w_x, b_x,
                  lam, gn_a_g, gn_b_g, w_out, norm2_g, w_up, w_down, normf_g):
    w_b = conv_b.shape[-1]
    d_model = norm1_g.shape[-1]
    hd_a = gn_a_g.shape[-1] // H_A
    heads_per_tile = MXU_DIM // hd_a
    rows5 = [ln_v_g[l], ln_v_b[l], conv_w[l, 0], conv_w[l, 1], conv_w[l, 2], conv_w[l, 3], conv_b[l],
             b_a[l], b_x[l], lam[l], gn_a_g[l], gn_b_g[l]]
    p5 = jnp.stack(rows5 + [jnp.zeros((w_b,), _f32)] * (16 - len(rows5)))
    p10 = jnp.stack([norm1_g[l], norm2_g[l], normf_g] + [jnp.zeros((d_model,), _f32)] * 5)
    ws = w_s[l][:, :chunk, :chunk]
    wmix = ws.reshape(H_A // heads_per_tile, heads_per_tile, chunk, chunk).transpose(0, 2, 1, 3)
    wmix = wmix.reshape(H_A // heads_per_tile, chunk, heads_per_tile * chunk)
    bmix = jnp.repeat(b_s[l][:, :chunk].T, hd_a, axis=1)
    hh = H_B // 2
    wgate = jnp.stack([jnp.concatenate([_block_diag(w_a[l, j * hh:(j + 1) * hh]),
                                        _block_diag(w_x[l, j * hh:(j + 1) * hh])], axis=1)
                       for j in range(2)]).astype(_bf16)
    return (p5, p10, w_in[l], wmix, bmix, wgate, w_out[l], w_up[l], w_down[l])


def _state_rows(conv_state, h_state):
    n, k, w = conv_state.shape
    hdr = jnp.concatenate([jnp.zeros((n, SUBLANES - k, w), conv_state.dtype), conv_state], axis=1)
    h0 = jnp.broadcast_to(h_state[:, None, :], (n, SUBLANES, w))
    return hdr, h0


def kernel(x_prompt, x_sample, state_conv_b, state_h_b, norm1_g, w_in, ln_v_g, ln_v_b, w_s, b_s, conv_w, conv_b,
           w_a, b_a, w_x, b_x, lam, gn_a_g, gn_b_g, w_out, norm2_g, w_up, w_down, normf_g):
    depth = w_in.shape[0]
    nb = x_prompt.shape[0]
    ns, ts, _ = x_sample.shape
    w_b = conv_b.shape[-1]
    params = (norm1_g, w_in, ln_v_g, ln_v_b, w_s, b_s, conv_w, conv_b, w_a, b_a, w_x, b_x, lam, gn_a_g, gn_b_g,
              w_out, norm2_g, w_up, w_down, normf_g)
    chunk_s = GMLP_CHUNK if ts % GMLP_CHUNK == 0 else ts
    tail = slice(SUBLANES - (CONV_W - 1), SUBLANES)
    hp, hs = x_prompt, x_sample
    conv_p, hlast_p, conv_s, hlast_s, v_s = [], [], [], [], []
    for l in range(depth):
        last = l == depth - 1
        hdr, h0 = _state_rows(jnp.zeros((nb, CONV_W - 1, w_b), _f32), jnp.zeros((nb, w_b), _f32))
        hp, stc, sth = _run_prompt_layer(hp, hdr[:, None], h0[:, None],
                                         _prep_weights(l, GMLP_CHUNK, *params),
                                         tm=PROMPT_TM, final_norm=last)
        conv_p.append(stc[:, 0, tail])
        hlast_p.append(sth[:, 0, SUBLANES - 1])
        hdr, h0 = _state_rows(state_conv_b[l], state_h_b[l])
        hs, stc, sth, vv = _run_sample_layer(hs, hdr[None], h0[None], _prep_weights(l, chunk_s, *params),
                                             chunk=chunk_s, final_norm=last)
        conv_s.append(stc[0, :, tail])
        hlast_s.append(sth[0, :, SUBLANES - 1])
        v_s.append(vv)
    return (hp, hs, jnp.stack(conv_p), jnp.stack(hlast_p), jnp.stack(conv_s), jnp.stack(hlast_s), jnp.stack(v_s))
```

```python
import functools

import jax
import jax.numpy as jnp
from jax import lax
from jax.experimental import pallas as pl
from jax.experimental.pallas import tpu as pltpu

H_A = 8
H_B = 8
GMLP_CHUNK = 128
CONV_W = 4
LRU_C = 8.0
EPS = 1e-6

SUBLANES = 8
PACK_ROWS = 16
MXU_DIM = 256
FF_CHUNK = 1024
LN_CHAINS = 2
PROMPT_TM = 512
STAGE_SLOTS, STAGE_ROWS, STAGE_COLS = 4, 256, 1024
VMEM_LIMIT_BYTES = 62 * 1024 * 1024

P_LNV_G, P_LNV_B, P_CONV_W, P_CONV_B, P_BA, P_BX, P_LAM, P_GNA, P_GNB = 0, 1, 2, 6, 7, 8, 9, 10, 11
P_NORM1, P_NORM2, P_NORMF = 0, 1, 2

_f32 = jnp.float32
_bf16 = jnp.bfloat16


def _rms(x, g):
    return x * lax.rsqrt(jnp.mean(x * x, axis=-1, keepdims=True) + EPS) * g


def _layernorm(x, g, b):
    mu = jnp.mean(x, axis=-1, keepdims=True)
    xc = x - mu
    var = jnp.mean(xc * xc, axis=-1, keepdims=True)
    return xc * lax.rsqrt(var + EPS) * g + b


def _sigmoid(x):
    return 0.5 * (1.0 + jnp.tanh(0.5 * x))


def _row(ref, r):
    return ref[r:r + 1, :]


def _load_weights_bf16(pairs, stage_ref, sem):
    n_slots, stage_rows, stage_cols = stage_ref.shape
    chunks = [(src, dst, r0, c0) for src, dst in pairs
              for c0 in range(0, src.shape[1], stage_cols) for r0 in range(0, src.shape[0], stage_rows)]

    def chunk_copy(k):
        src, _, r0, c0 = chunks[k]
        return pltpu.make_async_copy(src.at[r0:r0 + stage_rows, c0:c0 + stage_cols],
                                     stage_ref.at[k % n_slots], sem.at[k % n_slots])

    lookahead = n_slots - 1
    for k in range(min(lookahead, len(chunks))):
        chunk_copy(k).start()
    for k, (_, dst, r0, c0) in enumerate(chunks):
        if k + lookahead < len(chunks):
            chunk_copy(k + lookahead).start()
        chunk_copy(k).wait()
        dst[r0:r0 + stage_rows, c0:c0 + stage_cols] = stage_ref[k % n_slots].astype(_bf16)


def _init_state(ws_ref, conv0_ref, h0_ref, xpad_ref, hcar_ref, wmixm_ref, *, chunk, init_proj=True, init_gate=True):
    if init_proj:
        n_tiles_a, _, kcat = wmixm_ref.shape
        per_tile = kcat // chunk
        row = lax.broadcasted_iota(jnp.int32, (chunk, chunk), 0)
        col = lax.broadcasted_iota(jnp.int32, (chunk, chunk), 1)
        for q in range(n_tiles_a):
            wmixm_ref[q] = jnp.concatenate(
                [jnp.where(col <= row, ws_ref[q * per_tile + j, 0:chunk, 0:chunk], 0.0) for j in range(per_tile)],
                axis=1).astype(_bf16)
        xpad_ref[:, 0:SUBLANES, :] = jnp.zeros_like(xpad_ref[:, 0:SUBLANES, :])
        if conv0_ref is not None:
            xpad_ref[:, SUBLANES - (CONV_W - 1):SUBLANES, :] = conv0_ref[...]
    if init_gate:
        if h0_ref is None:
            hcar_ref[...] = jnp.zeros_like(hcar_ref)
        else:
            hcar_ref[...] = jnp.broadcast_to(h0_ref[...][:, None, :], hcar_ref.shape)


def _write_conv_state(stc_ref, xpad_ref):
    stc_ref[...] = xpad_ref[:, SUBLANES - (CONV_W - 1):SUBLANES, :]


def _write_h_state(sth_ref, hcar_ref):
    sth_ref[...] = hcar_ref[:, SUBLANES - 1:SUBLANES, :]


def _proj_phases(x_ref, p5_ref, p10_ref, win_ref, wgate_ref, v_ref, wmixm_ref,
                 xn_ref, z_ref, xpad_ref, gu_ref, gg_ref, mixed_ref, xc_ref, pre_ref,
                 *, tm, chunk, seg, never=None):
    nseg = tm // seg
    nchunk = tm // chunk
    w_a = gu_ref.shape[-1]
    w_b = p5_ref.shape[-1]
    hd_a = w_a // H_A
    heads_per_tile = MXU_DIM // hd_a
    n_tiles_a = w_a // MXU_DIM

    x = x_ref[...]
    xn_ref[...] = (x * _row(p10_ref, P_NORM1)).astype(_bf16)
    r1 = lax.rsqrt(jnp.mean(x * x, axis=-1, keepdims=True) + EPS)
    d_mix = w_a + w_b
    z_ref[:, 0:d_mix] = jnp.dot(xn_ref[...], win_ref[:, 0:d_mix], preferred_element_type=_f32) * r1
    yield
    z_ref[:, d_mix:2 * d_mix] = jnp.dot(xn_ref[...], win_ref[:, d_mix:2 * d_mix], preferred_element_type=_f32) * r1
    yield

    gu_ref[...] = z_ref[:, 0:w_a]
    gg_ref[...] = z_ref[:, 2 * w_a + w_b:2 * w_a + 2 * w_b]

    v_blocks = []
    for r0 in range(0, tm, PACK_ROWS):
        zv = z_ref[r0:r0 + PACK_ROWS, w_a:2 * w_a]
        if never is not None and len(v_blocks) >= LN_CHAINS:
            zv = jnp.where(never, v_blocks[-LN_CHAINS], zv)
        v_blocks.append(_layernorm(jax.nn.gelu(zv), _row(p5_ref, P_LNV_G), _row(p5_ref, P_LNV_B)))
    v = jnp.concatenate(v_blocks, axis=0)
    if v_ref is not None:
        v_ref[...] = v
    lane = lax.broadcasted_iota(jnp.int32, (chunk, MXU_DIM), 1)
    for c in range(nchunk):
        rows = slice(c * chunk, (c + 1) * chunk)
        for q in range(n_tiles_a):
            cols = slice(q * MXU_DIM, (q + 1) * MXU_DIM)
            vq = v[rows, cols]
            rhs = jnp.concatenate(
                [jnp.where((lane >= hd_a * j) & (lane < hd_a * (j + 1)), vq, 0.0)
                 for j in range(heads_per_tile)], axis=0).astype(_bf16)
            mixed_ref[rows, cols] = jnp.dot(wmixm_ref[q], rhs, preferred_element_type=_f32)

    for s in range(nseg):
        xpad_ref[s, SUBLANES:SUBLANES + seg, :] = z_ref[s * seg:(s + 1) * seg, 2 * w_a:2 * w_a + w_b]
    xcs = []
    for s in range(nseg):
        acc = _row(p5_ref, P_CONV_B)
        for k in range(CONV_W):
            off = SUBLANES - (CONV_W - 1) + k
            acc = acc + xpad_ref[s, off:off + seg, :] * _row(p5_ref, P_CONV_W + k)
        xcs.append(acc)
        xpad_ref[s, 0:SUBLANES, :] = xpad_ref[s, seg:seg + SUBLANES, :]
    xc = jnp.concatenate(xcs, axis=0) if nseg > 1 else xcs[0]
    xc_ref[...] = xc
    xcb = xc.astype(_bf16)
    half = w_b // 2
    for j in range(2):
        res = jnp.dot(xcb[:, j * half:(j + 1) * half], wgate_ref[j], preferred_element_type=_f32)
        pre_ref[:, j * half:(j + 1) * half] = res[:, :half]
        pre_ref[:, w_b + j * half:w_b + (j + 1) * half] = res[:, half:]


def _gate_stage(p5_ref, bmix_ref, gu_ref, gg_ref, mixed_ref, xc_ref, pre_ref, hcar_ref, cat_ref,
                *, tm, chunk, seg, first_tile, never=None):
    nseg = tm // seg
    w_a = gu_ref.shape[-1]
    w_b = p5_ref.shape[-1]

    rowid = lax.broadcasted_iota(jnp.int32, (SUBLANES, w_b), 0)
    for s in range(nseg):
        hp = hcar_ref[s]
        link = None
        for r0 in range(s * seg, (s + 1) * seg, PACK_ROWS):
            ybs = []
            for g0 in range(r0, r0 + PACK_ROWS, SUBLANES):
                rows = slice(g0, g0 + SUBLANES)
                pre_r = pre_ref[rows, 0:w_b]
                pre_i = pre_ref[rows, w_b:2 * w_b]
                g = gg_ref[rows, :]
                if never is not None and link is not None:
                    pre_r = jnp.where(never, link[0], pre_r)
                    pre_i = jnp.where(never, link[1], pre_i)
                    g = jnp.where(never, link[0], g)
                r = _sigmoid(pre_r + _row(p5_ref, P_BA))
                i = _sigmoid(pre_i + _row(p5_ref, P_BX))
                log_a = -LRU_C * r * jax.nn.softplus(-_row(p5_ref, P_LAM))
                ag = jnp.exp(log_a)
                th = jnp.tanh(log_a)
                n = -2.0 * th
                mult = jnp.where(n > 0.0, n * lax.rsqrt(n * (1.0 - th)), 0.0)
                if first_tile is not None and g0 == s * seg:
                    mult = jnp.where((rowid == 0) & first_tile, 1.0, mult)
                bg = mult * (i * xc_ref[rows, :])
                for d in (1, 2, 4):
                    keep = rowid >= d
                    a_sh = jnp.where(keep, pltpu.roll(ag, d, 0), 1.0)
                    b_sh = jnp.where(keep, pltpu.roll(bg, d, 0), 0.0)
                    bg = bg + ag * b_sh
                    ag = ag * a_sh
                hg = bg + ag * hp
                hp = jnp.broadcast_to(hg[SUBLANES - 1:SUBLANES, :], (SUBLANES, w_b))
                link = (hp, hp)
                ybs.append(hg * jax.nn.gelu(g))
            rows = slice(r0, r0 + PACK_ROWS)
            yb = _rms(jnp.concatenate(ybs, axis=0), _row(p5_ref, P_GNB))
            cat_ref[rows, w_a:w_a + w_b] = yb.astype(_bf16)
            u = gu_ref[rows, :]
            if never is not None:
                u = jnp.where(never, yb, u)
            ya = jax.nn.gelu(u) * (mixed_ref[rows, :] + bmix_ref[r0 % chunk:r0 % chunk + PACK_ROWS, :])
            link = (ya[0:SUBLANES], ya[SUBLANES:PACK_ROWS])
            cat_ref[rows, 0:w_a] = _rms(ya, _row(p5_ref, P_GNA)).astype(_bf16)
        hcar_ref[s] = hp


def _down_stage(h1_ref, hid_ref, r2_ref, p10_ref, wdn_ref, y_ref, *, final_norm, never=None):
    down = jnp.dot(hid_ref[...], wdn_ref[...], preferred_element_type=_f32)
    tm = down.shape[0]
    prev = None
    for r0 in range(0, tm, PACK_ROWS):
        rows = slice(r0, r0 + PACK_ROWS)
        out = h1_ref[rows, :] + down[rows, :] * r2_ref[rows, 0:1]
        if never is not None and prev is not None:
            out = jnp.where(never, prev, out)
        prev = _rms(out, _row(p10_ref, P_NORMF)) if final_norm else out
        y_ref[rows, :] = prev


def _up_phases(x_ref, cat_ref, p10_ref, wout_ref, wup_ref, h1_ref, hn_ref, hid_ref, r2_ref, anchor=None):
    d_ff = wup_ref.shape[-1]
    x = x_ref[...]
    if anchor is not None:
        never, anchored_ref = anchor
        x = jnp.where(never, anchored_ref[...], x)
    h1_ref[...] = x + jnp.dot(cat_ref[...], wout_ref[...], preferred_element_type=_f32)
    yield
    h1 = h1_ref[...]
    hn_ref[...] = (h1 * _row(p10_ref, P_NORM2)).astype(_bf16)
    r2_ref[...] = jnp.broadcast_to(1.0 / (jnp.mean(h1 * h1, axis=-1, keepdims=True) + EPS), r2_ref.shape)
    for c in range(d_ff // FF_CHUNK):
        if c:
            yield
        cols = slice(c * FF_CHUNK, (c + 1) * FF_CHUNK)
        up = jnp.dot(hn_ref[...], wup_ref[:, cols], preferred_element_type=_f32)
        hid_ref[:, cols] = jnp.square(jnp.maximum(up.astype(_bf16), 0.0))


_PIPELINE_ORDER = "PPUUUUUP"
_PIPELINE_DEPTH = 2


def _pipelined_body(x_ref, xres_ref, p5_ref, p10_ref, win_hbm, ws_ref, bmix_ref, wgate_ref,
                    wout_hbm, wup_hbm, wdn_hbm, y_ref, stc_ref, sth_ref,
                    xn_ref, z_ref, xpad_ref, gu_ref, gg_ref, mixed_ref, xc_ref, pre_ref, hcar_ref, cat_ref,
                    h1_ref, hn_ref, hid_ref, r2_ref, wmixm_ref, win_ref, wout_ref, wup_ref, wdn_ref,
                    stage_ref, dma_sem, *, tm, chunk, n_tiles, tiles_per_seq, final_norm):
    s = pl.program_id(0)
    t_proj = jnp.minimum(s, n_tiles - 1) % tiles_per_seq
    t_gate = jnp.clip(s - 1, 0, n_tiles - 1) % tiles_per_seq

    @pl.when(s == 0)
    def _():
        for ref in (gu_ref, gg_ref, mixed_ref, xc_ref, pre_ref, h1_ref, hid_ref, r2_ref):
            ref[...] = jnp.zeros_like(ref)
        _load_weights_bf16(((win_hbm, win_ref), (wout_hbm, wout_ref), (wup_hbm, wup_ref), (wdn_hbm, wdn_ref)),
                           stage_ref, dma_sem)

    @pl.when(t_proj == 0)
    def _():
        _init_state(ws_ref, None, None, xpad_ref, hcar_ref, wmixm_ref, chunk=chunk, init_gate=False)

    @pl.when(t_gate == 0)
    def _():
        _init_state(ws_ref, None, None, xpad_ref, hcar_ref, wmixm_ref, chunk=chunk, init_proj=False)

    _gate_stage(p5_ref, bmix_ref, gu_ref, gg_ref, mixed_ref, xc_ref, pre_ref, hcar_ref, cat_ref,
                tm=tm, chunk=chunk, seg=tm, first_tile=(t_gate == 0), never=(s < 0))
    _down_stage(h1_ref, hid_ref, r2_ref, p10_ref, wdn_ref, y_ref, final_norm=final_norm, never=(s < 0))
    up = _up_phases(xres_ref, cat_ref, p10_ref, wout_ref, wup_ref, h1_ref, hn_ref, hid_ref, r2_ref,
                    anchor=(s < 0, y_ref))
    proj = _proj_phases(x_ref, p5_ref, p10_ref, win_ref, wgate_ref, None, wmixm_ref,
                        xn_ref, z_ref, xpad_ref, gu_ref, gg_ref, mixed_ref, xc_ref, pre_ref,
                        tm=tm, chunk=chunk, seg=tm, never=(s < 0))
    for who in _PIPELINE_ORDER:
        next(proj if who == "P" else up, None)
    assert next(proj, "done") == "done" and next(up, "done") == "done"

    @pl.when((t_proj == tiles_per_seq - 1) & (s < n_tiles))
    def _():
        _write_conv_state(stc_ref, xpad_ref)

    @pl.when((t_gate == tiles_per_seq - 1) & (s >= 1) & (s <= n_tiles))
    def _():
        _write_h_state(sth_ref, hcar_ref)


def _single_body(x_ref, conv0_ref, h0_ref, p5_ref, p10_ref, win_hbm, ws_ref, bmix_ref, wgate_ref,
                 wout_hbm, wup_hbm, wdn_hbm, y_ref, stc_ref, sth_ref, v_ref,
                 xn_ref, z_ref, xpad_ref, gu_ref, gg_ref, mixed_ref, xc_ref, pre_ref, hcar_ref, cat_ref,
                 h1_ref, hn_ref, hid_ref, r2_ref, wmixm_ref, win_ref, wout_ref, wup_ref, wdn_ref,
                 stage_ref, dma_sem, *, tm, chunk, seg, final_norm):
    _load_weights_bf16(((win_hbm, win_ref), (wout_hbm, wout_ref), (wup_hbm, wup_ref), (wdn_hbm, wdn_ref)),
                       stage_ref, dma_sem)
    _init_state(ws_ref, conv0_ref, h0_ref, xpad_ref, hcar_ref, wmixm_ref, chunk=chunk)
    for _ in _proj_phases(x_ref, p5_ref, p10_ref, win_ref, wgate_ref, v_ref, wmixm_ref,
                          xn_ref, z_ref, xpad_ref, gu_ref, gg_ref, mixed_ref, xc_ref, pre_ref,
                          tm=tm, chunk=chunk, seg=seg):
        pass
    _gate_stage(p5_ref, bmix_ref, gu_ref, gg_ref, mixed_ref, xc_ref, pre_ref, hcar_ref, cat_ref,
                tm=tm, chunk=chunk, seg=seg, first_tile=None)
    for _ in _up_phases(x_ref, cat_ref, p10_ref, wout_ref, wup_ref, h1_ref, hn_ref, hid_ref, r2_ref):
        pass
    _down_stage(h1_ref, hid_ref, r2_ref, p10_ref, wdn_ref, y_ref, final_norm=final_norm)
    _write_conv_state(stc_ref, xpad_ref)
    _write_h_state(sth_ref, hcar_ref)


def _resident(shape):
    nd = len(shape)
    return pl.BlockSpec(shape, lambda i: (0,) * nd, pipeline_mode=pl.Buffered(1))


def _scratch(tm, nseg, seg, d_model, d_ff, w_a, w_b, wmix_shape):
    return [
        pltpu.VMEM((tm, d_model), _bf16),
        pltpu.VMEM((tm, 2 * w_a + 2 * w_b), _f32),
        pltpu.VMEM((nseg, seg + SUBLANES, w_b), _f32),
        pltpu.VMEM((tm, w_a), _f32),
        pltpu.VMEM((tm, w_b), _f32),
        pltpu.VMEM((tm, w_a), _f32),
        pltpu.VMEM((tm, w_b), _f32),
        pltpu.VMEM((tm, 2 * w_b), _f32),
        pltpu.VMEM((nseg, SUBLANES, w_b), _f32),
        pltpu.VMEM((tm, w_a + w_b), _bf16),
        pltpu.VMEM((tm, d_model), _f32),
        pltpu.VMEM((tm, d_model), _bf16),
        pltpu.VMEM((tm, d_ff), _bf16),
        pltpu.VMEM((tm, 128), _f32),
        pltpu.VMEM(wmix_shape, _bf16),
    ]


def _weight_scratch(*big_weights):
    return ([pltpu.VMEM(w.shape, _bf16) for w in big_weights]
            + [pltpu.VMEM((STAGE_SLOTS, STAGE_ROWS, STAGE_COLS), _f32), pltpu.SemaphoreType.DMA((STAGE_SLOTS,))])


def _mix_scratch_shape(chunk, w_a):
    heads_per_tile = MXU_DIM // (w_a // H_A)
    return (H_A // heads_per_tile, chunk, heads_per_tile * chunk)


def _run_prompt_layer(x, weights, *, tm, final_norm):
    p5, p10, win, ws, bmix, wgate, wout, wup, wdn = weights
    nb, T, d_model = x.shape
    w_b = p5.shape[-1]
    w_a = bmix.shape[-1]
    tiles_per_seq = T // tm
    n_tiles = nb * tiles_per_seq
    x2 = x.reshape(nb * T, d_model)
    body = functools.partial(_pipelined_body, tm=tm, chunk=GMLP_CHUNK, n_tiles=n_tiles,
                             tiles_per_seq=tiles_per_seq, final_norm=final_norm)
    tile = lambda lag: (lambda s: (jnp.clip(s - lag, 0, n_tiles - 1), 0))
    seq = lambda lag: (lambda s: (jnp.clip(s - lag, 0, n_tiles - 1) // tiles_per_seq, 0, 0))
    in_hbm = pl.BlockSpec(memory_space=pl.ANY)
    in_specs = [
        pl.BlockSpec((tm, d_model), tile(0)),
        pl.BlockSpec((tm, d_model), tile(1)),
        _resident(p5.shape), _resident(p10.shape), in_hbm, _resident(ws.shape), _resident(bmix.shape),
        _resident(wgate.shape), in_hbm, in_hbm, in_hbm,
    ]
    y, stc, sth = pl.pallas_call(
        body,
        grid=(n_tiles + _PIPELINE_DEPTH,),
        in_specs=in_specs,
        out_specs=[pl.BlockSpec((tm, d_model), tile(_PIPELINE_DEPTH)),
                   pl.BlockSpec((1, CONV_W - 1, w_b), seq(0)), pl.BlockSpec((1, 1, w_b), seq(1))],
        out_shape=[jax.ShapeDtypeStruct((nb * T, d_model), _f32),
                   jax.ShapeDtypeStruct((nb, CONV_W - 1, w_b), _f32),
                   jax.ShapeDtypeStruct((nb, 1, w_b), _f32)],
        scratch_shapes=_scratch(tm, 1, tm, d_model, wup.shape[-1], w_a, w_b, _mix_scratch_shape(GMLP_CHUNK, w_a))
        + _weight_scratch(win, wout, wup, wdn),
        compiler_params=pltpu.CompilerParams(dimension_semantics=("arbitrary",),
                                             vmem_limit_bytes=VMEM_LIMIT_BYTES),
        name="prompt_layer",
    )(x2, x2, *weights)
    return y.reshape(nb, T, d_model), stc, sth


def _run_sample_layer(x, conv0, h0, weights, *, chunk, final_norm):
    p5, p10, win, ws, bmix, wgate, wout, wup, wdn = weights
    ns, ts, d_model = x.shape
    w_b = p5.shape[-1]
    w_a = bmix.shape[-1]
    tm = ns * ts
    body = functools.partial(_single_body, tm=tm, chunk=chunk, seg=ts, final_norm=final_norm)
    whole = lambda shape: pl.BlockSpec(shape, lambda i: (0,) * len(shape))
    in_hbm = pl.BlockSpec(memory_space=pl.ANY)
    y, stc, sth, v = pl.pallas_call(
        body,
        grid=(1,),
        in_specs=[whole((tm, d_model)), whole(conv0.shape), whole(h0.shape),
                  _resident(p5.shape), _resident(p10.shape), in_hbm, _resident(ws.shape), _resident(bmix.shape),
                  _resident(wgate.shape), in_hbm, in_hbm, in_hbm],
        out_specs=[whole((tm, d_model)), whole((ns, CONV_W - 1, w_b)), whole((ns, 1, w_b)), whole((tm, w_a))],
        out_shape=[jax.ShapeDtypeStruct((tm, d_model), _f32),
                   jax.ShapeDtypeStruct((ns, CONV_W - 1, w_b), _f32),
                   jax.ShapeDtypeStruct((ns, 1, w_b), _f32),
                   jax.ShapeDtypeStruct((tm, w_a), _f32)],
        scratch_shapes=_scratch(tm, ns, ts, d_model, wup.shape[-1], w_a, w_b, _mix_scratch_shape(chunk, w_a))
        + _weight_scratch(win, wout, wup, wdn),
        compiler_params=pltpu.CompilerParams(dimension_semantics=("arbitrary",),
                                             vmem_limit_bytes=VMEM_LIMIT_BYTES),
        name="sample_layer",
    )(x.reshape(tm, d_model), conv0, h0, *weights)
    return y.reshape(ns, ts, d_model), stc, sth, v.reshape(ns, ts, w_a)


def _block_diag(w):
    n, k, _ = w.shape
    eye = jnp.eye(n, dtype=w.dtype)
    return (eye[:, None, :, None] * w[:, :, None, :]).reshape(n * k, n * k)


def _prep_weights(l, chunk, norm1_g, w_in, ln_v_g, ln_v_b, w_s, b_s, conv_w, conv_b, w_a, b_a, w_x, b_x,
                  lam, gn_a_g, gn_b_g, w_out, norm2_g, w_up, w_down, normf_g):
    w_b = conv_b.shape[-1]
    d_model = norm1_g.shape[-1]
    hd_a = gn_a_g.shape[-1] // H_A
    rows5 = [ln_v_g[l], ln_v_b[l], conv_w[l, 0], conv_w[l, 1], conv_w[l, 2], conv_w[l, 3], conv_b[l],
             b_a[l], b_x[l], lam[l], gn_a_g[l], gn_b_g[l]]
    p5 = jnp.stack(rows5 + [jnp.zeros((w_b,), _f32)] * (16 - len(rows5)))
    p10 = jnp.stack([norm1_g[l], norm2_g[l], normf_g] + [jnp.zeros((d_model,), _f32)] * 5)
    bmix = jnp.repeat(b_s[l][:, :chunk].T, hd_a, axis=1)
    hh = H_B // 2
    wgate = jnp.stack([jnp.concatenate([_block_diag(w_a[l, j * hh:(j + 1) * hh]),
                                        _block_diag(w_x[l, j * hh:(j + 1) * hh])], axis=1)
                       for j in range(2)]).astype(_bf16)
    return (p5, p10, w_in[l], w_s[l], bmix, wgate, w_out[l], w_up[l], w_down[l])


def kernel(x_prompt, x_sample, state_conv_b, state_h_b, norm1_g, w_in, ln_v_g, ln_v_b, w_s, b_s, conv_w, conv_b,
           w_a, b_a, w_x, b_x, lam, gn_a_g, gn_b_g, w_out, norm2_g, w_up, w_down, normf_g):
    depth = w_in.shape[0]
    nb = x_prompt.shape[0]
    ns, ts, _ = x_sample.shape
    w_b = conv_b.shape[-1]
    params = (norm1_g, w_in, ln_v_g, ln_v_b, w_s, b_s, conv_w, conv_b, w_a, b_a, w_x, b_x, lam, gn_a_g, gn_b_g,
              w_out, norm2_g, w_up, w_down, normf_g)
    chunk_s = GMLP_CHUNK if ts % GMLP_CHUNK == 0 else ts
    hp, hs = x_prompt, x_sample
    conv_p, hlast_p, conv_s, hlast_s, v_s = [], [], [], [], []
    for l in range(depth):
        last = l == depth - 1
        hp, stc, sth = _run_prompt_layer(hp, _prep_weights(l, GMLP_CHUNK, *params), tm=PROMPT_TM, final_norm=last)
        conv_p.append(stc)
        hlast_p.append(sth.reshape(nb, w_b))
        hs, stc, sth, vv = _run_sample_layer(hs, state_conv_b[l], state_h_b[l], _prep_weights(l, chunk_s, *params),
                                             chunk=chunk_s, final_norm=last)
        conv_s.append(stc)
        hlast_s.append(sth.reshape(ns, w_b))
        v_s.append(vv)
    return (hp, hs, jnp.stack(conv_p), jnp.stack(hlast_p), jnp.stack(conv_s), jnp.stack(hlast_s), jnp.stack(v_s))
```

```python
import functools

import jax
import jax.numpy as jnp
from jax import lax
from jax.experimental import pallas as pl
from jax.experimental.pallas import tpu as pltpu

H_A = 8
H_B = 8
GMLP_CHUNK = 128
CONV_W = 4
LRU_C = 8.0
EPS = 1e-6

SUBLANES = 8
PACK_ROWS = 16
MXU_DIM = 256
FF_CHUNK = 1024
LN_CHAINS = 2
PROMPT_TM = 512
STAGE_SLOTS, STAGE_ROWS, STAGE_COLS = 4, 256, 1024
VMEM_LIMIT_BYTES = 62 * 1024 * 1024

P_LNV_G, P_LNV_B, P_CONV_W, P_CONV_B, P_BA, P_BX, P_LAM, P_GNA, P_GNB = 0, 1, 2, 6, 7, 8, 9, 10, 11
P_NORM1, P_NORM2, P_NORMF = 0, 1, 2

_f32 = jnp.float32
_bf16 = jnp.bfloat16


def _rms(x, g):
    return x * lax.rsqrt(jnp.mean(x * x, axis=-1, keepdims=True) + EPS) * g


def _layernorm(x, g, b):
    mu = jnp.mean(x, axis=-1, keepdims=True)
    xc = x - mu
    var = jnp.mean(xc * xc, axis=-1, keepdims=True)
    return xc * lax.rsqrt(var + EPS) * g + b


def _sigmoid(x):
    return 0.5 * (1.0 + jnp.tanh(0.5 * x))


def _row(ref, r):
    return ref[r:r + 1, :]


def _load_weights_bf16(pairs, slots, sem):
    n_slots = len(slots)
    stage_rows, stage_cols = slots[0].shape
    chunks = [(src, dst, r0, c0) for src, dst in pairs
              for c0 in range(0, src.shape[1], stage_cols) for r0 in range(0, src.shape[0], stage_rows)]

    def chunk_copy(k):
        src, _, r0, c0 = chunks[k]
        return pltpu.make_async_copy(src.at[r0:r0 + stage_rows, c0:c0 + stage_cols],
                                     slots[k % n_slots], sem.at[k % n_slots])

    lookahead = n_slots - 1
    for k in range(min(lookahead, len(chunks))):
        chunk_copy(k).start()
    for k, (_, dst, r0, c0) in enumerate(chunks):
        if k + lookahead < len(chunks):
            chunk_copy(k + lookahead).start()
        chunk_copy(k).wait()
        dst[r0:r0 + stage_rows, c0:c0 + stage_cols] = slots[k % n_slots][...].astype(_bf16)


def _init_state(ws_ref, conv0_ref, h0_ref, xpad_ref, hcar_ref, wmixm_ref, *, chunk, init_proj=True, init_gate=True):
    if init_proj:
        n_tiles_a, _, kcat = wmixm_ref.shape
        per_tile = kcat // chunk
        row = lax.broadcasted_iota(jnp.int32, (chunk, chunk), 0)
        col = lax.broadcasted_iota(jnp.int32, (chunk, chunk), 1)
        for q in range(n_tiles_a):
            wmixm_ref[q] = jnp.concatenate(
                [jnp.where(col <= row, ws_ref[q * per_tile + j, 0:chunk, 0:chunk], 0.0) for j in range(per_tile)],
                axis=1).astype(_bf16)
        xpad_ref[:, 0:SUBLANES, :] = jnp.zeros_like(xpad_ref[:, 0:SUBLANES, :])
        if conv0_ref is not None:
            xpad_ref[:, SUBLANES - (CONV_W - 1):SUBLANES, :] = conv0_ref[...]
    if init_gate:
        if h0_ref is None:
            hcar_ref[...] = jnp.zeros_like(hcar_ref)
        else:
            hcar_ref[...] = jnp.broadcast_to(h0_ref[...][:, None, :], hcar_ref.shape)


def _write_conv_state(stc_ref, xpad_ref):
    stc_ref[...] = xpad_ref[:, SUBLANES - (CONV_W - 1):SUBLANES, :]


def _write_h_state(sth_ref, hcar_ref):
    sth_ref[...] = hcar_ref[:, SUBLANES - 1:SUBLANES, :]


def _proj_phases(x_ref, p5_ref, p10_ref, win_ref, wgate_ref, v_ref, wmixm_ref,
                 xn_ref, z_ref, xpad_ref, gu_ref, gg_ref, mixed_ref, xc_ref, pre_ref,
                 *, tm, chunk, seg, never=None):
    nseg = tm // seg
    nchunk = tm // chunk
    w_a = gu_ref.shape[-1]
    w_b = p5_ref.shape[-1]
    hd_a = w_a // H_A
    heads_per_tile = MXU_DIM // hd_a
    n_tiles_a = w_a // MXU_DIM

    x = x_ref[...]
    xn_ref[...] = (x * _row(p10_ref, P_NORM1)).astype(_bf16)
    r1 = lax.rsqrt(jnp.mean(x * x, axis=-1, keepdims=True) + EPS)
    d_mix = w_a + w_b
    z_ref[:, 0:d_mix] = jnp.dot(xn_ref[...], win_ref[:, 0:d_mix], preferred_element_type=_f32) * r1
    yield
    z_ref[:, d_mix:2 * d_mix] = jnp.dot(xn_ref[...], win_ref[:, d_mix:2 * d_mix], preferred_element_type=_f32) * r1
    yield

    gu_ref[...] = z_ref[:, 0:w_a]
    gg_ref[...] = z_ref[:, 2 * w_a + w_b:2 * w_a + 2 * w_b]

    v_blocks = []
    for r0 in range(0, tm, PACK_ROWS):
        zv = z_ref[r0:r0 + PACK_ROWS, w_a:2 * w_a]
        if never is not None and len(v_blocks) >= LN_CHAINS:
            zv = jnp.where(never, v_blocks[-LN_CHAINS], zv)
        v_blocks.append(_layernorm(jax.nn.gelu(zv), _row(p5_ref, P_LNV_G), _row(p5_ref, P_LNV_B)))
    v = jnp.concatenate(v_blocks, axis=0)
    if v_ref is not None:
        v_ref[...] = v
    lane = lax.broadcasted_iota(jnp.int32, (chunk, MXU_DIM), 1)
    for c in range(nchunk):
        rows = slice(c * chunk, (c + 1) * chunk)
        for q in range(n_tiles_a):
            cols = slice(q * MXU_DIM, (q + 1) * MXU_DIM)
            vq = v[rows, cols]
            rhs = jnp.concatenate(
                [jnp.where((lane >= hd_a * j) & (lane < hd_a * (j + 1)), vq, 0.0)
                 for j in range(heads_per_tile)], axis=0).astype(_bf16)
            mixed_ref[rows, cols] = jnp.dot(wmixm_ref[q], rhs, preferred_element_type=_f32)

    for s in range(nseg):
        xpad_ref[s, SUBLANES:SUBLANES + seg, :] = z_ref[s * seg:(s + 1) * seg, 2 * w_a:2 * w_a + w_b]
    xcs = []
    for s in range(nseg):
        acc = _row(p5_ref, P_CONV_B)
        for k in range(CONV_W):
            off = SUBLANES - (CONV_W - 1) + k
            acc = acc + xpad_ref[s, off:off + seg, :] * _row(p5_ref, P_CONV_W + k)
        xcs.append(acc)
        xpad_ref[s, 0:SUBLANES, :] = xpad_ref[s, seg:seg + SUBLANES, :]
    xc = jnp.concatenate(xcs, axis=0) if nseg > 1 else xcs[0]
    xc_ref[...] = xc
    xcb = xc.astype(_bf16)
    half = w_b // 2
    for j in range(2):
        res = jnp.dot(xcb[:, j * half:(j + 1) * half], wgate_ref[j], preferred_element_type=_f32)
        pre_ref[:, j * half:(j + 1) * half] = res[:, :half]
        pre_ref[:, w_b + j * half:w_b + (j + 1) * half] = res[:, half:]


def _gate_stage(p5_ref, bmix_ref, gu_ref, gg_ref, mixed_ref, xc_ref, pre_ref, hcar_ref, cat_ref,
                *, tm, chunk, seg, first_tile, never=None):
    nseg = tm // seg
    w_a = gu_ref.shape[-1]
    w_b = p5_ref.shape[-1]

    rowid = lax.broadcasted_iota(jnp.int32, (SUBLANES, w_b), 0)
    for s in range(nseg):
        hp = hcar_ref[s]
        link = None
        for r0 in range(s * seg, (s + 1) * seg, PACK_ROWS):
            ybs = []
            for g0 in range(r0, r0 + PACK_ROWS, SUBLANES):
                rows = slice(g0, g0 + SUBLANES)
                pre_r = pre_ref[rows, 0:w_b]
                pre_i = pre_ref[rows, w_b:2 * w_b]
                g = gg_ref[rows, :]
                if never is not None and link is not None:
                    pre_r = jnp.where(never, link[0], pre_r)
                    pre_i = jnp.where(never, link[1], pre_i)
                    g = jnp.where(never, link[0], g)
                r = _sigmoid(pre_r + _row(p5_ref, P_BA))
                i = _sigmoid(pre_i + _row(p5_ref, P_BX))
                log_a = -LRU_C * r * jax.nn.softplus(-_row(p5_ref, P_LAM))
                ag = jnp.exp(log_a)
                th = jnp.tanh(log_a)
                n = -2.0 * th
                mult = jnp.where(n > 0.0, n * lax.rsqrt(n * (1.0 - th)), 0.0)
                if first_tile is not None and g0 == s * seg:
                    mult = jnp.where((rowid == 0) & first_tile, 1.0, mult)
                bg = mult * (i * xc_ref[rows, :])
                for d in (1, 2, 4):
                    keep = rowid >= d
                    a_sh = jnp.where(keep, pltpu.roll(ag, d, 0), 1.0)
                    b_sh = jnp.where(keep, pltpu.roll(bg, d, 0), 0.0)
                    bg = bg + ag * b_sh
                    ag = ag * a_sh
                hg = bg + ag * hp
                hp = jnp.broadcast_to(hg[SUBLANES - 1:SUBLANES, :], (SUBLANES, w_b))
                link = (hp, hp)
                ybs.append(hg * jax.nn.gelu(g))
            rows = slice(r0, r0 + PACK_ROWS)
            yb = _rms(jnp.concatenate(ybs, axis=0), _row(p5_ref, P_GNB))
            cat_ref[rows, w_a:w_a + w_b] = yb.astype(_bf16)
            u = gu_ref[rows, :]
            if never is not None:
                u = jnp.where(never, yb, u)
            ya = jax.nn.gelu(u) * (mixed_ref[rows, :] + bmix_ref[r0 % chunk:r0 % chunk + PACK_ROWS, :])
            link = (ya[0:SUBLANES], ya[SUBLANES:PACK_ROWS])
            cat_ref[rows, 0:w_a] = _rms(ya, _row(p5_ref, P_GNA)).astype(_bf16)
        hcar_ref[s] = hp


def _down_stage(h1_ref, hid_ref, r2_ref, p10_ref, wdn_ref, y_ref, *, final_norm, never=None):
    down = jnp.dot(hid_ref[...], wdn_ref[...], preferred_element_type=_f32)
    tm = down.shape[0]
    prev = None
    for r0 in range(0, tm, PACK_ROWS):
        rows = slice(r0, r0 + PACK_ROWS)
        out = h1_ref[rows, :] + down[rows, :] * r2_ref[rows, 0:1]
        if never is not None and prev is not None:
            out = jnp.where(never, prev, out)
        prev = _rms(out, _row(p10_ref, P_NORMF)) if final_norm else out
        y_ref[rows, :] = prev


def _up_phases(x_ref, cat_ref, p10_ref, wout_ref, wup_ref, h1_ref, hn_ref, hid_ref, r2_ref, anchor=None):
    d_ff = wup_ref.shape[-1]
    x = x_ref[...]
    if anchor is not None:
        never, anchored_ref = anchor
        x = jnp.where(never, anchored_ref[...], x)
    h1_ref[...] = x + jnp.dot(cat_ref[...], wout_ref[...], preferred_element_type=_f32)
    yield
    h1 = h1_ref[...]
    hn_ref[...] = (h1 * _row(p10_ref, P_NORM2)).astype(_bf16)
    r2_ref[...] = jnp.broadcast_to(1.0 / (jnp.mean(h1 * h1, axis=-1, keepdims=True) + EPS), r2_ref.shape)
    for c in range(d_ff // FF_CHUNK):
        if c:
            yield
        cols = slice(c * FF_CHUNK, (c + 1) * FF_CHUNK)
        up = jnp.dot(hn_ref[...], wup_ref[:, cols], preferred_element_type=_f32)
        hid_ref[:, cols] = jnp.square(jnp.maximum(up.astype(_bf16), 0.0))


_PIPELINE_ORDER = "PPUUUUUP"
_PIPELINE_DEPTH = 2


def _pipelined_body(x_ref, xres_ref, p5_ref, p10_ref, win_hbm, ws_ref, bmix_ref, wgate_ref,
                    wout_hbm, wup_hbm, wdn_hbm, y_ref, stc_ref, sth_ref,
                    xn_ref, z_ref, xpad_ref, gu_ref, gg_ref, mixed_ref, xc_ref, pre_ref, hcar_ref, cat_ref,
                    h1_ref, hn_ref, hid_ref, r2_ref, wmixm_ref, win_ref, wout_ref, wup_ref, wdn_ref,
                    dma_sem, *, tm, chunk, n_tiles, tiles_per_seq, final_norm):
    s = pl.program_id(0)
    t_proj = jnp.minimum(s, n_tiles - 1) % tiles_per_seq
    t_gate = jnp.clip(s - 1, 0, n_tiles - 1) % tiles_per_seq

    @pl.when(s == 0)
    def _():
        for ref in (h1_ref, hid_ref, r2_ref):
            ref[...] = jnp.zeros_like(ref)
        slots = [z_ref.at[r0:r0 + STAGE_ROWS, c0:c0 + STAGE_COLS]
                 for c0 in range(0, z_ref.shape[1], STAGE_COLS) for r0 in range(0, tm, STAGE_ROWS)]
        _load_weights_bf16(((win_hbm, win_ref), (wout_hbm, wout_ref), (wup_hbm, wup_ref), (wdn_hbm, wdn_ref)),
                           slots[:STAGE_SLOTS], dma_sem)

    @pl.when(t_proj == 0)
    def _():
        _init_state(ws_ref, None, None, xpad_ref, hcar_ref, wmixm_ref, chunk=chunk, init_gate=False)

    @pl.when(t_gate == 0)
    def _():
        _init_state(ws_ref, None, None, xpad_ref, hcar_ref, wmixm_ref, chunk=chunk, init_proj=False)

    def run_stages(with_proj, with_gate_up, with_down):
        never = s < 0 if with_proj or with_gate_up else None
        up = proj = iter(())
        if with_gate_up:
            _gate_stage(p5_ref, bmix_ref, gu_ref, gg_ref, mixed_ref, xc_ref, pre_ref, hcar_ref, cat_ref,
                        tm=tm, chunk=chunk, seg=tm, first_tile=(t_gate == 0), never=never)
        if with_down:
            _down_stage(h1_ref, hid_ref, r2_ref, p10_ref, wdn_ref, y_ref, final_norm=final_norm, never=never)
        if with_gate_up:
            up = _up_phases(xres_ref, cat_ref, p10_ref, wout_ref, wup_ref, h1_ref, hn_ref, hid_ref, r2_ref,
                            anchor=(never, y_ref) if with_down else None)
        if with_proj:
            proj = _proj_phases(x_ref, p5_ref, p10_ref, win_ref, wgate_ref, None, wmixm_ref,
                                xn_ref, z_ref, xpad_ref, gu_ref, gg_ref, mixed_ref, xc_ref, pre_ref,
                                tm=tm, chunk=chunk, seg=tm, never=never)
        for who in _PIPELINE_ORDER:
            next(proj if who == "P" else up, None)
        assert next(proj, "done") == "done" and next(up, "done") == "done"

    filling, draining, last = s == 0, s == n_tiles, s == n_tiles + 1
    pl.when(filling)(functools.partial(run_stages, True, False, False))
    pl.when(draining)(functools.partial(run_stages, False, True, True))
    pl.when(last)(functools.partial(run_stages, False, False, True))
    pl.when(~(filling | draining | last))(functools.partial(run_stages, True, True, True))

    @pl.when((t_proj == tiles_per_seq - 1) & (s < n_tiles))
    def _():
        _write_conv_state(stc_ref, xpad_ref)

    @pl.when((t_gate == tiles_per_seq - 1) & (s >= 1) & (s <= n_tiles))
    def _():
        _write_h_state(sth_ref, hcar_ref)


def _single_body(x_ref, conv0_ref, h0_ref, p5_ref, p10_ref, win_hbm, ws_ref, bmix_ref, wgate_ref,
                 wout_hbm, wup_hbm, wdn_hbm, y_ref, stc_ref, sth_ref, v_ref,
                 xn_ref, z_ref, xpad_ref, gu_ref, gg_ref, mixed_ref, xc_ref, pre_ref, hcar_ref, cat_ref,
                 h1_ref, hn_ref, hid_ref, r2_ref, wmixm_ref, win_ref, wout_ref, wup_ref, wdn_ref,
                 stage_ref, dma_sem, *, tm, chunk, seg, final_norm):
    _load_weights_bf16(((win_hbm, win_ref), (wout_hbm, wout_ref), (wup_hbm, wup_ref), (wdn_hbm, wdn_ref)),
                       [stage_ref.at[k] for k in range(STAGE_SLOTS)], dma_sem)
    _init_state(ws_ref, conv0_ref, h0_ref, xpad_ref, hcar_ref, wmixm_ref, chunk=chunk)
    for _ in _proj_phases(x_ref, p5_ref, p10_ref, win_ref, wgate_ref, v_ref, wmixm_ref,
                          xn_ref, z_ref, xpad_ref, gu_ref, gg_ref, mixed_ref, xc_ref, pre_ref,
                          tm=tm, chunk=chunk, seg=seg):
        pass
    _gate_stage(p5_ref, bmix_ref, gu_ref, gg_ref, mixed_ref, xc_ref, pre_ref, hcar_ref, cat_ref,
                tm=tm, chunk=chunk, seg=seg, first_tile=None)
    for _ in _up_phases(x_ref, cat_ref, p10_ref, wout_ref, wup_ref, h1_ref, hn_ref, hid_ref, r2_ref):
        pass
    _down_stage(h1_ref, hid_ref, r2_ref, p10_ref, wdn_ref, y_ref, final_norm=final_norm)
    _write_conv_state(stc_ref, xpad_ref)
    _write_h_state(sth_ref, hcar_ref)


def _resident(shape):
    nd = len(shape)
    return pl.BlockSpec(shape, lambda i: (0,) * nd, pipeline_mode=pl.Buffered(1))


def _scratch(tm, nseg, seg, d_model, d_ff, w_a, w_b, wmix_shape):
    return [
        pltpu.VMEM((tm, d_model), _bf16),
        pltpu.VMEM((tm, 2 * w_a + 2 * w_b), _f32),
        pltpu.VMEM((nseg, seg + SUBLANES, w_b), _f32),
        pltpu.VMEM((tm, w_a), _f32),
        pltpu.VMEM((tm, w_b), _f32),
        pltpu.VMEM((tm, w_a), _f32),
        pltpu.VMEM((tm, w_b), _f32),
        pltpu.VMEM((tm, 2 * w_b), _f32),
        pltpu.VMEM((nseg, SUBLANES, w_b), _f32),
        pltpu.VMEM((tm, w_a + w_b), _bf16),
        pltpu.VMEM((tm, d_model), _f32),
        pltpu.VMEM((tm, d_model), _bf16),
        pltpu.VMEM((tm, d_ff), _bf16),
        pltpu.VMEM((tm, 128), _f32),
        pltpu.VMEM(wmix_shape, _bf16),
    ]


def _weight_scratch(*big_weights, own_staging):
    staging = [pltpu.VMEM((STAGE_SLOTS, STAGE_ROWS, STAGE_COLS), _f32)] if own_staging else []
    return [pltpu.VMEM(w.shape, _bf16) for w in big_weights] + staging + [pltpu.SemaphoreType.DMA((STAGE_SLOTS,))]


def _mix_scratch_shape(chunk, w_a):
    heads_per_tile = MXU_DIM // (w_a // H_A)
    return (H_A // heads_per_tile, chunk, heads_per_tile * chunk)


def _run_prompt_layer(x, weights, *, tm, final_norm):
    p5, p10, win, ws, bmix, wgate, wout, wup, wdn = weights
    nb, T, d_model = x.shape
    w_b = p5.shape[-1]
    w_a = bmix.shape[-1]
    tiles_per_seq = T // tm
    n_tiles = nb * tiles_per_seq
    x2 = x.reshape(nb * T, d_model)
    body = functools.partial(_pipelined_body, tm=tm, chunk=GMLP_CHUNK, n_tiles=n_tiles,
                             tiles_per_seq=tiles_per_seq, final_norm=final_norm)
    tile = lambda lag: (lambda s: (jnp.clip(s - lag, 0, n_tiles - 1), 0))
    seq = lambda lag: (lambda s: (jnp.clip(s - lag, 0, n_tiles - 1) // tiles_per_seq, 0, 0))
    in_hbm = pl.BlockSpec(memory_space=pl.ANY)
    in_specs = [
        pl.BlockSpec((tm, d_model), tile(0)),
        pl.BlockSpec((tm, d_model), tile(1)),
        _resident(p5.shape), _resident(p10.shape), in_hbm, _resident(ws.shape), _resident(bmix.shape),
        _resident(wgate.shape), in_hbm, in_hbm, in_hbm,
    ]
    y, stc, sth = pl.pallas_call(
        body,
        grid=(n_tiles + _PIPELINE_DEPTH,),
        in_specs=in_specs,
        out_specs=[pl.BlockSpec((tm, d_model), tile(_PIPELINE_DEPTH)),
                   pl.BlockSpec((1, CONV_W - 1, w_b), seq(0)), pl.BlockSpec((1, 1, w_b), seq(1))],
        out_shape=[jax.ShapeDtypeStruct((nb * T, d_model), _f32),
                   jax.ShapeDtypeStruct((nb, CONV_W - 1, w_b), _f32),
                   jax.ShapeDtypeStruct((nb, 1, w_b), _f32)],
        scratch_shapes=_scratch(tm, 1, tm, d_model, wup.shape[-1], w_a, w_b, _mix_scratch_shape(GMLP_CHUNK, w_a))
        + _weight_scratch(win, wout, wup, wdn, own_staging=False),
        compiler_params=pltpu.CompilerParams(dimension_semantics=("arbitrary",),
                                             vmem_limit_bytes=VMEM_LIMIT_BYTES),
        name="prompt_layer",
    )(x2, x2, *weights)
    return y.reshape(nb, T, d_model), stc, sth


def _run_sample_layer(x, conv0, h0, weights, *, chunk, final_norm):
    p5, p10, win, ws, bmix, wgate, wout, wup, wdn = weights
    ns, ts, d_model = x.shape
    w_b = p5.shape[-1]
    w_a = bmix.shape[-1]
    tm = ns * ts
    body = functools.partial(_single_body, tm=tm, chunk=chunk, seg=ts, final_norm=final_norm)
    whole = lambda shape: pl.BlockSpec(shape, lambda i: (0,) * len(shape))
    in_hbm = pl.BlockSpec(memory_space=pl.ANY)
    y, stc, sth, v = pl.pallas_call(
        body,
        grid=(1,),
        in_specs=[whole((tm, d_model)), whole(conv0.shape), whole(h0.shape),
                  _resident(p5.shape), _resident(p10.shape), in_hbm, _resident(ws.shape), _resident(bmix.shape),
                  _resident(wgate.shape), in_hbm, in_hbm, in_hbm],
        out_specs=[whole((tm, d_model)), whole((ns, CONV_W - 1, w_b)), whole((ns, 1, w_b)), whole((tm, w_a))],
        out_shape=[jax.ShapeDtypeStruct((tm, d_model), _f32),
                   jax.ShapeDtypeStruct((ns, CONV_W - 1, w_b), _f32),
                   jax.ShapeDtypeStruct((ns, 1, w_b), _f32),
                   jax.ShapeDtypeStruct((tm, w_a), _f32)],
        scratch_shapes=_scratch(tm, ns, ts, d_model, wup.shape[-1], w_a, w_b, _mix_scratch_shape(chunk, w_a))
        + _weight_scratch(win, wout, wup, wdn, own_staging=True),
        compiler_params=pltpu.CompilerParams(dimension_semantics=("arbitrary",),
                                             vmem_limit_bytes=VMEM_LIMIT_BYTES),
        name="sample_layer",
    )(x.reshape(tm, d_model), conv0, h0, *weights)
    return y.reshape(ns, ts, d_model), stc, sth, v.reshape(ns, ts, w_a)


def _block_diag(w):
    n, k, _ = w.shape
    eye = jnp.eye(n, dtype=w.dtype)
    return (eye[:, None, :, None] * w[:, :, None, :]).reshape(n * k, n * k)


def _prep_weights(l, chunk, norm1_g, w_in, ln_v_g, ln_v_b, w_s, b_s, conv_w, conv_b, w_a, b_a, w_x, b_x,
                  lam, gn_a_g, gn_b_g, w_out, norm2_g, w_up, w_down, normf_g):
    w_b = conv_b.shape[-1]
    d_model = norm1_g.shape[-1]
    hd_a = gn_a_g.shape[-1] // H_A
    rows5 = [ln_v_g[l], ln_v_b[l], conv_w[l, 0], conv_w[l, 1], conv_w[l, 2], conv_w[l, 3], conv_b[l],
             b_a[l], b_x[l], lam[l], gn_a_g[l], gn_b_g[l]]
    p5 = jnp.stack(rows5 + [jnp.zeros((w_b,), _f32)] * (16 - len(rows5)))
    p10 = jnp.stack([norm1_g[l], norm2_g[l], normf_g] + [jnp.zeros((d_model,), _f32)] * 5)
    bmix = jnp.repeat(b_s[l][:, :chunk].T, hd_a, axis=1)
    hh = H_B // 2
    wgate = jnp.stack([jnp.concatenate([_block_diag(w_a[l, j * hh:(j + 1) * hh]),
                                        _block_diag(w_x[l, j * hh:(j + 1) * hh])], axis=1)
                       for j in range(2)]).astype(_bf16)
    return (p5, p10, w_in[l], w_s[l], bmix, wgate, w_out[l], w_up[l], w_down[l])


def kernel(x_prompt, x_sample, state_conv_b, state_h_b, norm1_g, w_in, ln_v_g, ln_v_b, w_s, b_s, conv_w, conv_b,
           w_a, b_a, w_x, b_x, lam, gn_a_g, gn_b_g, w_out, norm2_g, w_up, w_down, normf_g):
    depth = w_in.shape[0]
    nb = x_prompt.shape[0]
    ns, ts, _ = x_sample.shape
    w_b = conv_b.shape[-1]
    params = (norm1_g, w_in, ln_v_g, ln_v_b, w_s, b_s, conv_w, conv_b, w_a, b_a, w_x, b_x, lam, gn_a_g, gn_b_g,
              w_out, norm2_g, w_up, w_down, normf_g)
    chunk_s = GMLP_CHUNK if ts % GMLP_CHUNK == 0 else ts
    hp, hs = x_prompt, x_sample
    conv_p, hlast_p, conv_s, hlast_s, v_s = [], [], [], [], []
    for l in range(depth):
        last = l == depth - 1
        hp, stc, sth = _run_prompt_layer(hp, _prep_weights(l, GMLP_CHUNK, *params), tm=PROMPT_TM, final_norm=last)
        conv_p.append(stc)
        hlast_p.append(sth.reshape(nb, w_b))
        hs, stc, sth, vv = _run_sample_layer(hs, state_conv_b[l], state_h_b[l], _prep_weights(l, chunk_s, *params),
                                             chunk=chunk_s, final_norm=last)
        conv_s.append(stc)
        hlast_s.append(sth.reshape(ns, w_b))
        v_s.append(vv)
    return (hp, hs, jnp.stack(conv_p), jnp.stack(hlast_p), jnp.stack(conv_s), jnp.stack(hlast_s), jnp.stack(v_s))
```

```python
import functools

import jax
import jax.numpy as jnp
from jax import lax
from jax.experimental import pallas as pl
from jax.experimental.pallas import tpu as pltpu

H_A = 8
H_B = 8
GMLP_CHUNK = 128
CONV_W = 4
LRU_C = 8.0
EPS = 1e-6

SUBLANES = 8
PACK_ROWS = 16
MXU_DIM = 256
FF_CHUNK = 1024
LN_CHAINS = 2
PROMPT_TM = 512
STAGE_SLOTS, STAGE_ROWS, STAGE_COLS = 4, 256, 1024
VMEM_LIMIT_BYTES = 62 * 1024 * 1024

P_LNV_G, P_LNV_B, P_CONV_W, P_CONV_B, P_BA, P_BX, P_LAM, P_GNA, P_GNB = 0, 1, 2, 6, 7, 8, 9, 10, 11
P_NORM1, P_NORM2, P_NORMF = 0, 1, 2

_f32 = jnp.float32
_bf16 = jnp.bfloat16


def _rms(x, g):
    return x * lax.rsqrt(jnp.mean(x * x, axis=-1, keepdims=True) + EPS) * g


def _layernorm(x, g, b):
    mu = jnp.mean(x, axis=-1, keepdims=True)
    xc = x - mu
    var = jnp.mean(xc * xc, axis=-1, keepdims=True)
    return xc * lax.rsqrt(var + EPS) * g + b


def _sigmoid(x):
    return 0.5 * (1.0 + jnp.tanh(0.5 * x))


def _row(ref, r):
    return ref[r:r + 1, :]


def _load_weights_bf16(pairs, slots, sem):
    n_slots = len(slots)
    stage_rows, stage_cols = slots[0].shape
    chunks = [(src, dst, r0, c0) for src, dst in pairs
              for c0 in range(0, src.shape[1], stage_cols) for r0 in range(0, src.shape[0], stage_rows)]

    def chunk_copy(k):
        src, _, r0, c0 = chunks[k]
        return pltpu.make_async_copy(src.at[r0:r0 + stage_rows, c0:c0 + stage_cols],
                                     slots[k % n_slots], sem.at[k % n_slots])

    lookahead = n_slots - 1
    for k in range(min(lookahead, len(chunks))):
        chunk_copy(k).start()
    for k, (_, dst, r0, c0) in enumerate(chunks):
        if k + lookahead < len(chunks):
            chunk_copy(k + lookahead).start()
        chunk_copy(k).wait()
        dst[r0:r0 + stage_rows, c0:c0 + stage_cols] = slots[k % n_slots][...].astype(_bf16)


def _init_state(ws_ref, conv0_ref, h0_ref, xpad_ref, hcar_ref, wmixm_ref, *, chunk, init_proj=True, init_gate=True):
    if init_proj:
        n_tiles_a, _, kcat = wmixm_ref.shape
        per_tile = kcat // chunk
        row = lax.broadcasted_iota(jnp.int32, (chunk, chunk), 0)
        col = lax.broadcasted_iota(jnp.int32, (chunk, chunk), 1)
        for q in range(n_tiles_a):
            wmixm_ref[q] = jnp.concatenate(
                [jnp.where(col <= row, ws_ref[q * per_tile + j, 0:chunk, 0:chunk], 0.0) for j in range(per_tile)],
                axis=1).astype(_bf16)
        xpad_ref[:, 0:SUBLANES, :] = jnp.zeros_like(xpad_ref[:, 0:SUBLANES, :])
        if conv0_ref is not None:
            xpad_ref[:, SUBLANES - (CONV_W - 1):SUBLANES, :] = conv0_ref[...]
    if init_gate:
        if h0_ref is None:
            hcar_ref[...] = jnp.zeros_like(hcar_ref)
        else:
            hcar_ref[...] = jnp.broadcast_to(h0_ref[...][:, None, :], hcar_ref.shape)


def _write_conv_state(stc_ref, xpad_ref):
    stc_ref[...] = xpad_ref[:, SUBLANES - (CONV_W - 1):SUBLANES, :]


def _write_h_state(sth_ref, hcar_ref):
    sth_ref[...] = hcar_ref[:, SUBLANES - 1:SUBLANES, :]


def _proj_phases(x_ref, p5_ref, p10_ref, win_ref, wgate_ref, v_ref, wmixm_ref,
                 xn_ref, z_ref, xpad_ref, gu_ref, gg_ref, mixed_ref, xc_ref, pre_ref,
                 *, tm, chunk, seg, never=None):
    nseg = tm // seg
    nchunk = tm // chunk
    w_a = gu_ref.shape[-1]
    w_b = p5_ref.shape[-1]
    hd_a = w_a // H_A
    heads_per_tile = MXU_DIM // hd_a
    n_tiles_a = w_a // MXU_DIM

    x = x_ref[...]
    xn_ref[...] = (x * _row(p10_ref, P_NORM1)).astype(_bf16)
    r1 = lax.rsqrt(jnp.mean(x * x, axis=-1, keepdims=True) + EPS)
    d_mix = w_a + w_b
    z_ref[:, 0:d_mix] = jnp.dot(xn_ref[...], win_ref[:, 0:d_mix], preferred_element_type=_f32) * r1
    yield
    z_ref[:, d_mix:2 * d_mix] = jnp.dot(xn_ref[...], win_ref[:, d_mix:2 * d_mix], preferred_element_type=_f32) * r1
    yield

    gu_ref[...] = z_ref[:, 0:w_a]
    gg_ref[...] = z_ref[:, 2 * w_a + w_b:2 * w_a + 2 * w_b]

    v_blocks = []
    for r0 in range(0, tm, PACK_ROWS):
        zv = z_ref[r0:r0 + PACK_ROWS, w_a:2 * w_a]
        if never is not None and len(v_blocks) >= LN_CHAINS:
            zv = jnp.where(never, v_blocks[-LN_CHAINS], zv)
        v_blocks.append(_layernorm(jax.nn.gelu(zv), _row(p5_ref, P_LNV_G), _row(p5_ref, P_LNV_B)))
    v = jnp.concatenate(v_blocks, axis=0)
    if v_ref is not None:
        v_ref[...] = v
    lane = lax.broadcasted_iota(jnp.int32, (chunk, MXU_DIM), 1)
    for c in range(nchunk):
        rows = slice(c * chunk, (c + 1) * chunk)
        for q in range(n_tiles_a):
            cols = slice(q * MXU_DIM, (q + 1) * MXU_DIM)
            vq = v[rows, cols]
            rhs = jnp.concatenate(
                [jnp.where((lane >= hd_a * j) & (lane < hd_a * (j + 1)), vq, 0.0)
                 for j in range(heads_per_tile)], axis=0).astype(_bf16)
            mixed_ref[rows, cols] = jnp.dot(wmixm_ref[q], rhs, preferred_element_type=_f32)

    for s in range(nseg):
        xpad_ref[s, SUBLANES:SUBLANES + seg, :] = z_ref[s * seg:(s + 1) * seg, 2 * w_a:2 * w_a + w_b]
    xcs = []
    for s in range(nseg):
        acc = _row(p5_ref, P_CONV_B)
        for k in range(CONV_W):
            off = SUBLANES - (CONV_W - 1) + k
            acc = acc + xpad_ref[s, off:off + seg, :] * _row(p5_ref, P_CONV_W + k)
        xcs.append(acc)
        xpad_ref[s, 0:SUBLANES, :] = xpad_ref[s, seg:seg + SUBLANES, :]
    xc = jnp.concatenate(xcs, axis=0) if nseg > 1 else xcs[0]
    xc_ref[...] = xc
    xcb = xc.astype(_bf16)
    half = w_b // 2
    for j in range(2):
        res = jnp.dot(xcb[:, j * half:(j + 1) * half], wgate_ref[j], preferred_element_type=_f32)
        pre_ref[:, j * half:(j + 1) * half] = res[:, :half]
        pre_ref[:, w_b + j * half:w_b + (j + 1) * half] = res[:, half:]


def _gate_stage(p5_ref, bmix_ref, gu_ref, gg_ref, mixed_ref, xc_ref, pre_ref, hcar_ref, cat_ref,
                *, tm, chunk, seg, first_tile, never=None):
    nseg = tm // seg
    w_a = gu_ref.shape[-1]
    w_b = p5_ref.shape[-1]

    rowid = lax.broadcasted_iota(jnp.int32, (SUBLANES, w_b), 0)
    for s in range(nseg):
        hp = hcar_ref[s]
        link = None
        for r0 in range(s * seg, (s + 1) * seg, PACK_ROWS):
            ybs = []
            for g0 in range(r0, r0 + PACK_ROWS, SUBLANES):
                rows = slice(g0, g0 + SUBLANES)
                pre_r = pre_ref[rows, 0:w_b]
                pre_i = pre_ref[rows, w_b:2 * w_b]
                g = gg_ref[rows, :]
                if never is not None and link is not None:
                    pre_r = jnp.where(never, link[0], pre_r)
                    pre_i = jnp.where(never, link[1], pre_i)
                    g = jnp.where(never, link[0], g)
                r = _sigmoid(pre_r + _row(p5_ref, P_BA))
                i = _sigmoid(pre_i + _row(p5_ref, P_BX))
                log_a = -LRU_C * r * jax.nn.softplus(-_row(p5_ref, P_LAM))
                ag = jnp.exp(log_a)
                th = jnp.tanh(log_a)
                n = -2.0 * th
                mult = jnp.where(n > 0.0, n * lax.rsqrt(n * (1.0 - th)), 0.0)
                if first_tile is not None and g0 == s * seg:
                    mult = jnp.where((rowid == 0) & first_tile, 1.0, mult)
                bg = mult * (i * xc_ref[rows, :])
                for d in (1, 2, 4):
                    keep = rowid >= d
                    a_sh = jnp.where(keep, pltpu.roll(ag, d, 0), 1.0)
                    b_sh = jnp.where(keep, pltpu.roll(bg, d, 0), 0.0)
                    bg = bg + ag * b_sh
                    ag = ag * a_sh
                hg = bg + ag * hp
                hp = jnp.broadcast_to(hg[SUBLANES - 1:SUBLANES, :], (SUBLANES, w_b))
                link = (hp, hp)
                ybs.append(hg * jax.nn.gelu(g))
            rows = slice(r0, r0 + PACK_ROWS)
            yb = _rms(jnp.concatenate(ybs, axis=0), _row(p5_ref, P_GNB))
            cat_ref[rows, w_a:w_a + w_b] = yb.astype(_bf16)
            u = gu_ref[rows, :]
            if never is not None:
                u = jnp.where(never, yb, u)
            ya = jax.nn.gelu(u) * (mixed_ref[rows, :] + bmix_ref[r0 % chunk:r0 % chunk + PACK_ROWS, :])
            link = (ya[0:SUBLANES], ya[SUBLANES:PACK_ROWS])
            cat_ref[rows, 0:w_a] = _rms(ya, _row(p5_ref, P_GNA)).astype(_bf16)
        hcar_ref[s] = hp


def _down_stage(h1_ref, hid_ref, r2_ref, p10_ref, wdn_ref, y_ref, *, final_norm, never=None):
    down = jnp.dot(hid_ref[...], wdn_ref[...], preferred_element_type=_f32)
    tm = down.shape[0]
    prev = None
    for r0 in range(0, tm, PACK_ROWS):
        rows = slice(r0, r0 + PACK_ROWS)
        out = h1_ref[rows, :] + down[rows, :] * r2_ref[rows, 0:1]
        if never is not None and prev is not None:
            out = jnp.where(never, prev, out)
        prev = _rms(out, _row(p10_ref, P_NORMF)) if final_norm else out
        y_ref[rows, :] = prev


def _up_phases(x_ref, cat_ref, p10_ref, wout_ref, wup_ref, h1_ref, hn_ref, hid_ref, r2_ref, anchor=None):
    d_ff = wup_ref.shape[-1]
    x = x_ref[...]
    if anchor is not None:
        never, anchored_ref = anchor
        x = jnp.where(never, anchored_ref[...], x)
    h1_ref[...] = x + jnp.dot(cat_ref[...], wout_ref[...], preferred_element_type=_f32)
    yield
    h1 = h1_ref[...]
    hn_ref[...] = (h1 * _row(p10_ref, P_NORM2)).astype(_bf16)
    r2_ref[...] = jnp.broadcast_to(1.0 / (jnp.mean(h1 * h1, axis=-1, keepdims=True) + EPS), r2_ref.shape)
    for c in range(d_ff // FF_CHUNK):
        if c:
            yield
        cols = slice(c * FF_CHUNK, (c + 1) * FF_CHUNK)
        up = jnp.dot(hn_ref[...], wup_ref[:, cols], preferred_element_type=_f32)
        hid_ref[:, cols] = jnp.square(jnp.maximum(up.astype(_bf16), 0.0))


_PIPELINE_ORDER = "PPUUUUUP"
_PIPELINE_DEPTH = 2


def _pipelined_body(x_ref, xres_ref, p5_ref, p10_ref, win_hbm, ws_ref, bmix_ref, wgate_ref,
                    wout_hbm, wup_hbm, wdn_hbm, y_ref, stc_ref, sth_ref,
                    xn_ref, z_ref, xpad_ref, gu_ref, gg_ref, mixed_ref, xc_ref, pre_ref, hcar_ref, cat_ref,
                    h1_ref, hn_ref, hid_ref, r2_ref, wmixm_ref, win_ref, wout_ref, wup_ref, wdn_ref,
                    dma_sem, *, tm, chunk, n_tiles, tiles_per_seq, final_norm):
    s = pl.program_id(0)
    t_proj = jnp.minimum(s, n_tiles - 1) % tiles_per_seq
    t_gate = jnp.clip(s - 1, 0, n_tiles - 1) % tiles_per_seq

    @pl.when(s == 0)
    def _():
        for ref in (h1_ref, hid_ref, r2_ref):
            ref[...] = jnp.zeros_like(ref)
        slots = [z_ref.at[r0:r0 + STAGE_ROWS, c0:c0 + STAGE_COLS]
                 for c0 in range(0, z_ref.shape[1], STAGE_COLS) for r0 in range(0, tm, STAGE_ROWS)]
        _load_weights_bf16(((win_hbm, win_ref), (wout_hbm, wout_ref), (wup_hbm, wup_ref), (wdn_hbm, wdn_ref)),
                           slots[:STAGE_SLOTS], dma_sem)

    @pl.when(t_proj == 0)
    def _():
        _init_state(ws_ref, None, None, xpad_ref, hcar_ref, wmixm_ref, chunk=chunk, init_gate=False)

    @pl.when(t_gate == 0)
    def _():
        _init_state(ws_ref, None, None, xpad_ref, hcar_ref, wmixm_ref, chunk=chunk, init_proj=False)

    def run_stages(with_proj, with_gate_up, with_down):
        never = s < 0 if with_proj or with_gate_up else None
        up = proj = iter(())
        if with_gate_up:
            _gate_stage(p5_ref, bmix_ref, gu_ref, gg_ref, mixed_ref, xc_ref, pre_ref, hcar_ref, cat_ref,
                        tm=tm, chunk=chunk, seg=tm, first_tile=(t_gate == 0), never=never)
        if with_down:
            _down_stage(h1_ref, hid_ref, r2_ref, p10_ref, wdn_ref, y_ref, final_norm=final_norm, never=never)
        if with_gate_up:
            up = _up_phases(xres_ref, cat_ref, p10_ref, wout_ref, wup_ref, h1_ref, hn_ref, hid_ref, r2_ref,
                            anchor=(never, y_ref) if with_down else None)
        if with_proj:
            proj = _proj_phases(x_ref, p5_ref, p10_ref, win_ref, wgate_ref, None, wmixm_ref,
                                xn_ref, z_ref, xpad_ref, gu_ref, gg_ref, mixed_ref, xc_ref, pre_ref,
                                tm=tm, chunk=chunk, seg=tm, never=never)
        for who in _PIPELINE_ORDER:
            next(proj if who == "P" else up, None)
        assert next(proj, "done") == "done" and next(up, "done") == "done"

    filling, last = s == 0, s == n_tiles + 1
    pl.when(filling)(functools.partial(run_stages, True, False, False))
    pl.when(last)(functools.partial(run_stages, False, False, True))
    pl.when(~(filling | last))(functools.partial(run_stages, True, True, True))

    @pl.when((t_proj == tiles_per_seq - 1) & (s < n_tiles))
    def _():
        _write_conv_state(stc_ref, xpad_ref)

    @pl.when((t_gate == tiles_per_seq - 1) & (s >= 1) & (s <= n_tiles))
    def _():
        _write_h_state(sth_ref, hcar_ref)


def _single_body(x_ref, conv0_ref, h0_ref, p5_ref, p10_ref, win_hbm, ws_ref, bmix_ref, wgate_ref,
                 wout_hbm, wup_hbm, wdn_hbm, y_ref, stc_ref, sth_ref, v_ref,
                 xn_ref, z_ref, xpad_ref, gu_ref, gg_ref, mixed_ref, xc_ref, pre_ref, hcar_ref, cat_ref,
                 h1_ref, hn_ref, hid_ref, r2_ref, wmixm_ref, win_ref, wout_ref, wup_ref, wdn_ref,
                 stage_ref, dma_sem, *, tm, chunk, seg, final_norm):
    _load_weights_bf16(((win_hbm, win_ref), (wout_hbm, wout_ref), (wup_hbm, wup_ref), (wdn_hbm, wdn_ref)),
                       [stage_ref.at[k] for k in range(STAGE_SLOTS)], dma_sem)
    _init_state(ws_ref, conv0_ref, h0_ref, xpad_ref, hcar_ref, wmixm_ref, chunk=chunk)
    for _ in _proj_phases(x_ref, p5_ref, p10_ref, win_ref, wgate_ref, v_ref, wmixm_ref,
                          xn_ref, z_ref, xpad_ref, gu_ref, gg_ref, mixed_ref, xc_ref, pre_ref,
                          tm=tm, chunk=chunk, seg=seg):
        pass
    _gate_stage(p5_ref, bmix_ref, gu_ref, gg_ref, mixed_ref, xc_ref, pre_ref, hcar_ref, cat_ref,
                tm=tm, chunk=chunk, seg=seg, first_tile=None)
    for _ in _up_phases(x_ref, cat_ref, p10_ref, wout_ref, wup_ref, h1_ref, hn_ref, hid_ref, r2_ref):
        pass
    _down_stage(h1_ref, hid_ref, r2_ref, p10_ref, wdn_ref, y_ref, final_norm=final_norm)
    _write_conv_state(stc_ref, xpad_ref)
    _write_h_state(sth_ref, hcar_ref)


def _resident(shape):
    nd = len(shape)
    return pl.BlockSpec(shape, lambda i: (0,) * nd, pipeline_mode=pl.Buffered(1))


def _scratch(tm, nseg, seg, d_model, d_ff, w_a, w_b, wmix_shape):
    return [
        pltpu.VMEM((tm, d_model), _bf16),
        pltpu.VMEM((tm, 2 * w_a + 2 * w_b), _f32),
        pltpu.VMEM((nseg, seg + SUBLANES, w_b), _f32),
        pltpu.VMEM((tm, w_a), _f32),
        pltpu.VMEM((tm, w_b), _f32),
        pltpu.VMEM((tm, w_a), _f32),
        pltpu.VMEM((tm, w_b), _f32),
        pltpu.VMEM((tm, 2 * w_b), _f32),
        pltpu.VMEM((nseg, SUBLANES, w_b), _f32),
        pltpu.VMEM((tm, w_a + w_b), _bf16),
        pltpu.VMEM((tm, d_model), _f32),
        pltpu.VMEM((tm, d_model), _bf16),
        pltpu.VMEM((tm, d_ff), _bf16),
        pltpu.VMEM((tm, 128), _f32),
        pltpu.VMEM(wmix_shape, _bf16),
    ]


def _weight_scratch(*big_weights, own_staging):
    staging = [pltpu.VMEM((STAGE_SLOTS, STAGE_ROWS, STAGE_COLS), _f32)] if own_staging else []
    return [pltpu.VMEM(w.shape, _bf16) for w in big_weights] + staging + [pltpu.SemaphoreType.DMA((STAGE_SLOTS,))]


def _mix_scratch_shape(chunk, w_a):
    heads_per_tile = MXU_DIM // (w_a // H_A)
    return (H_A // heads_per_tile, chunk, heads_per_tile * chunk)


def _run_prompt_layer(x, weights, *, tm, final_norm):
    p5, p10, win, ws, bmix, wgate, wout, wup, wdn = weights
    nb, T, d_model = x.shape
    w_b = p5.shape[-1]
    w_a = bmix.shape[-1]
    tiles_per_seq = T // tm
    n_tiles = nb * tiles_per_seq
    x2 = x.reshape(nb * T, d_model)
    body = functools.partial(_pipelined_body, tm=tm, chunk=GMLP_CHUNK, n_tiles=n_tiles,
                             tiles_per_seq=tiles_per_seq, final_norm=final_norm)
    tile = lambda lag: (lambda s: (jnp.clip(s - lag, 0, n_tiles - 1), 0))
    seq = lambda lag: (lambda s: (jnp.clip(s - lag, 0, n_tiles - 1) // tiles_per_seq, 0, 0))
    in_hbm = pl.BlockSpec(memory_space=pl.ANY)
    in_specs = [
        pl.BlockSpec((tm, d_model), tile(0)),
        pl.BlockSpec((tm, d_model), tile(1)),
        _resident(p5.shape), _resident(p10.shape), in_hbm, _resident(ws.shape), _resident(bmix.shape),
        _resident(wgate.shape), in_hbm, in_hbm, in_hbm,
    ]
    y, stc, sth = pl.pallas_call(
        body,
        grid=(n_tiles + _PIPELINE_DEPTH,),
        in_specs=in_specs,
        out_specs=[pl.BlockSpec((tm, d_model), tile(_PIPELINE_DEPTH)),
                   pl.BlockSpec((1, CONV_W - 1, w_b), seq(0)), pl.BlockSpec((1, 1, w_b), seq(1))],
        out_shape=[jax.ShapeDtypeStruct((nb * T, d_model), _f32),
                   jax.ShapeDtypeStruct((nb, CONV_W - 1, w_b), _f32),
                   jax.ShapeDtypeStruct((nb, 1, w_b), _f32)],
        scratch_shapes=_scratch(tm, 1, tm, d_model, wup.shape[-1], w_a, w_b, _mix_scratch_shape(GMLP_CHUNK, w_a))
        + _weight_scratch(win, wout, wup, wdn, own_staging=False),
        compiler_params=pltpu.CompilerParams(dimension_semantics=("arbitrary",),
                                             vmem_limit_bytes=VMEM_LIMIT_BYTES),
        name="prompt_layer",
    )(x2, x2, *weights)
    return y.reshape(nb, T, d_model), stc, sth


def _run_sample_layer(x, conv0, h0, weights, *, chunk, final_norm):
    p5, p10, win, ws, bmix, wgate, wout, wup, wdn = weights
    ns, ts, d_model = x.shape
    w_b = p5.shape[-1]
    w_a = bmix.shape[-1]
    tm = ns * ts
    body = functools.partial(_single_body, tm=tm, chunk=chunk, seg=ts, final_norm=final_norm)
    whole = lambda shape: pl.BlockSpec(shape, lambda i: (0,) * len(shape))
    in_hbm = pl.BlockSpec(memory_space=pl.ANY)
    y, stc, sth, v = pl.pallas_call(
        body,
        grid=(1,),
        in_specs=[whole((tm, d_model)), whole(conv0.shape), whole(h0.shape),
                  _resident(p5.shape), _resident(p10.shape), in_hbm, _resident(ws.shape), _resident(bmix.shape),
                  _resident(wgate.shape), in_hbm, in_hbm, in_hbm],
        out_specs=[whole((tm, d_model)), whole((ns, CONV_W - 1, w_b)), whole((ns, 1, w_b)), whole((tm, w_a))],
        out_shape=[jax.ShapeDtypeStruct((tm, d_model), _f32),
                   jax.ShapeDtypeStruct((ns, CONV_W - 1, w_b), _f32),
                   jax.ShapeDtypeStruct((ns, 1, w_b), _f32),
                   jax.ShapeDtypeStruct((tm, w_a), _f32)],
        scratch_shapes=_scratch(tm, ns, ts, d_model, wup.shape[-1], w_a, w_b, _mix_scratch_shape(chunk, w_a))
        + _weight_scratch(win, wout, wup, wdn, own_staging=True),
        compiler_params=pltpu.CompilerParams(dimension_semantics=("arbitrary",),
                                             vmem_limit_bytes=VMEM_LIMIT_BYTES),
        name="sample_layer",
    )(x.reshape(tm, d_model), conv0, h0, *weights)
    return y.reshape(ns, ts, d_model), stc, sth, v.reshape(ns, ts, w_a)


def _block_diag(w):
    n, k, _ = w.shape
    eye = jnp.eye(n, dtype=w.dtype)
    return (eye[:, None, :, None] * w[:, :, None, :]).reshape(n * k, n * k)


def _prep_weights(l, chunk, norm1_g, w_in, ln_v_g, ln_v_b, w_s, b_s, conv_w, conv_b, w_a, b_a, w_x, b_x,
                  lam, gn_a_g, gn_b_g, w_out, norm2_g, w_up, w_down, normf_g):
    w_b = conv_b.shape[-1]
    d_model = norm1_g.shape[-1]
    hd_a = gn_a_g.shape[-1] // H_A
    rows5 = [ln_v_g[l], ln_v_b[l], conv_w[l, 0], conv_w[l, 1], conv_w[l, 2], conv_w[l, 3], conv_b[l],
             b_a[l], b_x[l], lam[l], gn_a_g[l], gn_b_g[l]]
    p5 = jnp.stack(rows5 + [jnp.zeros((w_b,), _f32)] * (16 - len(rows5)))
    p10 = jnp.stack([norm1_g[l], norm2_g[l], normf_g] + [jnp.zeros((d_model,), _f32)] * 5)
    bmix = jnp.repeat(b_s[l][:, :chunk].T, hd_a, axis=1)
    hh = H_B // 2
    wgate = jnp.stack([jnp.concatenate([_block_diag(w_a[l, j * hh:(j + 1) * hh]),
                                        _block_diag(w_x[l, j * hh:(j + 1) * hh])], axis=1)
                       for j in range(2)]).astype(_bf16)
    return (p5, p10, w_in[l], w_s[l], bmix, wgate, w_out[l], w_up[l], w_down[l])


def kernel(x_prompt, x_sample, state_conv_b, state_h_b, norm1_g, w_in, ln_v_g, ln_v_b, w_s, b_s, conv_w, conv_b,
           w_a, b_a, w_x, b_x, lam, gn_a_g, gn_b_g, w_out, norm2_g, w_up, w_down, normf_g):
    depth = w_in.shape[0]
    nb = x_prompt.shape[0]
    ns, ts, _ = x_sample.shape
    w_b = conv_b.shape[-1]
    params = (norm1_g, w_in, ln_v_g, ln_v_b, w_s, b_s, conv_w, conv_b, w_a, b_a, w_x, b_x, lam, gn_a_g, gn_b_g,
              w_out, norm2_g, w_up, w_down, normf_g)
    chunk_s = GMLP_CHUNK if ts % GMLP_CHUNK == 0 else ts
    hp, hs = x_prompt, x_sample
    conv_p, hlast_p, conv_s, hlast_s, v_s = [], [], [], [], []
    for l in range(depth):
        last = l == depth - 1
        hp, stc, sth = _run_prompt_layer(hp, _prep_weights(l, GMLP_CHUNK, *params), tm=PROMPT_TM, final_norm=last)
        conv_p.append(stc)
        hlast_p.append(sth.reshape(nb, w_b))
        hs, stc, sth, vv = _run_sample_layer(hs, state_conv_b[l], state_h_b[l], _prep_weights(l, chunk_s, *params),
                                             chunk=chunk_s, final_norm=last)
        conv_s.append(stc)
        hlast_s.append(sth.reshape(ns, w_b))
        v_s.append(vv)
    return (hp, hs, jnp.stack(conv_p), jnp.stack(hlast_p), jnp.stack(conv_s), jnp.stack(hlast_s), jnp.stack(v_s))
```

```python
import functools

import jax
import jax.numpy as jnp
from jax import lax
from jax.experimental import pallas as pl
from jax.experimental.pallas import tpu as pltpu

H_A = 8
H_B = 8
GMLP_CHUNK = 128
CONV_W = 4
LRU_C = 8.0
EPS = 1e-6

SUBLANES = 8
PACK_ROWS = 16
MXU_DIM = 256
FF_CHUNK = 1024
LN_CHAINS = 2
PROMPT_TM = 512
STAGE_SLOTS, STAGE_ROWS, STAGE_COLS = 4, 256, 1024
SAMPLE_STAGE_SLOTS = 16
VMEM_LIMIT_BYTES = 62 * 1024 * 1024

P_LNV_G, P_LNV_B, P_CONV_W, P_CONV_B, P_BA, P_BX, P_LAM, P_GNA, P_GNB = 0, 1, 2, 6, 7, 8, 9, 10, 11
P_NORM1, P_NORM2, P_NORMF = 0, 1, 2

_f32 = jnp.float32
_bf16 = jnp.bfloat16


def _rms(x, g):
    return x * lax.rsqrt(jnp.mean(x * x, axis=-1, keepdims=True) + EPS) * g


def _layernorm(x, g, b):
    mu = jnp.mean(x, axis=-1, keepdims=True)
    xc = x - mu
    var = jnp.mean(xc * xc, axis=-1, keepdims=True)
    return xc * lax.rsqrt(var + EPS) * g + b


def _sigmoid(x):
    return 0.5 * (1.0 + jnp.tanh(0.5 * x))


def _row(ref, r):
    return ref[r:r + 1, :]


def _load_weights_bf16(pairs, slots, sem):
    n_slots = len(slots)
    stage_rows, stage_cols = slots[0].shape
    chunks = [(src, dst, r0, c0) for src, dst in pairs
              for c0 in range(0, src.shape[1], stage_cols) for r0 in range(0, src.shape[0], stage_rows)]

    def chunk_copy(k):
        src, _, r0, c0 = chunks[k]
        return pltpu.make_async_copy(src.at[r0:r0 + stage_rows, c0:c0 + stage_cols],
                                     slots[k % n_slots], sem.at[k % n_slots])

    lookahead = n_slots - 1
    for k in range(min(lookahead, len(chunks))):
        chunk_copy(k).start()
    for k, (_, dst, r0, c0) in enumerate(chunks):
        if k + lookahead < len(chunks):
            chunk_copy(k + lookahead).start()
        chunk_copy(k).wait()
        dst[r0:r0 + stage_rows, c0:c0 + stage_cols] = slots[k % n_slots][...].astype(_bf16)
        if k + 1 == len(chunks) or chunks[k + 1][1] is not dst:
            yield


def _init_state(ws_ref, conv0_ref, h0_ref, xpad_ref, hcar_ref, wmixm_ref, *, chunk, init_proj=True, init_gate=True):
    if init_proj:
        n_tiles_a, _, kcat = wmixm_ref.shape
        per_tile = kcat // chunk
        row = lax.broadcasted_iota(jnp.int32, (chunk, chunk), 0)
        col = lax.broadcasted_iota(jnp.int32, (chunk, chunk), 1)
        for q in range(n_tiles_a):
            wmixm_ref[q] = jnp.concatenate(
                [jnp.where(col <= row, ws_ref[q * per_tile + j, 0:chunk, 0:chunk], 0.0) for j in range(per_tile)],
                axis=1).astype(_bf16)
        xpad_ref[:, 0:SUBLANES, :] = jnp.zeros_like(xpad_ref[:, 0:SUBLANES, :])
        if conv0_ref is not None:
            xpad_ref[:, SUBLANES - (CONV_W - 1):SUBLANES, :] = conv0_ref[...]
    if init_gate:
        if h0_ref is None:
            hcar_ref[...] = jnp.zeros_like(hcar_ref)
        else:
            hcar_ref[...] = jnp.broadcast_to(h0_ref[...][:, None, :], hcar_ref.shape)


def _write_conv_state(stc_ref, xpad_ref):
    stc_ref[...] = xpad_ref[:, SUBLANES - (CONV_W - 1):SUBLANES, :]


def _write_h_state(sth_ref, hcar_ref):
    sth_ref[...] = hcar_ref[:, SUBLANES - 1:SUBLANES, :]


def _proj_phases(x_ref, p5_ref, p10_ref, win_ref, wgate_ref, v_ref, wmixm_ref,
                 xn_ref, z_ref, xpad_ref, gu_ref, gg_ref, mixed_ref, xc_ref, pre_ref,
                 *, tm, chunk, seg, never=None):
    nseg = tm // seg
    nchunk = tm // chunk
    w_a = gu_ref.shape[-1]
    w_b = p5_ref.shape[-1]
    hd_a = w_a // H_A
    heads_per_tile = MXU_DIM // hd_a
    n_tiles_a = w_a // MXU_DIM

    x = x_ref[...]
    xn_ref[...] = (x * _row(p10_ref, P_NORM1)).astype(_bf16)
    r1 = lax.rsqrt(jnp.mean(x * x, axis=-1, keepdims=True) + EPS)
    d_mix = w_a + w_b
    z_ref[:, 0:d_mix] = jnp.dot(xn_ref[...], win_ref[:, 0:d_mix], preferred_element_type=_f32) * r1
    yield
    z_ref[:, d_mix:2 * d_mix] = jnp.dot(xn_ref[...], win_ref[:, d_mix:2 * d_mix], preferred_element_type=_f32) * r1
    yield

    gu_ref[...] = z_ref[:, 0:w_a]
    gg_ref[...] = z_ref[:, 2 * w_a + w_b:2 * w_a + 2 * w_b]

    v_blocks = []
    for r0 in range(0, tm, PACK_ROWS):
        zv = z_ref[r0:r0 + PACK_ROWS, w_a:2 * w_a]
        if never is not None and len(v_blocks) >= LN_CHAINS:
            zv = jnp.where(never, v_blocks[-LN_CHAINS], zv)
        v_blocks.append(_layernorm(jax.nn.gelu(zv), _row(p5_ref, P_LNV_G), _row(p5_ref, P_LNV_B)))
    v = jnp.concatenate(v_blocks, axis=0)
    if v_ref is not None:
        v_ref[...] = v
    lane = lax.broadcasted_iota(jnp.int32, (chunk, MXU_DIM), 1)
    for c in range(nchunk):
        rows = slice(c * chunk, (c + 1) * chunk)
        for q in range(n_tiles_a):
            cols = slice(q * MXU_DIM, (q + 1) * MXU_DIM)
            vq = v[rows, cols]
            rhs = jnp.concatenate(
                [jnp.where((lane >= hd_a * j) & (lane < hd_a * (j + 1)), vq, 0.0)
                 for j in range(heads_per_tile)], axis=0).astype(_bf16)
            mixed_ref[rows, cols] = jnp.dot(wmixm_ref[q], rhs, preferred_element_type=_f32)

    for s in range(nseg):
        xpad_ref[s, SUBLANES:SUBLANES + seg, :] = z_ref[s * seg:(s + 1) * seg, 2 * w_a:2 * w_a + w_b]
    xcs = []
    for s in range(nseg):
        acc = _row(p5_ref, P_CONV_B)
        for k in range(CONV_W):
            off = SUBLANES - (CONV_W - 1) + k
            acc = acc + xpad_ref[s, off:off + seg, :] * _row(p5_ref, P_CONV_W + k)
        xcs.append(acc)
        xpad_ref[s, 0:SUBLANES, :] = xpad_ref[s, seg:seg + SUBLANES, :]
    xc = jnp.concatenate(xcs, axis=0) if nseg > 1 else xcs[0]
    xc_ref[...] = xc
    xcb = xc.astype(_bf16)
    half = w_b // 2
    for j in range(2):
        res = jnp.dot(xcb[:, j * half:(j + 1) * half], wgate_ref[j], preferred_element_type=_f32)
        pre_ref[:, j * half:(j + 1) * half] = res[:, :half]
        pre_ref[:, w_b + j * half:w_b + (j + 1) * half] = res[:, half:]


def _gate_stage(p5_ref, bmix_ref, gu_ref, gg_ref, mixed_ref, xc_ref, pre_ref, hcar_ref, cat_ref,
                *, tm, chunk, seg, first_tile, never=None):
    nseg = tm // seg
    w_a = gu_ref.shape[-1]
    w_b = p5_ref.shape[-1]

    rowid = lax.broadcasted_iota(jnp.int32, (SUBLANES, w_b), 0)
    for s in range(nseg):
        hp = hcar_ref[s]
        link = None
        for r0 in range(s * seg, (s + 1) * seg, PACK_ROWS):
            ybs = []
            for g0 in range(r0, r0 + PACK_ROWS, SUBLANES):
                rows = slice(g0, g0 + SUBLANES)
                pre_r = pre_ref[rows, 0:w_b]
                pre_i = pre_ref[rows, w_b:2 * w_b]
                g = gg_ref[rows, :]
                if never is not None and link is not None:
                    pre_r = jnp.where(never, link[0], pre_r)
                    pre_i = jnp.where(never, link[1], pre_i)
                    g = jnp.where(never, link[0], g)
                r = _sigmoid(pre_r + _row(p5_ref, P_BA))
                i = _sigmoid(pre_i + _row(p5_ref, P_BX))
                log_a = -LRU_C * r * jax.nn.softplus(-_row(p5_ref, P_LAM))
                ag = jnp.exp(log_a)
                th = jnp.tanh(log_a)
                n = -2.0 * th
                mult = jnp.where(n > 0.0, n * lax.rsqrt(n * (1.0 - th)), 0.0)
                if first_tile is not None and g0 == s * seg:
                    mult = jnp.where((rowid == 0) & first_tile, 1.0, mult)
                bg = mult * (i * xc_ref[rows, :])
                for d in (1, 2, 4):
                    keep = rowid >= d
                    a_sh = jnp.where(keep, pltpu.roll(ag, d, 0), 1.0)
                    b_sh = jnp.where(keep, pltpu.roll(bg, d, 0), 0.0)
                    bg = bg + ag * b_sh
                    ag = ag * a_sh
                hg = bg + ag * hp
                hp = jnp.broadcast_to(hg[SUBLANES - 1:SUBLANES, :], (SUBLANES, w_b))
                link = (hp, hp)
                ybs.append(hg * jax.nn.gelu(g))
            rows = slice(r0, r0 + PACK_ROWS)
            yb = _rms(jnp.concatenate(ybs, axis=0), _row(p5_ref, P_GNB))
            cat_ref[rows, w_a:w_a + w_b] = yb.astype(_bf16)
            u = gu_ref[rows, :]
            if never is not None:
                u = jnp.where(never, yb, u)
            ya = jax.nn.gelu(u) * (mixed_ref[rows, :] + bmix_ref[r0 % chunk:r0 % chunk + PACK_ROWS, :])
            link = (ya[0:SUBLANES], ya[SUBLANES:PACK_ROWS])
            cat_ref[rows, 0:w_a] = _rms(ya, _row(p5_ref, P_GNA)).astype(_bf16)
        hcar_ref[s] = hp


def _down_stage(h1_ref, hid_ref, r2_ref, p10_ref, wdn_ref, y_ref, *, final_norm, never=None):
    down = jnp.dot(hid_ref[...], wdn_ref[...], preferred_element_type=_f32)
    tm = down.shape[0]
    prev = None
    for r0 in range(0, tm, PACK_ROWS):
        rows = slice(r0, r0 + PACK_ROWS)
        out = h1_ref[rows, :] + down[rows, :] * r2_ref[rows, 0:1]
        if never is not None and prev is not None:
            out = jnp.where(never, prev, out)
        prev = _rms(out, _row(p10_ref, P_NORMF)) if final_norm else out
        y_ref[rows, :] = prev


def _up_phases(x_ref, cat_ref, p10_ref, wout_ref, wup_ref, h1_ref, hn_ref, hid_ref, r2_ref, anchor=None):
    d_ff = wup_ref.shape[-1]
    x = x_ref[...]
    if anchor is not None:
        never, anchored_ref = anchor
        x = jnp.where(never, anchored_ref[...], x)
    h1_ref[...] = x + jnp.dot(cat_ref[...], wout_ref[...], preferred_element_type=_f32)
    yield
    h1 = h1_ref[...]
    hn_ref[...] = (h1 * _row(p10_ref, P_NORM2)).astype(_bf16)
    r2_ref[...] = jnp.broadcast_to(1.0 / (jnp.mean(h1 * h1, axis=-1, keepdims=True) + EPS), r2_ref.shape)
    for c in range(d_ff // FF_CHUNK):
        if c:
            yield
        cols = slice(c * FF_CHUNK, (c + 1) * FF_CHUNK)
        up = jnp.dot(hn_ref[...], wup_ref[:, cols], preferred_element_type=_f32)
        hid_ref[:, cols] = jnp.square(jnp.maximum(up.astype(_bf16), 0.0))


_PIPELINE_ORDER = "PPUUUUUP"
_PIPELINE_DEPTH = 2


def _pipelined_body(x_ref, xres_ref, p5_ref, p10_ref, win_hbm, ws_ref, bmix_ref, wgate_ref,
                    wout_hbm, wup_hbm, wdn_hbm, y_ref, stc_ref, sth_ref,
                    xn_ref, z_ref, xpad_ref, gu_ref, gg_ref, mixed_ref, xc_ref, pre_ref, hcar_ref, cat_ref,
                    h1_ref, hn_ref, hid_ref, r2_ref, wmixm_ref, win_ref, wout_ref, wup_ref, wdn_ref,
                    dma_sem, *, tm, chunk, n_tiles, tiles_per_seq, final_norm):
    s = pl.program_id(0)
    t_proj = jnp.minimum(s, n_tiles - 1) % tiles_per_seq
    t_gate = jnp.clip(s - 1, 0, n_tiles - 1) % tiles_per_seq

    @pl.when(s == 0)
    def _():
        for ref in (h1_ref, hid_ref, r2_ref):
            ref[...] = jnp.zeros_like(ref)
        slots = [z_ref.at[r0:r0 + STAGE_ROWS, c0:c0 + STAGE_COLS]
                 for c0 in range(0, z_ref.shape[1], STAGE_COLS) for r0 in range(0, tm, STAGE_ROWS)]
        for _ in _load_weights_bf16(((win_hbm, win_ref), (wout_hbm, wout_ref), (wup_hbm, wup_ref),
                                     (wdn_hbm, wdn_ref)), slots[:STAGE_SLOTS], dma_sem):
            pass

    @pl.when(t_proj == 0)
    def _():
        _init_state(ws_ref, None, None, xpad_ref, hcar_ref, wmixm_ref, chunk=chunk, init_gate=False)

    @pl.when(t_gate == 0)
    def _():
        _init_state(ws_ref, None, None, xpad_ref, hcar_ref, wmixm_ref, chunk=chunk, init_proj=False)

    def run_stages(with_proj, with_gate_up, with_down):
        never = s < 0 if with_proj or with_gate_up else None
        up = proj = iter(())
        if with_gate_up:
            _gate_stage(p5_ref, bmix_ref, gu_ref, gg_ref, mixed_ref, xc_ref, pre_ref, hcar_ref, cat_ref,
                        tm=tm, chunk=chunk, seg=tm, first_tile=(t_gate == 0), never=never)
        if with_down:
            _down_stage(h1_ref, hid_ref, r2_ref, p10_ref, wdn_ref, y_ref, final_norm=final_norm, never=never)
        if with_gate_up:
            up = _up_phases(xres_ref, cat_ref, p10_ref, wout_ref, wup_ref, h1_ref, hn_ref, hid_ref, r2_ref,
                            anchor=(never, y_ref) if with_down else None)
        if with_proj:
            proj = _proj_phases(x_ref, p5_ref, p10_ref, win_ref, wgate_ref, None, wmixm_ref,
                                xn_ref, z_ref, xpad_ref, gu_ref, gg_ref, mixed_ref, xc_ref, pre_ref,
                                tm=tm, chunk=chunk, seg=tm, never=never)
        for who in _PIPELINE_ORDER:
            next(proj if who == "P" else up, None)
        assert next(proj, "done") == "done" and next(up, "done") == "done"

    filling, last = s == 0, s == n_tiles + 1
    pl.when(filling)(functools.partial(run_stages, True, False, False))
    pl.when(last)(functools.partial(run_stages, False, False, True))
    pl.when(~(filling | last))(functools.partial(run_stages, True, True, True))

    @pl.when((t_proj == tiles_per_seq - 1) & (s < n_tiles))
    def _():
        _write_conv_state(stc_ref, xpad_ref)

    @pl.when((t_gate == tiles_per_seq - 1) & (s >= 1) & (s <= n_tiles))
    def _():
        _write_h_state(sth_ref, hcar_ref)


def _single_body(x_ref, conv0_ref, h0_ref, p5_ref, p10_ref, win_hbm, ws_ref, bmix_ref, wgate_ref,
                 wout_hbm, wup_hbm, wdn_hbm, y_ref, stc_ref, sth_ref, v_ref,
                 xn_ref, z_ref, xpad_ref, gu_ref, gg_ref, mixed_ref, xc_ref, pre_ref, hcar_ref, cat_ref,
                 h1_ref, hn_ref, hid_ref, r2_ref, wmixm_ref, win_ref, wout_ref, wup_ref, wdn_ref,
                 stage_ref, dma_sem, *, tm, chunk, seg, final_norm):
    weights = _load_weights_bf16(((win_hbm, win_ref), (wout_hbm, wout_ref), (wup_hbm, wup_ref), (wdn_hbm, wdn_ref)),
                                 [stage_ref.at[k] for k in range(stage_ref.shape[0])], dma_sem)
    next(weights)
    _init_state(ws_ref, conv0_ref, h0_ref, xpad_ref, hcar_ref, wmixm_ref, chunk=chunk)
    for _ in _proj_phases(x_ref, p5_ref, p10_ref, win_ref, wgate_ref, v_ref, wmixm_ref,
                          xn_ref, z_ref, xpad_ref, gu_ref, gg_ref, mixed_ref, xc_ref, pre_ref,
                          tm=tm, chunk=chunk, seg=seg):
        pass
    _gate_stage(p5_ref, bmix_ref, gu_ref, gg_ref, mixed_ref, xc_ref, pre_ref, hcar_ref, cat_ref,
                tm=tm, chunk=chunk, seg=seg, first_tile=None)
    next(weights)
    up = _up_phases(x_ref, cat_ref, p10_ref, wout_ref, wup_ref, h1_ref, hn_ref, hid_ref, r2_ref)
    next(up)
    next(weights)
    for _ in up:
        pass
    next(weights)
    _down_stage(h1_ref, hid_ref, r2_ref, p10_ref, wdn_ref, y_ref, final_norm=final_norm)
    _write_conv_state(stc_ref, xpad_ref)
    _write_h_state(sth_ref, hcar_ref)


def _resident(shape):
    nd = len(shape)
    return pl.BlockSpec(shape, lambda i: (0,) * nd, pipeline_mode=pl.Buffered(1))


def _scratch(tm, nseg, seg, d_model, d_ff, w_a, w_b, wmix_shape):
    return [
        pltpu.VMEM((tm, d_model), _bf16),
        pltpu.VMEM((tm, 2 * w_a + 2 * w_b), _f32),
        pltpu.VMEM((nseg, seg + SUBLANES, w_b), _f32),
        pltpu.VMEM((tm, w_a), _f32),
        pltpu.VMEM((tm, w_b), _f32),
        pltpu.VMEM((tm, w_a), _f32),
        pltpu.VMEM((tm, w_b), _f32),
        pltpu.VMEM((tm, 2 * w_b), _f32),
        pltpu.VMEM((nseg, SUBLANES, w_b), _f32),
        pltpu.VMEM((tm, w_a + w_b), _bf16),
        pltpu.VMEM((tm, d_model), _f32),
        pltpu.VMEM((tm, d_model), _bf16),
        pltpu.VMEM((tm, d_ff), _bf16),
        pltpu.VMEM((tm, 128), _f32),
        pltpu.VMEM(wmix_shape, _bf16),
    ]


def _weight_scratch(*big_weights, n_slots, own_staging):
    staging = [pltpu.VMEM((n_slots, STAGE_ROWS, STAGE_COLS), _f32)] if own_staging else []
    return [pltpu.VMEM(w.shape, _bf16) for w in big_weights] + staging + [pltpu.SemaphoreType.DMA((n_slots,))]


def _mix_scratch_shape(chunk, w_a):
    heads_per_tile = MXU_DIM // (w_a // H_A)
    return (H_A // heads_per_tile, chunk, heads_per_tile * chunk)


def _run_prompt_layer(x, weights, *, tm, final_norm):
    p5, p10, win, ws, bmix, wgate, wout, wup, wdn = weights
    nb, T, d_model = x.shape
    w_b = p5.shape[-1]
    w_a = bmix.shape[-1]
    tiles_per_seq = T // tm
    n_tiles = nb * tiles_per_seq
    x2 = x.reshape(nb * T, d_model)
    body = functools.partial(_pipelined_body, tm=tm, chunk=GMLP_CHUNK, n_tiles=n_tiles,
                             tiles_per_seq=tiles_per_seq, final_norm=final_norm)
    tile = lambda lag: (lambda s: (jnp.clip(s - lag, 0, n_tiles - 1), 0))
    seq = lambda lag: (lambda s: (jnp.clip(s - lag, 0, n_tiles - 1) // tiles_per_seq, 0, 0))
    in_hbm = pl.BlockSpec(memory_space=pl.ANY)
    in_specs = [
        pl.BlockSpec((tm, d_model), tile(0)),
        pl.BlockSpec((tm, d_model), tile(1)),
        _resident(p5.shape), _resident(p10.shape), in_hbm, _resident(ws.shape), _resident(bmix.shape),
        _resident(wgate.shape), in_hbm, in_hbm, in_hbm,
    ]
    y, stc, sth = pl.pallas_call(
        body,
        grid=(n_tiles + _PIPELINE_DEPTH,),
        in_specs=in_specs,
        out_specs=[pl.BlockSpec((tm, d_model), tile(_PIPELINE_DEPTH)),
                   pl.BlockSpec((1, CONV_W - 1, w_b), seq(0)), pl.BlockSpec((1, 1, w_b), seq(1))],
        out_shape=[jax.ShapeDtypeStruct((nb * T, d_model), _f32),
                   jax.ShapeDtypeStruct((nb, CONV_W - 1, w_b), _f32),
                   jax.ShapeDtypeStruct((nb, 1, w_b), _f32)],
        scratch_shapes=_scratch(tm, 1, tm, d_model, wup.shape[-1], w_a, w_b, _mix_scratch_shape(GMLP_CHUNK, w_a))
        + _weight_scratch(win, wout, wup, wdn, n_slots=STAGE_SLOTS, own_staging=False),
        compiler_params=pltpu.CompilerParams(dimension_semantics=("arbitrary",),
                                             vmem_limit_bytes=VMEM_LIMIT_BYTES),
        name="prompt_layer",
    )(x2, x2, *weights)
    return y.reshape(nb, T, d_model), stc, sth


def _run_sample_layer(x, conv0, h0, weights, *, chunk, final_norm):
    p5, p10, win, ws, bmix, wgate, wout, wup, wdn = weights
    ns, ts, d_model = x.shape
    w_b = p5.shape[-1]
    w_a = bmix.shape[-1]
    tm = ns * ts
    body = functools.partial(_single_body, tm=tm, chunk=chunk, seg=ts, final_norm=final_norm)
    whole = lambda shape: pl.BlockSpec(shape, lambda i: (0,) * len(shape))
    in_hbm = pl.BlockSpec(memory_space=pl.ANY)
    y, stc, sth, v = pl.pallas_call(
        body,
        grid=(1,),
        in_specs=[whole((tm, d_model)), whole(conv0.shape), whole(h0.shape),
                  _resident(p5.shape), _resident(p10.shape), in_hbm, _resident(ws.shape), _resident(bmix.shape),
                  _resident(wgate.shape), in_hbm, in_hbm, in_hbm],
        out_specs=[whole((tm, d_model)), whole((ns, CONV_W - 1, w_b)), whole((ns, 1, w_b)), whole((tm, w_a))],
        out_shape=[jax.ShapeDtypeStruct((tm, d_model), _f32),
                   jax.ShapeDtypeStruct((ns, CONV_W - 1, w_b), _f32),
                   jax.ShapeDtypeStruct((ns, 1, w_b), _f32),
                   jax.ShapeDtypeStruct((tm, w_a), _f32)],
        scratch_shapes=_scratch(tm, ns, ts, d_model, wup.shape[-1], w_a, w_b, _mix_scratch_shape(chunk, w_a))
        + _weight_scratch(win, wout, wup, wdn, n_slots=SAMPLE_STAGE_SLOTS, own_staging=True),
        compiler_params=pltpu.CompilerParams(dimension_semantics=("arbitrary",),
                                             vmem_limit_bytes=VMEM_LIMIT_BYTES),
        name="sample_layer",
    )(x.reshape(tm, d_model), conv0, h0, *weights)
    return y.reshape(ns, ts, d_model), stc, sth, v.reshape(ns, ts, w_a)


def _block_diag(w):
    n, k, _ = w.shape
    eye = jnp.eye(n, dtype=w.dtype)
    return (eye[:, None, :, None] * w[:, :, None, :]).reshape(n * k, n * k)


def _prep_weights(l, chunk, norm1_g, w_in, ln_v_g, ln_v_b, w_s, b_s, conv_w, conv_b, w_a, b_a, w_x, b_x,
                  lam, gn_a_g, gn_b_g, w_out, norm2_g, w_up, w_down, normf_g):
    w_b = conv_b.shape[-1]
    d_model = norm1_g.shape[-1]
    hd_a = gn_a_g.shape[-1] // H_A
    rows5 = [ln_v_g[l], ln_v_b[l], conv_w[l, 0], conv_w[l, 1], conv_w[l, 2], conv_w[l, 3], conv_b[l],
             b_a[l], b_x[l], lam[l], gn_a_g[l], gn_b_g[l]]
    p5 = jnp.stack(rows5 + [jnp.zeros((w_b,), _f32)] * (16 - len(rows5)))
    p10 = jnp.stack([norm1_g[l], norm2_g[l], normf_g] + [jnp.zeros((d_model,), _f32)] * 5)
    bmix = jnp.repeat(b_s[l][:, :chunk].T, hd_a, axis=1)
    hh = H_B // 2
    wgate = jnp.stack([jnp.concatenate([_block_diag(w_a[l, j * hh:(j + 1) * hh]),
                                        _block_diag(w_x[l, j * hh:(j + 1) * hh])], axis=1)
                       for j in range(2)]).astype(_bf16)
    return (p5, p10, w_in[l], w_s[l], bmix, wgate, w_out[l], w_up[l], w_down[l])


def kernel(x_prompt, x_sample, state_conv_b, state_h_b, norm1_g, w_in, ln_v_g, ln_v_b, w_s, b_s, conv_w, conv_b,
           w_a, b_a, w_x, b_x, lam, gn_a_g, gn_b_g, w_out, norm2_g, w_up, w_down, normf_g):
    depth = w_in.shape[0]
    nb = x_prompt.shape[0]
    ns, ts, _ = x_sample.shape
    w_b = conv_b.shape[-1]
    params = (norm1_g, w_in, ln_v_g, ln_v_b, w_s, b_s, conv_w, conv_b, w_a, b_a, w_x, b_x, lam, gn_a_g, gn_b_g,
              w_out, norm2_g, w_up, w_down, normf_g)
    chunk_s = GMLP_CHUNK if ts % GMLP_CHUNK == 0 else ts
    hp, hs = x_prompt, x_sample
    conv_p, hlast_p, conv_s, hlast_s, v_s = [], [], [], [], []
    for l in range(depth):
        last = l == depth - 1
        hp, stc, sth = _run_prompt_layer(hp, _prep_weights(l, GMLP_CHUNK, *params), tm=PROMPT_TM, final_norm=last)
        conv_p.append(stc)
        hlast_p.append(sth.reshape(nb, w_b))
        hs, stc, sth, vv = _run_sample_layer(hs, state_conv_b[l], state_h_b[l], _prep_weights(l, chunk_s, *params),
                                             chunk=chunk_s, final_norm=last)
        conv_s.append(stc)
        hlast_s.append(sth.reshape(ns, w_b))
        v_s.append(vv)
    return (hp, hs, jnp.stack(conv_p), jnp.stack(hlast_p), jnp.stack(conv_s), jnp.stack(hlast_s), jnp.stack(v_s))
```

```python
import functools

import jax
import jax.numpy as jnp
from jax import lax
from jax.experimental import pallas as pl
from jax.experimental.pallas import tpu as pltpu

H_A = 8
H_B = 8
GMLP_CHUNK = 128
CONV_W = 4
LRU_C = 8.0
EPS = 1e-6

SUBLANES = 8
PACK_ROWS = 16
MXU_DIM = 256
FF_CHUNK = 1024
LN_CHAINS = 2
PROMPT_TM = 512
STAGE_SLOTS, STAGE_ROWS, STAGE_COLS = 4, 256, 1024
SAMPLE_STAGE_SLOTS = 16
VMEM_LIMIT_BYTES = 62 * 1024 * 1024

P_LNV_G, P_LNV_B, P_CONV_W, P_CONV_B, P_BA, P_BX, P_LAM, P_GNA, P_GNB = 0, 1, 2, 6, 7, 8, 9, 10, 11
P_NORM1, P_NORM2, P_NORMF = 0, 1, 2

_f32 = jnp.float32
_bf16 = jnp.bfloat16


def _rms(x, g):
    return x * lax.rsqrt(jnp.mean(x * x, axis=-1, keepdims=True) + EPS) * g


def _layernorm(x, g, b):
    mu = jnp.mean(x, axis=-1, keepdims=True)
    xc = x - mu
    var = jnp.mean(xc * xc, axis=-1, keepdims=True)
    return xc * lax.rsqrt(var + EPS) * g + b


def _sigmoid(x):
    return 0.5 * (1.0 + jnp.tanh(0.5 * x))


def _row(ref, r):
    return ref[r:r + 1, :]


def _load_weights_bf16(pairs, slots, sem):
    n_slots = len(slots)
    stage_rows, stage_cols = slots[0].shape
    chunks = [(src, dst, r0, c0) for src, dst in pairs
              for c0 in range(0, src.shape[1], stage_cols) for r0 in range(0, src.shape[0], stage_rows)]

    def chunk_copy(k):
        src, _, r0, c0 = chunks[k]
        return pltpu.make_async_copy(src.at[r0:r0 + stage_rows, c0:c0 + stage_cols],
                                     slots[k % n_slots], sem.at[k % n_slots])

    lookahead = n_slots - 1
    for k in range(min(lookahead, len(chunks))):
        chunk_copy(k).start()
    for k, (_, dst, r0, c0) in enumerate(chunks):
        if k + lookahead < len(chunks):
            chunk_copy(k + lookahead).start()
        chunk_copy(k).wait()
        dst[r0:r0 + stage_rows, c0:c0 + stage_cols] = slots[k % n_slots][...].astype(_bf16)
        if k + 1 == len(chunks) or chunks[k + 1][1] is not dst:
            yield


def _init_state(ws_ref, conv0_ref, h0_ref, xpad_ref, hcar_ref, wmixm_ref, *, chunk, init_proj=True, init_gate=True):
    if init_proj:
        n_tiles_a, _, kcat = wmixm_ref.shape
        per_tile = kcat // chunk
        row = lax.broadcasted_iota(jnp.int32, (chunk, chunk), 0)
        col = lax.broadcasted_iota(jnp.int32, (chunk, chunk), 1)
        for q in range(n_tiles_a):
            wmixm_ref[q] = jnp.concatenate(
                [jnp.where(col <= row, ws_ref[q * per_tile + j, 0:chunk, 0:chunk], 0.0) for j in range(per_tile)],
                axis=1).astype(_bf16)
        xpad_ref[:, 0:SUBLANES, :] = jnp.zeros_like(xpad_ref[:, 0:SUBLANES, :])
        if conv0_ref is not None:
            xpad_ref[:, SUBLANES - (CONV_W - 1):SUBLANES, :] = conv0_ref[...]
    if init_gate:
        if h0_ref is None:
            hcar_ref[...] = jnp.zeros_like(hcar_ref)
        else:
            hcar_ref[...] = jnp.broadcast_to(h0_ref[...][:, None, :], hcar_ref.shape)


def _write_conv_state(stc_ref, xpad_ref):
    stc_ref[...] = xpad_ref[:, SUBLANES - (CONV_W - 1):SUBLANES, :]


def _write_h_state(sth_ref, hcar_ref):
    sth_ref[...] = hcar_ref[:, SUBLANES - 1:SUBLANES, :]


def _proj_phases(x_ref, p5_ref, p10_ref, win_ref, wgate_ref, v_ref, wmixm_ref,
                 xn_ref, z_ref, xpad_ref, gu_ref, gg_ref, mixed_ref, xc_ref, pre_ref,
                 *, tm, chunk, seg, never=None):
    nseg = tm // seg
    nchunk = tm // chunk
    w_a = gu_ref.shape[-1]
    w_b = p5_ref.shape[-1]
    hd_a = w_a // H_A
    heads_per_tile = MXU_DIM // hd_a
    n_tiles_a = w_a // MXU_DIM

    x = x_ref[...]
    xn_ref[...] = (x * _row(p10_ref, P_NORM1)).astype(_bf16)
    r1 = lax.rsqrt(jnp.mean(x * x, axis=-1, keepdims=True) + EPS)
    d_mix = w_a + w_b
    z_ref[:, 0:d_mix] = jnp.dot(xn_ref[...], win_ref[:, 0:d_mix], preferred_element_type=_f32) * r1
    yield
    z_ref[:, d_mix:2 * d_mix] = jnp.dot(xn_ref[...], win_ref[:, d_mix:2 * d_mix], preferred_element_type=_f32) * r1
    yield

    gu_ref[...] = z_ref[:, 0:w_a]
    gg_ref[...] = z_ref[:, 2 * w_a + w_b:2 * w_a + 2 * w_b]

    v_blocks = []
    for r0 in range(0, tm, PACK_ROWS):
        zv = z_ref[r0:r0 + PACK_ROWS, w_a:2 * w_a]
        if never is not None and len(v_blocks) >= LN_CHAINS:
            zv = jnp.where(never, v_blocks[-LN_CHAINS], zv)
        v_blocks.append(_layernorm(jax.nn.gelu(zv), _row(p5_ref, P_LNV_G), _row(p5_ref, P_LNV_B)))
    v = jnp.concatenate(v_blocks, axis=0)
    if v_ref is not None:
        v_ref[...] = v
    lane = lax.broadcasted_iota(jnp.int32, (chunk, MXU_DIM), 1)
    for c in range(nchunk):
        rows = slice(c * chunk, (c + 1) * chunk)
        for q in range(n_tiles_a):
            cols = slice(q * MXU_DIM, (q + 1) * MXU_DIM)
            vq = v[rows, cols]
            rhs = jnp.concatenate(
                [jnp.where((lane >= hd_a * j) & (lane < hd_a * (j + 1)), vq, 0.0)
                 for j in range(heads_per_tile)], axis=0).astype(_bf16)
            mixed_ref[rows, cols] = jnp.dot(wmixm_ref[q], rhs, preferred_element_type=_f32)

    for s in range(nseg):
        xpad_ref[s, SUBLANES:SUBLANES + seg, :] = z_ref[s * seg:(s + 1) * seg, 2 * w_a:2 * w_a + w_b]
    xcs = []
    for s in range(nseg):
        acc = _row(p5_ref, P_CONV_B)
        for k in range(CONV_W):
            off = SUBLANES - (CONV_W - 1) + k
            acc = acc + xpad_ref[s, off:off + seg, :] * _row(p5_ref, P_CONV_W + k)
        xcs.append(acc)
        xpad_ref[s, 0:SUBLANES, :] = xpad_ref[s, seg:seg + SUBLANES, :]
    xc = jnp.concatenate(xcs, axis=0) if nseg > 1 else xcs[0]
    xc_ref[...] = xc
    xcb = xc.astype(_bf16)
    half = w_b // 2
    for j in range(2):
        res = jnp.dot(xcb[:, j * half:(j + 1) * half], wgate_ref[j], preferred_element_type=_f32)
        pre_ref[:, j * half:(j + 1) * half] = res[:, :half]
        pre_ref[:, w_b + j * half:w_b + (j + 1) * half] = res[:, half:]


def _gate_stage(p5_ref, bmix_ref, gu_ref, gg_ref, mixed_ref, xc_ref, pre_ref, hcar_ref, cat_ref,
                *, tm, chunk, seg, first_tile, never=None):
    nseg = tm // seg
    w_a = gu_ref.shape[-1]
    w_b = p5_ref.shape[-1]

    rowid = lax.broadcasted_iota(jnp.int32, (SUBLANES, w_b), 0)
    for s in range(nseg):
        hp = hcar_ref[s]
        link = None
        for r0 in range(s * seg, (s + 1) * seg, PACK_ROWS):
            ybs = []
            for g0 in range(r0, r0 + PACK_ROWS, SUBLANES):
                rows = slice(g0, g0 + SUBLANES)
                pre_r = pre_ref[rows, 0:w_b]
                pre_i = pre_ref[rows, w_b:2 * w_b]
                g = gg_ref[rows, :]
                if never is not None and link is not None:
                    pre_r = jnp.where(never, link[0], pre_r)
                    pre_i = jnp.where(never, link[1], pre_i)
                    g = jnp.where(never, link[0], g)
                r = _sigmoid(pre_r + _row(p5_ref, P_BA))
                i = _sigmoid(pre_i + _row(p5_ref, P_BX))
                log_a = -LRU_C * r * jax.nn.softplus(-_row(p5_ref, P_LAM))
                ag = jnp.exp(log_a)
                th = jnp.tanh(log_a)
                n = -2.0 * th
                mult = jnp.where(n > 0.0, n * lax.rsqrt(n * (1.0 - th)), 0.0)
                if first_tile is not None and g0 == s * seg:
                    mult = jnp.where((rowid == 0) & first_tile, 1.0, mult)
                bg = mult * (i * xc_ref[rows, :])
                for d in (1, 2, 4):
                    keep = rowid >= d
                    a_sh = jnp.where(keep, pltpu.roll(ag, d, 0), 1.0)
                    b_sh = jnp.where(keep, pltpu.roll(bg, d, 0), 0.0)
                    bg = bg + ag * b_sh
                    ag = ag * a_sh
                hg = bg + ag * hp
                hp = jnp.broadcast_to(hg[SUBLANES - 1:SUBLANES, :], (SUBLANES, w_b))
                link = (hp, hp)
                ybs.append(hg * jax.nn.gelu(g))
            rows = slice(r0, r0 + PACK_ROWS)
            yb = _rms(jnp.concatenate(ybs, axis=0), _row(p5_ref, P_GNB))
            cat_ref[rows, w_a:w_a + w_b] = yb.astype(_bf16)
            u = gu_ref[rows, :]
            if never is not None:
                u = jnp.where(never, yb, u)
            ya = jax.nn.gelu(u) * (mixed_ref[rows, :] + bmix_ref[r0 % chunk:r0 % chunk + PACK_ROWS, :])
            link = (ya[0:SUBLANES], ya[SUBLANES:PACK_ROWS])
            cat_ref[rows, 0:w_a] = _rms(ya, _row(p5_ref, P_GNA)).astype(_bf16)
        hcar_ref[s] = hp


def _down_stage(h1_ref, hid_ref, r2_ref, p10_ref, wdn_ref, y_ref, *, final_norm, never=None):
    down = jnp.dot(hid_ref[...], wdn_ref[...], preferred_element_type=_f32)
    tm = down.shape[0]
    prev = None
    for r0 in range(0, tm, PACK_ROWS):
        rows = slice(r0, r0 + PACK_ROWS)
        out = h1_ref[rows, :] + down[rows, :] * r2_ref[rows, 0:1]
        if never is not None and prev is not None:
            out = jnp.where(never, prev, out)
        prev = _rms(out, _row(p10_ref, P_NORMF)) if final_norm else out
        y_ref[rows, :] = prev


def _up_phases(x_ref, cat_ref, p10_ref, wout_ref, wup_ref, h1_ref, hn_ref, hid_ref, r2_ref, anchor=None):
    d_ff = wup_ref.shape[-1]
    x = x_ref[...]
    if anchor is not None:
        never, anchored_ref = anchor
        head = jnp.where(never, anchored_ref[anchored_ref.shape[0] - PACK_ROWS:, :], x[0:PACK_ROWS])
        x = jnp.concatenate([head, x[PACK_ROWS:]], axis=0)
    h1_ref[...] = x + jnp.dot(cat_ref[...], wout_ref[...], preferred_element_type=_f32)
    yield
    h1 = h1_ref[...]
    hn_ref[...] = (h1 * _row(p10_ref, P_NORM2)).astype(_bf16)
    r2_ref[...] = jnp.broadcast_to(1.0 / (jnp.mean(h1 * h1, axis=-1, keepdims=True) + EPS), r2_ref.shape)
    for c in range(d_ff // FF_CHUNK):
        if c:
            yield
        cols = slice(c * FF_CHUNK, (c + 1) * FF_CHUNK)
        up = jnp.dot(hn_ref[...], wup_ref[:, cols], preferred_element_type=_f32)
        hid_ref[:, cols] = jnp.square(jnp.maximum(up.astype(_bf16), 0.0))


_PIPELINE_ORDER = "PPUUUUUP"
_PIPELINE_DEPTH = 2


def _pipelined_body(x_ref, xres_ref, p5_ref, p10_ref, win_hbm, ws_ref, bmix_ref, wgate_ref,
                    wout_hbm, wup_hbm, wdn_hbm, y_ref, stc_ref, sth_ref,
                    xn_ref, z_ref, xpad_ref, gu_ref, gg_ref, mixed_ref, xc_ref, pre_ref, hcar_ref, cat_ref,
                    h1_ref, hn_ref, hid_ref, r2_ref, wmixm_ref, win_ref, wout_ref, wup_ref, wdn_ref,
                    dma_sem, *, tm, chunk, n_tiles, tiles_per_seq, final_norm):
    s = pl.program_id(0)
    t_proj = jnp.minimum(s, n_tiles - 1) % tiles_per_seq
    t_gate = jnp.clip(s - 1, 0, n_tiles - 1) % tiles_per_seq

    @pl.when(s == 0)
    def _():
        for ref in (h1_ref, hid_ref, r2_ref):
            ref[...] = jnp.zeros_like(ref)
        slots = [z_ref.at[r0:r0 + STAGE_ROWS, c0:c0 + STAGE_COLS]
                 for c0 in range(0, z_ref.shape[1], STAGE_COLS) for r0 in range(0, tm, STAGE_ROWS)]
        for _ in _load_weights_bf16(((win_hbm, win_ref), (wout_hbm, wout_ref), (wup_hbm, wup_ref),
                                     (wdn_hbm, wdn_ref)), slots[:STAGE_SLOTS], dma_sem):
            pass

    @pl.when(t_proj == 0)
    def _():
        _init_state(ws_ref, None, None, xpad_ref, hcar_ref, wmixm_ref, chunk=chunk, init_gate=False)

    @pl.when(t_gate == 0)
    def _():
        _init_state(ws_ref, None, None, xpad_ref, hcar_ref, wmixm_ref, chunk=chunk, init_proj=False)

    def run_stages(with_proj, with_gate_up, with_down):
        never = s < 0 if with_proj or with_gate_up else None
        up = proj = iter(())
        if with_gate_up:
            _gate_stage(p5_ref, bmix_ref, gu_ref, gg_ref, mixed_ref, xc_ref, pre_ref, hcar_ref, cat_ref,
                        tm=tm, chunk=chunk, seg=tm, first_tile=(t_gate == 0), never=never)
        if with_down:
            _down_stage(h1_ref, hid_ref, r2_ref, p10_ref, wdn_ref, y_ref, final_norm=final_norm, never=never)
        if with_gate_up:
            up = _up_phases(xres_ref, cat_ref, p10_ref, wout_ref, wup_ref, h1_ref, hn_ref, hid_ref, r2_ref,
                            anchor=(never, y_ref) if with_down else None)
        if with_proj:
            proj = _proj_phases(x_ref, p5_ref, p10_ref, win_ref, wgate_ref, None, wmixm_ref,
                                xn_ref, z_ref, xpad_ref, gu_ref, gg_ref, mixed_ref, xc_ref, pre_ref,
                                tm=tm, chunk=chunk, seg=tm, never=never)
        for who in _PIPELINE_ORDER:
            next(proj if who == "P" else up, None)
        assert next(proj, "done") == "done" and next(up, "done") == "done"

    filling, last = s == 0, s == n_tiles + 1
    pl.when(filling)(functools.partial(run_stages, True, False, False))
    pl.when(last)(functools.partial(run_stages, False, False, True))
    pl.when(~(filling | last))(functools.partial(run_stages, True, True, True))

    @pl.when((t_proj == tiles_per_seq - 1) & (s < n_tiles))
    def _():
        _write_conv_state(stc_ref, xpad_ref)

    @pl.when((t_gate == tiles_per_seq - 1) & (s >= 1) & (s <= n_tiles))
    def _():
        _write_h_state(sth_ref, hcar_ref)


def _single_body(x_ref, conv0_ref, h0_ref, p5_ref, p10_ref, win_hbm, ws_ref, bmix_ref, wgate_ref,
                 wout_hbm, wup_hbm, wdn_hbm, y_ref, stc_ref, sth_ref, v_ref,
                 xn_ref, z_ref, xpad_ref, gu_ref, gg_ref, mixed_ref, xc_ref, pre_ref, hcar_ref, cat_ref,
                 h1_ref, hn_ref, hid_ref, r2_ref, wmixm_ref, win_ref, wout_ref, wup_ref, wdn_ref,
                 stage_ref, dma_sem, *, tm, chunk, seg, final_norm):
    weights = _load_weights_bf16(((win_hbm, win_ref), (wout_hbm, wout_ref), (wup_hbm, wup_ref), (wdn_hbm, wdn_ref)),
                                 [stage_ref.at[k] for k in range(stage_ref.shape[0])], dma_sem)
    next(weights)
    _init_state(ws_ref, conv0_ref, h0_ref, xpad_ref, hcar_ref, wmixm_ref, chunk=chunk)
    for _ in _proj_phases(x_ref, p5_ref, p10_ref, win_ref, wgate_ref, v_ref, wmixm_ref,
                          xn_ref, z_ref, xpad_ref, gu_ref, gg_ref, mixed_ref, xc_ref, pre_ref,
                          tm=tm, chunk=chunk, seg=seg):
        pass
    _gate_stage(p5_ref, bmix_ref, gu_ref, gg_ref, mixed_ref, xc_ref, pre_ref, hcar_ref, cat_ref,
                tm=tm, chunk=chunk, seg=seg, first_tile=None)
    next(weights)
    up = _up_phases(x_ref, cat_ref, p10_ref, wout_ref, wup_ref, h1_ref, hn_ref, hid_ref, r2_ref)
    next(up)
    next(weights)
    for _ in up:
        pass
    next(weights)
    _down_stage(h1_ref, hid_ref, r2_ref, p10_ref, wdn_ref, y_ref, final_norm=final_norm)
    _write_conv_state(stc_ref, xpad_ref)
    _write_h_state(sth_ref, hcar_ref)


def _resident(shape):
    nd = len(shape)
    return pl.BlockSpec(shape, lambda i: (0,) * nd, pipeline_mode=pl.Buffered(1))


def _scratch(tm, nseg, seg, d_model, d_ff, w_a, w_b, wmix_shape):
    return [
        pltpu.VMEM((tm, d_model), _bf16),
        pltpu.VMEM((tm, 2 * w_a + 2 * w_b), _f32),
        pltpu.VMEM((nseg, seg + SUBLANES, w_b), _f32),
        pltpu.VMEM((tm, w_a), _f32),
        pltpu.VMEM((tm, w_b), _f32),
        pltpu.VMEM((tm, w_a), _f32),
        pltpu.VMEM((tm, w_b), _f32),
        pltpu.VMEM((tm, 2 * w_b), _f32),
        pltpu.VMEM((nseg, SUBLANES, w_b), _f32),
        pltpu.VMEM((tm, w_a + w_b), _bf16),
        pltpu.VMEM((tm, d_model), _f32),
        pltpu.VMEM((tm, d_model), _bf16),
        pltpu.VMEM((tm, d_ff), _bf16),
        pltpu.VMEM((tm, 128), _f32),
        pltpu.VMEM(wmix_shape, _bf16),
    ]


def _weight_scratch(*big_weights, n_slots, own_staging):
    staging = [pltpu.VMEM((n_slots, STAGE_ROWS, STAGE_COLS), _f32)] if own_staging else []
    return [pltpu.VMEM(w.shape, _bf16) for w in big_weights] + staging + [pltpu.SemaphoreType.DMA((n_slots,))]


def _mix_scratch_shape(chunk, w_a):
    heads_per_tile = MXU_DIM // (w_a // H_A)
    return (H_A // heads_per_tile, chunk, heads_per_tile * chunk)


def _run_prompt_layer(x, weights, *, tm, final_norm):
    p5, p10, win, ws, bmix, wgate, wout, wup, wdn = weights
    nb, T, d_model = x.shape
    w_b = p5.shape[-1]
    w_a = bmix.shape[-1]
    tiles_per_seq = T // tm
    n_tiles = nb * tiles_per_seq
    x2 = x.reshape(nb * T, d_model)
    body = functools.partial(_pipelined_body, tm=tm, chunk=GMLP_CHUNK, n_tiles=n_tiles,
                             tiles_per_seq=tiles_per_seq, final_norm=final_norm)
    tile = lambda lag: (lambda s: (jnp.clip(s - lag, 0, n_tiles - 1), 0))
    seq = lambda lag: (lambda s: (jnp.clip(s - lag, 0, n_tiles - 1) // tiles_per_seq, 0, 0))
    in_hbm = pl.BlockSpec(memory_space=pl.ANY)
    in_specs = [
        pl.BlockSpec((tm, d_model), tile(0)),
        pl.BlockSpec((tm, d_model), tile(1)),
        _resident(p5.shape), _resident(p10.shape), in_hbm, _resident(ws.shape), _resident(bmix.shape),
        _resident(wgate.shape), in_hbm, in_hbm, in_hbm,
    ]
    y, stc, sth = pl.pallas_call(
        body,
        grid=(n_tiles + _PIPELINE_DEPTH,),
        in_specs=in_specs,
        out_specs=[pl.BlockSpec((tm, d_model), tile(_PIPELINE_DEPTH)),
                   pl.BlockSpec((1, CONV_W - 1, w_b), seq(0)), pl.BlockSpec((1, 1, w_b), seq(1))],
        out_shape=[jax.ShapeDtypeStruct((nb * T, d_model), _f32),
                   jax.ShapeDtypeStruct((nb, CONV_W - 1, w_b), _f32),
                   jax.ShapeDtypeStruct((nb, 1, w_b), _f32)],
        scratch_shapes=_scratch(tm, 1, tm, d_model, wup.shape[-1], w_a, w_b, _mix_scratch_shape(GMLP_CHUNK, w_a))
        + _weight_scratch(win, wout, wup, wdn, n_slots=STAGE_SLOTS, own_staging=False),
        compiler_params=pltpu.CompilerParams(dimension_semantics=("arbitrary",),
                                             vmem_limit_bytes=VMEM_LIMIT_BYTES),
        name="prompt_layer",
    )(x2, x2, *weights)
    return y.reshape(nb, T, d_model), stc, sth


def _run_sample_layer(x, conv0, h0, weights, *, chunk, final_norm):
    p5, p10, win, ws, bmix, wgate, wout, wup, wdn = weights
    ns, ts, d_model = x.shape
    w_b = p5.shape[-1]
    w_a = bmix.shape[-1]
    tm = ns * ts
    body = functools.partial(_single_body, tm=tm, chunk=chunk, seg=ts, final_norm=final_norm)
    whole = lambda shape: pl.BlockSpec(shape, lambda i: (0,) * len(shape))
    in_hbm = pl.BlockSpec(memory_space=pl.ANY)
    y, stc, sth, v = pl.pallas_call(
        body,
        grid=(1,),
        in_specs=[whole((tm, d_model)), whole(conv0.shape), whole(h0.shape),
                  _resident(p5.shape), _resident(p10.shape), in_hbm, _resident(ws.shape), _resident(bmix.shape),
                  _resident(wgate.shape), in_hbm, in_hbm, in_hbm],
        out_specs=[whole((tm, d_model)), whole((ns, CONV_W - 1, w_b)), whole((ns, 1, w_b)), whole((tm, w_a))],
        out_shape=[jax.ShapeDtypeStruct((tm, d_model), _f32),
                   jax.ShapeDtypeStruct((ns, CONV_W - 1, w_b), _f32),
                   jax.ShapeDtypeStruct((ns, 1, w_b), _f32),
                   jax.ShapeDtypeStruct((tm, w_a), _f32)],
        scratch_shapes=_scratch(tm, ns, ts, d_model, wup.shape[-1], w_a, w_b, _mix_scratch_shape(chunk, w_a))
        + _weight_scratch(win, wout, wup, wdn, n_slots=SAMPLE_STAGE_SLOTS, own_staging=True),
        compiler_params=pltpu.CompilerParams(dimension_semantics=("arbitrary",),
                                             vmem_limit_bytes=VMEM_LIMIT_BYTES),
        name="sample_layer",
    )(x.reshape(tm, d_model), conv0, h0, *weights)
    return y.reshape(ns, ts, d_model), stc, sth, v.reshape(ns, ts, w_a)


def _block_diag(w):
    n, k, _ = w.shape
    eye = jnp.eye(n, dtype=w.dtype)
    return (eye[:, None, :, None] * w[:, :, None, :]).reshape(n * k, n * k)


def _prep_weights(l, chunk, norm1_g, w_in, ln_v_g, ln_v_b, w_s, b_s, conv_w, conv_b, w_a, b_a, w_x, b_x,
                  lam, gn_a_g, gn_b_g, w_out, norm2_g, w_up, w_down, normf_g):
    w_b = conv_b.shape[-1]
    d_model = norm1_g.shape[-1]
    hd_a = gn_a_g.shape[-1] // H_A
    rows5 = [ln_v_g[l], ln_v_b[l], conv_w[l, 0], conv_w[l, 1], conv_w[l, 2], conv_w[l, 3], conv_b[l],
             b_a[l], b_x[l], lam[l], gn_a_g[l], gn_b_g[l]]
    p5 = jnp.stack(rows5 + [jnp.zeros((w_b,), _f32)] * (16 - len(rows5)))
    p10 = jnp.stack([norm1_g[l], norm2_g[l], normf_g] + [jnp.zeros((d_model,), _f32)] * 5)
    bmix = jnp.repeat(b_s[l][:, :chunk].T, hd_a, axis=1)
    hh = H_B // 2
    wgate = jnp.stack([jnp.concatenate([_block_diag(w_a[l, j * hh:(j + 1) * hh]),
                                        _block_diag(w_x[l, j * hh:(j + 1) * hh])], axis=1)
                       for j in range(2)]).astype(_bf16)
    return (p5, p10, w_in[l], w_s[l], bmix, wgate, w_out[l], w_up[l], w_down[l])


def kernel(x_prompt, x_sample, state_conv_b, state_h_b, norm1_g, w_in, ln_v_g, ln_v_b, w_s, b_s, conv_w, conv_b,
           w_a, b_a, w_x, b_x, lam, gn_a_g, gn_b_g, w_out, norm2_g, w_up, w_down, normf_g):
    depth = w_in.shape[0]
    nb = x_prompt.shape[0]
    ns, ts, _ = x_sample.shape
    w_b = conv_b.shape[-1]
    params = (norm1_g, w_in, ln_v_g, ln_v_b, w_s, b_s, conv_w, conv_b, w_a, b_a, w_x, b_x, lam, gn_a_g, gn_b_g,
              w_out, norm2_g, w_up, w_down, normf_g)
    chunk_s = GMLP_CHUNK if ts % GMLP_CHUNK == 0 else ts
    hp, hs = x_prompt, x_sample
    conv_p, hlast_p, conv_s, hlast_s, v_s = [], [], [], [], []
    for l in range(depth):
        last = l == depth - 1
        hp, stc, sth = _run_prompt_layer(hp, _prep_weights(l, GMLP_CHUNK, *params), tm=PROMPT_TM, final_norm=last)
        conv_p.append(stc)
        hlast_p.append(sth.reshape(nb, w_b))
        hs, stc, sth, vv = _run_sample_layer(hs, state_conv_b[l], state_h_b[l], _prep_weights(l, chunk_s, *params),
                                             chunk=chunk_s, final_norm=last)
        conv_s.append(stc)
        hlast_s.append(sth.reshape(ns, w_b))
        v_s.append(vv)
    return (hp, hs, jnp.stack(conv_p), jnp.stack(hlast_p), jnp.stack(conv_s), jnp.stack(hlast_s), jnp.stack(v_s))
```
